```python
import math
import jax
import jax.numpy as jnp
from jax import lax
import numpy as np

D_MODEL = 1024
BATCH = 2
SEQ = 8192
DEPTH = 4

GRID_W = 64
CTX_LEN = 256

HEAD_DIM = 64
ATTN_HEADS = 8
ATTN_KV_HEADS = 2
ATTN_GROUP = ATTN_HEADS // ATTN_KV_HEADS
ATTN_WIDTH = ATTN_HEADS * HEAD_DIM
KV_WIDTH = ATTN_KV_HEADS * HEAD_DIM
ATTN_SCALE = HEAD_DIM ** -0.5
Q_BLOCK = 128
ROPE_THETA = 10000.0

HYENA_WIDTH = 256
HYENA_ORDER = 2
SHORT_CONV = 3
FILTER_BANDS = 16
FILTER_EMB = 1 + 2 * FILTER_BANDS
FILTER_HIDDEN = 64
HYENA_TARGET = 1e-2
HYENA_FAST_DECAY_PCT = 0.3
HYENA_SLOW_DECAY_PCT = 1.5
HYENA_MIN_DECAY = math.log(HYENA_TARGET) / HYENA_SLOW_DECAY_PCT
HYENA_MAX_DECAY = math.log(HYENA_TARGET) / HYENA_FAST_DECAY_PCT

GLA_HEADS = 4
GLA_DK = 32
GLA_DV = 64
GLA_K_WIDTH = GLA_HEADS * GLA_DK
GLA_V_WIDTH = GLA_HEADS * GLA_DV
GLA_GATE_RANK = 16
GLA_GATE_TAU = 16.0
GLA_CHUNK = 64

MIX_WIDTH = ATTN_WIDTH + HYENA_WIDTH + GLA_V_WIDTH
FFN_HIDDEN = -(-8 * D_MODEL // (3 * 256)) * 256
N_MOD = 6
NORM_EPS = 1e-6

IN_SPLITS = (ATTN_WIDTH, KV_WIDTH, KV_WIDTH, 3 * HYENA_WIDTH, GLA_K_WIDTH, GLA_K_WIDTH,
             GLA_V_WIDTH, GLA_V_WIDTH, GLA_GATE_RANK, GLA_GATE_RANK)
IN_WIDTH = sum(IN_SPLITS)

kernel_name = "hybrid_gqa_hyena_gla_prefix_dit"


def rms_norm(x, g):
    xf = x.astype(jnp.float32)
    y = xf * lax.rsqrt(jnp.mean(jnp.square(xf), axis=-1, keepdims=True) + NORM_EPS)
    return (y * g.astype(jnp.float32)).astype(x.dtype)


def head_rms_norm(x, g):
    shp = x.shape
    xh = x.reshape(shp[:-1] + (shp[-1] // HEAD_DIM, HEAD_DIM))
    return rms_norm(xh, g.reshape(-1, HEAD_DIM)).reshape(shp)


def modulate(x, shift, scale):
    return x * (1.0 + scale) + shift


def split_columns(p):
    parts, start = [], 0
    for width in IN_SPLITS:
        parts.append(p[..., start:start + width])
        start += width
    return parts


def axial_rope_tables(n_tokens):
    rows = n_tokens // GRID_W
    row = jnp.broadcast_to(jnp.arange(rows, dtype=jnp.float32)[:, None], (rows, GRID_W)).reshape(-1)
    col = jnp.broadcast_to(jnp.arange(GRID_W, dtype=jnp.float32)[None, :], (rows, GRID_W)).reshape(-1)
    n_freq = HEAD_DIM // 4
    inv_freq = jnp.power(ROPE_THETA, -jnp.arange(n_freq, dtype=jnp.float32) / n_freq)
    ang = jnp.concatenate([row[:, None] * inv_freq, col[:, None] * inv_freq], axis=-1)
    return jnp.cos(ang), jnp.sin(ang)


def apply_rope(x, cos, sin):
    half = HEAD_DIM // 2
    xf = x.astype(jnp.float32)
    x1, x2 = xf[..., :half], xf[..., half:]
    c, s = cos[None, :, None, :], sin[None, :, None, :]
    return jnp.concatenate([x1 * c - x2 * s, x1 * s + x2 * c], axis=-1).astype(x.dtype)


def gqa_attend(q, k, v):
    b, lq = q.shape[:2]
    qg = q.reshape(b, lq, ATTN_KV_HEADS, ATTN_GROUP, HEAD_DIM)
    s = jnp.einsum('bqkgd,bskd->bkgqs', qg, k, preferred_element_type=jnp.float32)
    p = jax.nn.softmax(s, axis=-1).astype(v.dtype)
    o = jnp.einsum('bkgqs,bskd->bqkgd', p, v)
    return o.reshape(b, lq, ATTN_WIDTH)


def latent_block_attention(q_lat, k_all, v_all):
    b, n = q_lat.shape[:2]
    nb = n // Q_BLOCK
    qb = q_lat.reshape(b, nb, Q_BLOCK, ATTN_HEADS, HEAD_DIM).swapaxes(0, 1)
    ob = lax.map(lambda blk: gqa_attend(blk, k_all, v_all), qb)
    return ob.swapaxes(0, 1).reshape(b, n, ATTN_WIDTH)


def short_conv(u, w, b):
    pad = SHORT_CONV // 2
    y = lax.conv_general_dilated(u, w[:, None, :].astype(u.dtype), window_strides=(1,), padding=[(pad, pad)],
                                 dimension_numbers=('NWC', 'WIO', 'NWC'), feature_group_count=u.shape[-1])
    return y + b


def hyena_filter_spectra(n, w1, b1, w2, b2, w3, freq):
    f32 = jnp.float32
    t = jnp.linspace(0.0, 1.0, n, dtype=f32)[:, None]
    omega = (2.0 * math.pi / n) * jnp.arange(n, dtype=f32)
    bands = jnp.linspace(1e-4, FILTER_BANDS - 1, FILTER_BANDS, dtype=f32)
    phase = omega[:, None] * bands[None, :]
    z = jnp.concatenate([t, jnp.cos(phase), -jnp.sin(phase)], axis=-1)
    fr = freq.astype(f32)
    h = jnp.sin(fr * (z @ w1.astype(f32) + b1.astype(f32)))
    h = jnp.sin(fr * (h @ w2.astype(f32) + b2.astype(f32)))
    h = (h @ w3.astype(f32)).reshape(n, HYENA_ORDER, 2, HYENA_WIDTH)
    deltas = jnp.abs(jnp.linspace(HYENA_MIN_DECAY, HYENA_MAX_DECAY, HYENA_WIDTH, dtype=f32))
    window = jnp.exp(-t * deltas[None, :])
    h = h * window[:, None, None, :]
    h_pos, h_neg = h[:, :, 0], h[:, :, 1]
    h_circ = jnp.concatenate([h_pos, jnp.zeros((1, HYENA_ORDER, HYENA_WIDTH), f32), h_neg[:0:-1]], axis=0)
    return jnp.fft.rfft(h_circ, axis=0)


def long_conv(z, spec, bias):
    n = z.shape[1]
    y = jnp.fft.irfft(jnp.fft.rfft(z, n=2 * n, axis=1) * spec[None], n=2 * n, axis=1)[:, :n]
    return y + z * bias


def hyena_mix(proj, conv_w, conv_b, spec, bias):
    u = short_conv(proj, conv_w, conv_b).astype(jnp.float32)
    v, x1, x2 = u[..., :HYENA_WIDTH], u[..., HYENA_WIDTH:2 * HYENA_WIDTH], u[..., 2 * HYENA_WIDTH:]
    bias = bias.astype(jnp.float32)
    z = x1 * long_conv(v, spec[:, 0], bias[0])
    z = x2 * long_conv(z, spec[:, 1], bias[1])
    return z.astype(proj.dtype)


def gla_chunked(q, k, v, log_a, s0):
    b, n, h, _ = q.shape
    dv = v.shape[-1]
    nc = n // GLA_CHUNK

    def to_chunks(t):
        return t.reshape(b, nc, GLA_CHUNK, h, t.shape[-1]).transpose(0, 1, 3, 2, 4)

    qc, kc, vc, gc = to_chunks(q), to_chunks(k), to_chunks(v), to_chunks(log_a)
    g_cum = jnp.cumsum(gc, axis=3)
    g_last = g_cum[:, :, :, -1:, :]
    q_dec = qc * jnp.exp(g_cum)
    k_dec = kc * jnp.exp(-g_cum)
    k_tail = kc * jnp.exp(g_last - g_cum)
    lower = jnp.tril(jnp.ones((GLA_CHUNK, GLA_CHUNK), dtype=bool))
    scores = jnp.where(lower, jnp.einsum('bnhid,bnhjd->bnhij', q_dec, k_dec), 0.0)
    o_intra = jnp.einsum('bnhij,bnhjv->bnhiv', scores, vc)
    kv = jnp.einsum('bnhjd,bnhjv->bnhdv', k_tail, vc)
    decay = jnp.exp(g_last[:, :, :, 0, :])

    def step(state, inp):
        d, kv_c = inp
        return d[..., None] * state + kv_c, state

    s_final, s_prev = lax.scan(step, s0, (jnp.moveaxis(decay, 1, 0), jnp.moveaxis(kv, 1, 0)))
    s_prev = jnp.moveaxis(s_prev, 0, 1)
    o_inter = jnp.einsum('bnhid,bnhdv->bnhiv', q_dec, s_prev)
    o = (o_intra + o_inter).transpose(0, 1, 3, 2, 4).reshape(b, n, h, dv)
    return o, s_final


def gla_bidirectional(q_l, k_l, v_l, af_l, ab_l, q_c, k_c, v_c, af_c, ab_c):
    b = q_l.shape[0]
    s0 = jnp.zeros((b, GLA_HEADS, GLA_DK, GLA_DV), jnp.float32)
    rev = lambda t: t[:, ::-1]
    oc_f, sc_f = gla_chunked(q_c, k_c, v_c, af_c, s0)
    oc_b, sc_b = gla_chunked(rev(q_c), rev(k_c), rev(v_c), rev(ab_c), s0)
    ol_f, _ = gla_chunked(q_l, k_l, v_l, af_l, sc_f)
    ol_b, _ = gla_chunked(rev(q_l), rev(k_l), rev(v_l), rev(ab_l), sc_b)
    return ol_f + rev(ol_b), oc_f + rev(oc_b)


def mixer_inputs(u, w_in, q_norm_g, k_norm_g, gla_gate_w, gla_gate_b):
    b, n = u.shape[:2]
    f32 = jnp.float32
    aq, ak, av, hy, gq, gk, gv, gr, gaf, gab = split_columns(u @ w_in)
    q = rms_norm(aq.reshape(b, n, ATTN_HEADS, HEAD_DIM), q_norm_g)
    k = rms_norm(ak.reshape(b, n, ATTN_KV_HEADS, HEAD_DIM), k_norm_g)
    v = av.reshape(b, n, ATTN_KV_HEADS, HEAD_DIM)
    gla_q = gq.reshape(b, n, GLA_HEADS, GLA_DK).astype(f32) * (GLA_DK ** -0.5)
    gla_k = gk.reshape(b, n, GLA_HEADS, GLA_DK).astype(f32)
    gla_v = gv.reshape(b, n, GLA_HEADS, GLA_DV).astype(f32)

    def log_gate(low_rank, d):
        zg = (low_rank @ gla_gate_w[d] + gla_gate_b[d]).astype(f32)
        return (jax.nn.log_sigmoid(zg) / GLA_GATE_TAU).reshape(b, n, GLA_HEADS, GLA_DK)

    gla = (gla_q, gla_k, gla_v, log_gate(gaf, 0), log_gate(gab, 1))
    return q, k, v, hy, gla, gr


def merge_heads(attn_o, hy_o, gla_o, gate_r, out_norm_g, w_out):
    dt = attn_o.dtype
    gla_flat = gla_o.reshape(gla_o.shape[:2] + (GLA_V_WIDTH,))
    y = jnp.concatenate([
        head_rms_norm(attn_o, out_norm_g[:ATTN_WIDTH]),
        head_rms_norm(hy_o, out_norm_g[ATTN_WIDTH:ATTN_WIDTH + HYENA_WIDTH]),
        (head_rms_norm(gla_flat, out_norm_g[ATTN_WIDTH + HYENA_WIDTH:]) * jax.nn.silu(gate_r.astype(jnp.float32))).astype(dt),
    ], axis=-1)
    return y @ w_out


def swiglu(u, w1, w3, w2):
    return (jax.nn.silu(u @ w1) * (u @ w3)) @ w2


def hybrid_layer(h_lat, h_ctx, mod_lat, mod_ctx, rope_cos, rope_sin, norm1_g, w_in, q_norm_g, k_norm_g,
                 hy_conv_w, hy_conv_b, filt_w1, filt_b1, filt_w2, filt_b2, filt_w3, filt_freq, hy_bias,
                 gla_gate_w, gla_gate_b, out_norm_g, w_out, norm2_g, ffn_w1, ffn_w3, ffn_w2, update_ctx):
    n_lat, n_ctx = h_lat.shape[1], h_ctx.shape[1]
    sh1_l, sc1_l, g1_l, sh2_l, sc2_l, g2_l = jnp.split(mod_lat, N_MOD, axis=-1)
    sh1_c, sc1_c, g1_c, sh2_c, sc2_c, g2_c = jnp.split(mod_ctx, N_MOD, axis=-1)
    u_lat = modulate(rms_norm(h_lat, norm1_g), sh1_l, sc1_l)
    u_ctx = modulate(rms_norm(h_ctx, norm1_g), sh1_c, sc1_c)
    q_l, k_l, v_l, hy_l, gla_l, r_l = mixer_inputs(u_lat, w_in, q_norm_g, k_norm_g, gla_gate_w, gla_gate_b)
    q_c, k_c, v_c, hy_c, gla_c, r_c = mixer_inputs(u_ctx, w_in, q_norm_g, k_norm_g, gla_gate_w, gla_gate_b)

    q_l = apply_rope(q_l, rope_cos, rope_sin) * ATTN_SCALE
    k_l = apply_rope(k_l, rope_cos, rope_sin)
    q_c = q_c * ATTN_SCALE
    attn_l = latent_block_attention(q_l, jnp.concatenate([k_l, k_c], axis=1), jnp.concatenate([v_l, v_c], axis=1))

    filt = (filt_w1, filt_b1, filt_w2, filt_b2, filt_w3, filt_freq)
    hy_out_l = hyena_mix(hy_l, hy_conv_w, hy_conv_b, hyena_filter_spectra(n_lat, *filt), hy_bias)

    gla_out_l, gla_out_c = gla_bidirectional(*gla_l, *gla_c)

    h_lat = h_lat + g1_l * merge_heads(attn_l, hy_out_l, gla_out_l, r_l, out_norm_g, w_out)
    h_lat = h_lat + g2_l * swiglu(modulate(rms_norm(h_lat, norm2_g), sh2_l, sc2_l), ffn_w1, ffn_w3, ffn_w2)

    if update_ctx:
        attn_c = gqa_attend(q_c, k_c, v_c)
        hy_out_c = hyena_mix(hy_c, hy_conv_w, hy_conv_b, hyena_filter_spectra(n_ctx, *filt), hy_bias)
        h_ctx = h_ctx + g1_c * merge_heads(attn_c, hy_out_c, gla_out_c, r_c, out_norm_g, w_out)
        h_ctx = h_ctx + g2_c * swiglu(modulate(rms_norm(h_ctx, norm2_g), sh2_c, sc2_c), ffn_w1, ffn_w3, ffn_w2)
    return h_lat, h_ctx


def setup_inputs(seed: int = 0) -> dict:
    key = jax.random.key(seed)
    ks = jax.random.split(key, 28)
    f32 = jnp.float32

    def nrm(k, shape, std):
        return std * jax.random.normal(k, shape, f32)

    d = D_MODEL
    return {
        "x": nrm(ks[0], (BATCH, SEQ, d), 1.0),
        "c": nrm(ks[1], (BATCH, d), 1.0),
        "ctx": nrm(ks[2], (BATCH, CTX_LEN, d), 1.0),
        "c_ctx": nrm(ks[3], (d,), 1.0),
        "ada_w": nrm(ks[4], (DEPTH, d, N_MOD * d), 0.5 * d ** -0.5),
        "ada_b": nrm(ks[5], (DEPTH, N_MOD * d), 0.02),
        "norm1_g": 1.0 + nrm(ks[6], (DEPTH, d), 0.02),
        "w_in": nrm(ks[7], (DEPTH, d, IN_WIDTH), d ** -0.5),
        "q_norm_g": 1.0 + nrm(ks[8], (DEPTH, HEAD_DIM), 0.02),
        "k_norm_g": 1.0 + nrm(ks[9], (DEPTH, HEAD_DIM), 0.02),
        "hy_conv_w": nrm(ks[10], (DEPTH, SHORT_CONV, 3 * HYENA_WIDTH), SHORT_CONV ** -0.5),
        "hy_conv_b": nrm(ks[11], (DEPTH, 3 * HYENA_WIDTH), 0.02),
        "filt_w1": nrm(ks[12], (DEPTH, FILTER_EMB, FILTER_HIDDEN), FILTER_EMB ** -0.5),
        "filt_b1": nrm(ks[13], (DEPTH, FILTER_HIDDEN), 0.02),
        "filt_w2": nrm(ks[14], (DEPTH, FILTER_HIDDEN, FILTER_HIDDEN), FILTER_HIDDEN ** -0.5),
        "filt_b2": nrm(ks[15], (DEPTH, FILTER_HIDDEN), 0.02),
        "filt_w3": nrm(ks[16], (DEPTH, FILTER_HIDDEN, HYENA_ORDER * 2 * HYENA_WIDTH), 0.02),
        "filt_freq": 1.0 + nrm(ks[17], (DEPTH, FILTER_HIDDEN), 0.02),
        "hy_bias": nrm(ks[18], (DEPTH, HYENA_ORDER, HYENA_WIDTH), 0.5),
        "gla_gate_w": nrm(ks[19], (DEPTH, 2, GLA_GATE_RANK, GLA_K_WIDTH), GLA_GATE_RANK ** -0.5),
        "gla_gate_b": nrm(ks[20], (DEPTH, 2, GLA_K_WIDTH), 0.02),
        "out_norm_g": 1.0 + nrm(ks[21], (DEPTH, MIX_WIDTH), 0.02),
        "w_out": nrm(ks[22], (DEPTH, MIX_WIDTH, d), MIX_WIDTH ** -0.5),
        "norm2_g": 1.0 + nrm(ks[23], (DEPTH, d), 0.02),
        "ffn_w1": nrm(ks[24], (DEPTH, d, FFN_HIDDEN), d ** -0.5),
        "ffn_w3": nrm(ks[25], (DEPTH, d, FFN_HIDDEN), d ** -0.5),
        "ffn_w2": nrm(ks[26], (DEPTH, FFN_HIDDEN, d), FFN_HIDDEN ** -0.5),
        "final_norm_g": 1.0 + nrm(ks[27], (d,), 0.02),
    }


def reference(x, c, ctx, c_ctx, ada_w, ada_b, norm1_g, w_in, q_norm_g, k_norm_g, hy_conv_w, hy_conv_b,
              filt_w1, filt_b1, filt_w2, filt_b2, filt_w3, filt_freq, hy_bias, gla_gate_w, gla_gate_b,
              out_norm_g, w_out, norm2_g, ffn_w1, ffn_w3, ffn_w2, final_norm_g):
    rope_cos, rope_sin = axial_rope_tables(x.shape[1])
    silu_c = jax.nn.silu(c)
    silu_cc = jax.nn.silu(c_ctx)
    h_lat, h_ctx = x, ctx
    for i in range(DEPTH):
        mod_lat = (silu_c @ ada_w[i] + ada_b[i])[:, None, :]
        mod_ctx = (silu_cc @ ada_w[i] + ada_b[i])[None, None, :]
        h_lat, h_ctx = hybrid_layer(
            h_lat, h_ctx, mod_lat, mod_ctx, rope_cos, rope_sin, norm1_g[i], w_in[i], q_norm_g[i], k_norm_g[i],
            hy_conv_w[i], hy_conv_b[i], filt_w1[i], filt_b1[i], filt_w2[i], filt_b2[i], filt_w3[i], filt_freq[i],
            hy_bias[i], gla_gate_w[i], gla_gate_b[i], out_norm_g[i], w_out[i], norm2_g[i],
            ffn_w1[i], ffn_w3[i], ffn_w2[i], i < DEPTH - 1)
    return rms_norm(h_lat, final_norm_g)
```

```python
import functools
import math

import numpy as np
import jax
import jax.numpy as jnp
from jax import lax
from jax.experimental import pallas as pl
from jax.experimental.pallas import tpu as pltpu

F32 = jnp.float32
BF16 = jnp.bfloat16
HI = lax.Precision.HIGHEST

LANES = 128
VMEM_LIMIT = 56 * 1024 * 1024

HEAD_DIM = 64
ATTN_HEADS = 8
ATTN_KV_HEADS = 2
ATTN_WIDTH = ATTN_HEADS * HEAD_DIM
KV_WIDTH = ATTN_KV_HEADS * HEAD_DIM
HYENA_WIDTH = 256
HYENA_ORDER = 2
FILTER_BANDS = 16
FILTER_HIDDEN = 64
GLA_HEADS = 4
GLA_DK = 32
GLA_DV = 64
GLA_K_WIDTH = GLA_HEADS * GLA_DK
GLA_V_WIDTH = GLA_HEADS * GLA_DV
GLA_GATE_RANK = 16
GLA_GATE_TAU = 16.0
GLA_CHUNK = 64
GRID_W = 64
ROPE_THETA = 10000.0
N_MOD = 6
NORM_EPS = 1e-6
HYENA_TARGET = 1e-2
HYENA_MIN_DECAY = math.log(HYENA_TARGET) / 1.5
HYENA_MAX_DECAY = math.log(HYENA_TARGET) / 0.3

TOK_TILE = 256
DFT_INNER = 128
IN_PAD = 2432

_OQ, _OK, _OV, _OHY = 0, 512, 640, 768
_OGQ, _OGK, _OGV, _OGR, _OGA = 1536, 1664, 1792, 2048, 2304


def _cparams(sem, vmem=VMEM_LIMIT):
    return pltpu.CompilerParams(dimension_semantics=sem, vmem_limit_bytes=vmem)


def _bdot(a, b):
    return jnp.dot(a.astype(BF16), b.astype(BF16), preferred_element_type=F32)


def _hdot(a, b):
    return jnp.dot(a, b, preferred_element_type=F32, precision=HI)


def _group_mean_square(x, gmat):
    outs = []
    for j in range(x.shape[1] // LANES):
        sq = jnp.square(x[:, j * LANES:(j + 1) * LANES])
        hi = sq.astype(BF16)
        lo = (sq - hi.astype(F32)).astype(BF16)
        s = jnp.dot(hi, gmat, preferred_element_type=F32) + jnp.dot(lo, gmat, preferred_element_type=F32)
        outs.append(s * (1.0 / HEAD_DIM))
    return outs[0] if len(outs) == 1 else jnp.concatenate(outs, axis=1)


def _head_rms(x, gain, gmat):
    ms = _group_mean_square(x, gmat)
    return x * lax.rsqrt(ms + NORM_EPS) * gain


def _mod_row(mod_ref, is_ctx, b, n_batch):
    row = jnp.where(is_ctx, n_batch, b)
    return mod_ref[pl.ds(row, 1), :]


def _silu(x):
    return x * (1.0 / (1.0 + jnp.exp(-x)))


def _mod_kernel(c_ref, w_ref, b_ref, o_ref):
    s = _silu(c_ref[...])
    o_ref[0] = _bdot(s, w_ref[0]) + b_ref[0]


def _modulation(cond_rows, ada_w, ada_b):
    depth, d, w = ada_w.shape
    ct = 1536
    return pl.pallas_call(
        _mod_kernel,
        out_shape=jax.ShapeDtypeStruct((depth, 8, w), F32),
        grid=(depth, w // ct),
        in_specs=[pl.BlockSpec((8, d), lambda l, j: (0, 0)),
                  pl.BlockSpec((1, d, ct), lambda l, j: (l, 0, j)),
                  pl.BlockSpec((1, 1, ct), lambda l, j: (l, 0, j))],
        out_specs=pl.BlockSpec((1, 8, ct), lambda l, j: (l, 0, j)),
        compiler_params=_cparams(("arbitrary", "arbitrary")),
        name="adaln_modulation",
    )(cond_rows, ada_w, ada_b.reshape(depth, 1, w))


def _in_kernel(h_ref, mod_ref, g1_ref, w_ref, qg_ref, kg_ref, cos_ref, sin_ref, gmat_ref, wg_ref, bg_ref,
               q_ref, k_ref, v_ref, hy_ref, gla_ref, gr_ref, *, n_batch, ctx_tiles, d_model):
    b, i = pl.program_id(0), pl.program_id(1)
    mod = _mod_row(mod_ref, i < ctx_tiles, b, n_batch)
    shift, scale = mod[:, 0:d_model], mod[:, d_model:2 * d_model]
    x = h_ref[0]
    xn = x * lax.rsqrt(jnp.mean(jnp.square(x), axis=-1, keepdims=True) + NORM_EPS) * g1_ref[...]
    u = xn * (1.0 + scale) + shift
    proj = _bdot(u, w_ref[...])

    gmat = gmat_ref[...]
    cosf, sinf = cos_ref[...], sin_ref[...]
    lane = lax.broadcasted_iota(jnp.int32, (x.shape[0], LANES), 1)
    first_half = (lane % HEAD_DIM) < (HEAD_DIM // 2)
    low_head = lane < HEAD_DIM

    def rope(t):
        partner = jnp.where(first_half, pltpu.roll(t, LANES - HEAD_DIM // 2, 1), pltpu.roll(t, HEAD_DIM // 2, 1))
        return t * cosf + partner * sinf

    qn = _head_rms(proj[:, _OQ:_OQ + ATTN_WIDTH], qg_ref[...], gmat)
    zero = jnp.zeros((x.shape[0], LANES), F32)
    for p in range(ATTN_HEADS // 2):
        t = rope(qn[:, p * LANES:(p + 1) * LANES]) * (HEAD_DIM ** -0.5)
        tr = pltpu.roll(t, HEAD_DIM, 1)
        if 2 * p < ATTN_HEADS // ATTN_KV_HEADS:
            q_ref[0, 2 * p] = jnp.where(low_head, t, zero).astype(BF16)
            q_ref[0, 2 * p + 1] = jnp.where(low_head, tr, zero).astype(BF16)
        else:
            q_ref[0, 2 * p] = jnp.where(low_head, zero, tr).astype(BF16)
            q_ref[0, 2 * p + 1] = jnp.where(low_head, zero, t).astype(BF16)
    kn = _head_rms(proj[:, _OK:_OK + KV_WIDTH], kg_ref[...], gmat)
    k_ref[0] = rope(kn).astype(BF16)
    v_ref[0] = proj[:, _OV:_OV + KV_WIDTH].astype(BF16)
    hy_ref[0] = proj[:, _OHY:_OHY + 3 * HYENA_WIDTH]

    zg = _bdot(proj[:, _OGA:_OGA + LANES], wg_ref[...]) + bg_ref[...]
    log_gate = (jnp.minimum(zg, 0.0) - jnp.log(1.0 + jnp.exp(-jnp.abs(zg)))) * (1.0 / GLA_GATE_TAU)
    gla_ref[0, :, 0:GLA_K_WIDTH] = proj[:, _OGQ:_OGQ + GLA_K_WIDTH] * (GLA_DK ** -0.5)
    gla_ref[0, :, GLA_K_WIDTH:2 * GLA_K_WIDTH + GLA_V_WIDTH] = proj[:, _OGK:_OGK + GLA_K_WIDTH + GLA_V_WIDTH]
    gla_ref[0, :, 2 * GLA_K_WIDTH + GLA_V_WIDTH:] = log_gate
    gr_ref[0] = proj[:, _OGR:_OGR + GLA_V_WIDTH]


def _in_projection(h, mod, g1, w_in, qg, kg, cosf, sinf, gmat, wg, bg, *, ctx_len):
    nb, t, d = h.shape
    tb = TOK_TILE
    kern = functools.partial(_in_kernel, n_batch=nb, ctx_tiles=ctx_len // tb, d_model=d)
    const = lambda shape: pl.BlockSpec(shape, lambda b, i: (0,) * len(shape))
    tok = lambda w: pl.BlockSpec((1, tb, w), lambda b, i: (b, i, 0))
    gla_w = 2 * GLA_K_WIDTH + GLA_V_WIDTH + 2 * GLA_K_WIDTH
    return pl.pallas_call(
        kern,
        out_shape=(jax.ShapeDtypeStruct((nb, ATTN_HEADS, t, LANES), BF16),
                   jax.ShapeDtypeStruct((nb, t, KV_WIDTH), BF16),
                   jax.ShapeDtypeStruct((nb, t, KV_WIDTH), BF16),
                   jax.ShapeDtypeStruct((nb, t, 3 * HYENA_WIDTH), F32),
                   jax.ShapeDtypeStruct((nb, t, gla_w), F32),
                   jax.ShapeDtypeStruct((nb, t, GLA_V_WIDTH), F32)),
        grid=(nb, t // tb),
        in_specs=[tok(d), const(mod.shape), const(g1.shape), const(w_in.shape), const(qg.shape), const(kg.shape),
                  pl.BlockSpec((tb, LANES), lambda b, i: (i, 0)), pl.BlockSpec((tb, LANES), lambda b, i: (i, 0)),
                  const(gmat.shape), const(wg.shape), const(bg.shape)],
        out_specs=(pl.BlockSpec((1, ATTN_HEADS, tb, LANES), lambda b, i: (b, 0, i, 0)),
                   tok(KV_WIDTH), tok(KV_WIDTH), tok(3 * HYENA_WIDTH), tok(gla_w), tok(GLA_V_WIDTH)),
        compiler_params=_cparams(("parallel", "parallel")),
        name="in_projection",
    )(h, mod, g1, w_in, qg, kg, cosf, sinf, gmat, wg, bg)


def _attn_kernel(q_ref, k_ref, v_ref, o_ref, m_sc, l_sc, acc_sc, *, ctx_len, ctx_tiles, kv_tile, n_kv_tiles):
    i = pl.program_id(1)
    tq = q_ref.shape[2]
    m_sc[...] = jnp.full(m_sc.shape, -jnp.inf, F32)
    l_sc[...] = jnp.zeros(l_sc.shape, F32)
    acc_sc[...] = jnp.zeros(acc_sc.shape, F32)

    def visit(start, size):
        kc = k_ref[0, pl.ds(start, size), :]
        vc = v_ref[0, pl.ds(start, size), :]

        def head(hh, carry):
            rows = pl.ds(pl.multiple_of(hh * tq, tq), tq)
            s = lax.dot_general(q_ref[0, hh], kc, (((1,), (1,)), ((), ())), preferred_element_type=F32)
            m_prev = m_sc[rows, :]
            m_new = jnp.maximum(m_prev, jnp.max(s, axis=1, keepdims=True))
            alpha = jnp.exp(m_prev - m_new)
            p = jnp.exp(s - m_new)
            l_sc[rows, :] = alpha * l_sc[rows, :] + jnp.sum(p, axis=1, keepdims=True)
            acc_sc[rows, :] = alpha * acc_sc[rows, :] + jnp.dot(p.astype(BF16), vc, preferred_element_type=F32)
            m_sc[rows, :] = m_new
            return carry

        lax.fori_loop(0, ATTN_HEADS, head, 0)

    visit(0, ctx_len)
    n_lat = jnp.where(i < ctx_tiles, 0, n_kv_tiles)

    def lat_body(j, carry):
        visit(pl.multiple_of(ctx_len + j * kv_tile, kv_tile), kv_tile)
        return carry

    lax.fori_loop(0, n_lat, lat_body, 0)

    lane = lax.broadcasted_iota(jnp.int32, (tq, LANES), 1)
    low_head = lane < HEAD_DIM
    per_kv = ATTN_HEADS // ATTN_KV_HEADS
    for p in range(ATTN_HEADS // 2):
        outs = []
        for hh in (2 * p, 2 * p + 1):
            o = acc_sc[hh * tq:(hh + 1) * tq, :] / l_sc[hh * tq:(hh + 1) * tq, :]
            src_low = hh < per_kv
            dst_low = hh % 2 == 0
            outs.append(o if src_low == dst_low else pltpu.roll(o, HEAD_DIM, 1))
        o_ref[0, :, p * LANES:(p + 1) * LANES] = jnp.where(low_head, outs[0], outs[1])


def _attention(q, k, v, *, ctx_len):
    nb, nh, t, _ = q.shape
    tq = TOK_TILE
    n_lat = t - ctx_len
    kv_tile = 512 if n_lat % 512 == 0 else 256
    rows = nh * tq
    kern = functools.partial(_attn_kernel, ctx_len=ctx_len, ctx_tiles=ctx_len // tq, kv_tile=kv_tile,
                             n_kv_tiles=n_lat // kv_tile)
    return pl.pallas_call(
        kern,
        out_shape=jax.ShapeDtypeStruct((nb, t, ATTN_WIDTH), F32),
        grid=(nb, t // tq),
        in_specs=[pl.BlockSpec((1, nh, tq, LANES), lambda b, i: (b, 0, i, 0)),
                  pl.BlockSpec((1, t, KV_WIDTH), lambda b, i: (b, 0, 0)),
                  pl.BlockSpec((1, t, KV_WIDTH), lambda b, i: (b, 0, 0))],
        out_specs=pl.BlockSpec((1, tq, ATTN_WIDTH), lambda b, i: (b, i, 0)),
        scratch_shapes=[pltpu.VMEM((rows, 1), F32), pltpu.VMEM((rows, 1), F32), pltpu.VMEM((rows, LANES), F32)],
        compiler_params=_cparams(("parallel", "arbitrary")),
        name="gqa_attention",
    )(q, k, v)


def _short_conv_kernel(x_ref, w_ref, b_ref, lat_ref, ctx_ref, *, ctx_len):
    x = x_ref[0]
    t = x.shape[0]
    row = lax.broadcasted_iota(jnp.int32, x.shape, 0)
    prev = jnp.where((row == 0) | (row == ctx_len), 0.0, pltpu.roll(x, 1, 0))
    nxt = jnp.where((row == ctx_len - 1) | (row == t - 1), 0.0, pltpu.roll(x, t - 1, 0))
    w = w_ref[...]
    u = prev * w[0:1, :] + x * w[1:2, :] + nxt * w[2:3, :] + b_ref[...]
    ctx_ref[0] = u[:ctx_len]
    lat_ref[0, 0] = u[ctx_len:]


def _short_conv(hy, conv_w, conv_b, *, ctx_len):
    nb, t, w = hy.shape
    per = HYENA_WIDTH // LANES
    return pl.pallas_call(
        functools.partial(_short_conv_kernel, ctx_len=ctx_len),
        out_shape=(jax.ShapeDtypeStruct((3, nb, t - ctx_len, HYENA_WIDTH), F32),
                   jax.ShapeDtypeStruct((nb, ctx_len, w), F32)),
        grid=(nb, w // LANES),
        in_specs=[pl.BlockSpec((1, t, LANES), lambda b, j: (b, 0, j)),
                  pl.BlockSpec((3, LANES), lambda b, j: (0, j)),
                  pl.BlockSpec((1, LANES), lambda b, j: (0, j))],
        out_specs=(pl.BlockSpec((1, 1, t - ctx_len, LANES), lambda b, j: (j // per, b, 0, j % per)),
                   pl.BlockSpec((1, ctx_len, LANES), lambda b, j: (b, 0, j))),
        compiler_params=_cparams(("parallel", "parallel")),
        name="hyena_short_conv",
    )(hy, conv_w, conv_b.reshape(1, w))


def _filter_kernel(z_ref, w1_ref, b1_ref, w2_ref, b2_ref, w3_ref, fr_ref, dl_ref, o_ref, *, n):
    r = pl.program_id(0)
    z = z_ref[...]
    fr = fr_ref[...]
    a = jnp.sin(fr * (_hdot(z, w1_ref[...]) + b1_ref[...]))
    a = jnp.sin(fr * (_hdot(a, w2_ref[...]) + b2_ref[...]))
    h = _hdot(a, w3_ref[0])
    window = jnp.exp(-z[:, 0:1] * dl_ref[...])
    row = r * z.shape[0] + lax.broadcasted_iota(jnp.int32, h.shape, 0)
    o_ref[...] = jnp.where(row == n, 0.0, h * window)


def _filter_features(n):
    t = jnp.linspace(0.0, 1.0, n, dtype=F32)[:, None]
    omega = (2.0 * math.pi / n) * jnp.arange(n, dtype=F32)
    bands = jnp.linspace(1e-4, FILTER_BANDS - 1, FILTER_BANDS, dtype=F32)
    phase = omega[:, None] * bands[None, :]
    z = jnp.concatenate([t, jnp.cos(phase), -jnp.sin(phase)], axis=-1)
    src = np.concatenate([np.arange(n), [0], np.arange(n - 1, 0, -1)])
    zc = z[src]
    return jnp.pad(zc, ((0, 0), (0, LANES - zc.shape[1])))


def _hyena_filter(zc, w1p, b1, w2, b2, w3r, freq, deltas, *, n):
    rt = min(512, n)
    width = HYENA_ORDER * HYENA_WIDTH
    const = lambda shape: pl.BlockSpec(shape, lambda r: (0,) * len(shape))
    return pl.pallas_call(
        functools.partial(_filter_kernel, n=n),
        out_shape=jax.ShapeDtypeStruct((2 * n, width), F32),
        grid=(2 * n // rt,),
        in_specs=[pl.BlockSpec((rt, LANES), lambda r: (r, 0)), const(w1p.shape), const(b1.shape), const(w2.shape),
                  const(b2.shape), pl.BlockSpec((1, FILTER_HIDDEN, width), lambda r: (r // (n // rt), 0, 0)),
                  const(freq.shape), const(deltas.shape)],
        out_specs=pl.BlockSpec((rt, width), lambda r: (r, 0)),
        compiler_params=_cparams(("parallel",)),
        name="hyena_filter",
    )(zc, w1p, b1, w2, b2, w3r, freq, deltas)


def _dft_consts(n):
    big = 2 * n
    n1 = big // DFT_INNER
    half = n1 // 2
    k1 = np.arange(n1)[:, None]
    a = np.arange(n1)[None, :]
    ang = -2.0 * np.pi * ((k1 * a) % n1) / n1
    fr, fi = np.cos(ang), np.sin(ang)
    first = np.block([[fr[:, :half], -fi[:, :half]], [fi[:, :half], fr[:, :half]]])
    first_real = np.concatenate([fr, fi], axis=0)
    last = np.block([[fr[:half, :], fi[:half, :]], [-fi[:half, :], fr[:half, :]]]) / big
    k2 = np.arange(DFT_INNER)[:, None]
    s = np.arange(DFT_INNER)[None, :]
    ang2 = -2.0 * np.pi * ((k2 * s) % DFT_INNER) / DFT_INNER
    gr, gi = np.cos(ang2), np.sin(ang2)
    inner = np.block([[gr, -gi], [gi, gr]])
    inner_inv = np.block([[gr, gi], [-gi, gr]])
    prod = (jnp.arange(n1, dtype=jnp.int32)[:, None] * jnp.arange(DFT_INNER, dtype=jnp.int32)[None, :]) % big
    angt = prod.astype(F32) * (-2.0 * math.pi / big)
    tw = jnp.stack([jnp.cos(angt), jnp.sin(angt)])
    tw = jnp.broadcast_to(tw[..., None], tw.shape + (LANES,))
    f = lambda m: jnp.asarray(np.ascontiguousarray(m), F32)
    return dict(first=f(first), first_real=f(first_real), last=f(last), inner=f(inner), inner_inv=f(inner_inv),
                tw=tw, n1=n1)


def _left_matmul_kernel(*refs, mode):
    if mode == "first":
        m_ref, x_ref, o_ref = refs
        o_ref[...] = _hdot(m_ref[...], x_ref[...])
        return
    if mode == "last":
        m3_ref, b_ref, g_ref, v_ref, bias_ref, z_ref = refs
    else:
        m3_ref, b_ref, g_ref, v_ref, bias_ref, m1_ref, z_ref, a_ref = refs
    y = _hdot(m3_ref[...], b_ref[...])
    v = v_ref[...]
    z = g_ref[...] * (y + bias_ref[...] * v)
    z_ref[...] = z
    if mode == "last_first":
        a_ref[...] = _hdot(m1_ref[...], z)


def _dft_first(m1, x2):
    rows, wtot = x2.shape
    wt = 2048
    return pl.pallas_call(
        functools.partial(_left_matmul_kernel, mode="first"),
        out_shape=jax.ShapeDtypeStruct((m1.shape[0], wtot), F32),
        grid=(wtot // wt,),
        in_specs=[pl.BlockSpec(m1.shape, lambda j: (0, 0)), pl.BlockSpec((rows, wt), lambda j: (0, j))],
        out_specs=pl.BlockSpec((m1.shape[0], wt), lambda j: (0, j)),
        compiler_params=_cparams(("parallel",)),
        name="dft_outer_forward",
    )(m1, x2)


def _dft_last(m3, b2, gate, v, bias_flat, m1=None):
    rows, wtot = gate.shape
    wt = 2048
    col = lambda r: pl.BlockSpec((r, wt), lambda j: (0, j))
    const = lambda a: pl.BlockSpec(a.shape, lambda j: (0, 0))
    if m1 is None:
        return pl.pallas_call(
            functools.partial(_left_matmul_kernel, mode="last"),
            out_shape=jax.ShapeDtypeStruct((rows, wtot), F32),
            grid=(wtot // wt,),
            in_specs=[const(m3), col(b2.shape[0]), col(rows), col(rows), pl.BlockSpec((1, wt), lambda j: (0, 0))],
            out_specs=col(rows),
            compiler_params=_cparams(("parallel",)),
            name="dft_outer_inverse",
        )(m3, b2, gate, v, bias_flat)
    return pl.pallas_call(
        functools.partial(_left_matmul_kernel, mode="last_first"),
        out_shape=(jax.ShapeDtypeStruct((rows, wtot), F32), jax.ShapeDtypeStruct((m1.shape[0], wtot), F32)),
        grid=(wtot // wt,),
        in_specs=[const(m3), col(b2.shape[0]), col(rows), col(rows), pl.BlockSpec((1, wt), lambda j: (0, 0)),
                  const(m1)],
        out_specs=(col(rows), col(m1.shape[0])),
        compiler_params=_cparams(("parallel",)),
        name="dft_outer_inverse_forward",
    )(m3, b2, gate, v, bias_flat, m1)


def _dft_mid_kernel(a_ref, tw_ref, g_ref, *rest, slabs, with_filter):
    if with_filter:
        h_ref, gi_ref, o_ref = rest
    else:
        (o_ref,) = rest
    nl = a_ref.shape[3] // LANES
    for j in range(slabs):
        twr = jnp.concatenate([tw_ref[0, j]] * nl, axis=1)
        twi = jnp.concatenate([tw_ref[1, j]] * nl, axis=1)
        ar, ai = a_ref[0, j], a_ref[1, j]
        br = ar * twr - ai * twi
        bi = ar * twi + ai * twr
        x = _hdot(g_ref[...], jnp.concatenate([br, bi], axis=0))
        if not with_filter:
            o_ref[0, j] = x[:DFT_INNER]
            o_ref[1, j] = x[DFT_INNER:]
            continue
        xr, xi = x[:DFT_INNER], x[DFT_INNER:]
        hr, hi = h_ref[0, j], h_ref[1, j]
        yr = xr * hr - xi * hi
        yi = xr * hi + xi * hr
        zz = _hdot(gi_ref[...], jnp.concatenate([yr, yi], axis=0))
        zr, zi = zz[:DFT_INNER], zz[DFT_INNER:]
        o_ref[0, j] = zr * twr + zi * twi
        o_ref[1, j] = zi * twr - zr * twi


def _dft_mid(a4, tw, inner, spec=None, inner_inv=None, order=0):
    _, n1, _, c = a4.shape
    slabs = 8 if n1 % 8 == 0 else n1
    blk = lambda w: pl.BlockSpec((2, slabs, DFT_INNER, w), lambda j: (0, j, 0, 0))
    const = lambda a: pl.BlockSpec(a.shape, lambda j: (0, 0))
    in_specs = [blk(c), blk(LANES), const(inner)]
    args = [a4, tw, inner]
    if spec is not None:
        in_specs += [pl.BlockSpec((2, slabs, DFT_INNER, c), lambda j: (0, j, 0, order)), const(inner_inv)]
        args += [spec, inner_inv]
    return pl.pallas_call(
        functools.partial(_dft_mid_kernel, slabs=slabs, with_filter=spec is not None),
        out_shape=jax.ShapeDtypeStruct(a4.shape, F32),
        grid=(n1 // slabs,),
        in_specs=in_specs,
        out_specs=blk(c),
        compiler_params=_cparams(("parallel",)),
        name="dft_inner_filter" if spec is not None else "dft_inner_forward",
    )(*args)


def _hyena_latent(vx, h_circ, bias, consts):
    _, nb, n, c = vx.shape
    assert nb == 2, "the two samples are carried as the real and imaginary part of one transform"
    n1 = consts["n1"]
    half = n1 // 2
    wflat = DFT_INNER * c
    flat = vx.reshape(3, nb * half, wflat)
    v_f, x1_f, x2_f = flat[0], flat[1], flat[2]
    bias_flat = jnp.tile(bias, (1, 2048 // c))

    hf = h_circ.reshape(n1, DFT_INNER * h_circ.shape[1])
    ha = _dft_first(consts["first_real"], hf).reshape(2, n1, DFT_INNER, h_circ.shape[1])
    spec = _dft_mid(ha, consts["tw"], consts["inner"])

    a = _dft_first(consts["first"], v_f).reshape(2, n1, DFT_INNER, c)
    bm = _dft_mid(a, consts["tw"], consts["inner"], spec, consts["inner_inv"], order=0)
    z, a = _dft_last(consts["last"], bm.reshape(2 * n1, wflat), x1_f, v_f, bias_flat[0:1], consts["first"])
    bm = _dft_mid(a.reshape(2, n1, DFT_INNER, c), consts["tw"], consts["inner"], spec, consts["inner_inv"], order=1)
    y = _dft_last(consts["last"], bm.reshape(2 * n1, wflat), x2_f, z, bias_flat[1:2])
    return y.reshape(nb, n, c)


def _hyena_ctx_kernel(u_ref, h_ref, fs_ref, ff_ref, fi_ref, bias_ref, o_ref, *, n):
    big = 2 * n
    c = HYENA_WIDTH
    spec = _hdot(fs_ref[...], h_ref[...])
    u = u_ref[...]
    v, x1, x2 = u[:, 0:c], u[:, c:2 * c], u[:, 2 * c:3 * c]

    def conv(zin, order):
        x = _hdot(ff_ref[...], zin)
        xr, xi = x[:big], x[big:]
        hr, hi = spec[:big, order * c:(order + 1) * c], spec[big:, order * c:(order + 1) * c]
        y = jnp.concatenate([xr * hr - xi * hi, xr * hi + xi * hr], axis=0)
        return _hdot(fi_ref[...], y)

    bias = bias_ref[...]
    z = x1 * (conv(v, 0) + bias[0:1] * v)
    o_ref[...] = x2 * (conv(z, 1) + bias[1:2] * z)


def _ctx_dft_consts(n):
    big = 2 * n
    k = np.arange(big)[:, None]
    t = np.arange(big)[None, :]
    ang = -2.0 * np.pi * ((k * t) % big) / big
    fr, fi = np.cos(ang), np.sin(ang)
    spec_m = np.concatenate([fr, fi], axis=0)
    fwd = np.block([[fr[:, :n], -fi[:, :n]], [fi[:, :n], fr[:, :n]]])
    inv = np.block([[fr.T[:n, :], fi.T[:n, :]], [-fi.T[:n, :], fr.T[:n, :]]]) / big
    f = lambda m: jnp.asarray(np.ascontiguousarray(m), F32)
    return f(spec_m), f(fwd), f(inv)


def _hyena_ctx(u_ctx, h_circ, bias, cconsts):
    nb, n, w = u_ctx.shape
    assert nb == 2
    fs, ff, fi = cconsts
    full = lambda a: pl.BlockSpec(a.shape, lambda i: (0,) * a.ndim)
    u2 = u_ctx.reshape(nb * n, w)
    out = pl.pallas_call(
        functools.partial(_hyena_ctx_kernel, n=n),
        out_shape=jax.ShapeDtypeStruct((nb * n, HYENA_WIDTH), F32),
        grid=(1,),
        in_specs=[full(u2), full(h_circ), full(fs), full(ff), full(fi), full(bias)],
        out_specs=pl.BlockSpec((nb * n, HYENA_WIDTH), lambda i: (0, 0)),
        compiler_params=_cparams(("arbitrary",)),
        name="hyena_context",
    )(u2, h_circ, fs, ff, fi, bias)
    return out.reshape(nb, n, HYENA_WIDTH)


def _gla_kernel(x_ref, o_ref, st_ref, *, backward):
    s = pl.program_id(1)

    @pl.when(s == 0)
    def _():
        st_ref[...] = jnp.zeros(st_ref.shape, F32)

    c = GLA_CHUNK
    kw, vw = GLA_K_WIDTH, GLA_V_WIDTH
    ri = lax.broadcasted_iota(jnp.int32, (c, c), 0)
    ci = lax.broadcasted_iota(jnp.int32, (c, c), 1)
    tri = ((ri <= ci) if backward else (ri >= ci)).astype(F32)
    kr = lax.broadcasted_iota(jnp.int32, (GLA_HEADS * c, kw), 0) // c
    kc = lax.broadcasted_iota(jnp.int32, (GLA_HEADS * c, kw), 1) // GLA_DK
    mask_k = kr == kc
    vr = lax.broadcasted_iota(jnp.int32, (GLA_HEADS * c, vw), 0) // c
    vc = lax.broadcasted_iota(jnp.int32, (GLA_HEADS * c, vw), 1) // GLA_DV
    mask_v = vr == vc
    sr = lax.broadcasted_iota(jnp.int32, (vw, kw), 0) // GLA_DV
    sc = lax.broadcasted_iota(jnp.int32, (vw, kw), 1) // GLA_DK
    mask_s = sr == sc
    qi = lax.broadcasted_iota(jnp.int32, (c, GLA_HEADS * c), 0)
    kj = lax.broadcasted_iota(jnp.int32, (c, GLA_HEADS * c), 1) % c
    causal = (qi <= kj) if backward else (qi >= kj)

    n_chunks = x_ref.shape[1] // c
    order = range(n_chunks - 1, -1, -1) if backward else range(n_chunks)
    goff = 2 * kw + vw + (kw if backward else 0)
    for ch in order:
        r0 = ch * c
        q = x_ref[0, r0:r0 + c, 0:kw]
        k = x_ref[0, r0:r0 + c, kw:2 * kw]
        v = x_ref[0, r0:r0 + c, 2 * kw:2 * kw + vw]
        g = x_ref[0, r0:r0 + c, goff:goff + kw]
        gc = _hdot(tri, g)
        g_last = gc[0:1] if backward else gc[c - 1:c]
        q_dec = q * jnp.exp(gc)
        k_dec = k * jnp.exp(-gc)
        k_tail = k * jnp.exp(g_last - gc)
        k_blk = jnp.where(mask_k, jnp.concatenate([k_dec] * GLA_HEADS, axis=0), 0.0).astype(BF16)
        v_bf = v.astype(BF16)
        v_blk = jnp.where(mask_v, jnp.concatenate([v_bf] * GLA_HEADS, axis=0), jnp.zeros((), BF16))
        q_bf = q_dec.astype(BF16)
        scores = lax.dot_general(q_bf, k_blk, (((1,), (1,)), ((), ())), preferred_element_type=F32)
        scores = jnp.where(causal, scores, 0.0)
        st = st_ref[...]
        o = jnp.dot(scores.astype(BF16), v_blk, preferred_element_type=F32)
        o = o + lax.dot_general(q_bf, st.astype(BF16), (((1,), (1,)), ((), ())), preferred_element_type=F32)
        o_ref[0, r0:r0 + c, :] = o
        kv_t = lax.dot_general(v_bf, k_tail.astype(BF16), (((0,), (0,)), ((), ())), preferred_element_type=F32)
        st_ref[...] = st * jnp.exp(g_last) + jnp.where(mask_s, kv_t, 0.0)


def _gla(gla_in, *, ctx_len, backward):
    nb, t, w = gla_in.shape
    tb = TOK_TILE
    nblk = t // tb
    cblk = ctx_len // tb
    if backward:
        idx = lambda b, s: (b, jnp.where(s < cblk, cblk - 1 - s, nblk - 1 - (s - cblk)), 0)
    else:
        idx = lambda b, s: (b, s, 0)
    return pl.pallas_call(
        functools.partial(_gla_kernel, backward=backward),
        out_shape=jax.ShapeDtypeStruct((nb, t, GLA_V_WIDTH), F32),
        grid=(nb, nblk),
        in_specs=[pl.BlockSpec((1, tb, w), idx)],
        out_specs=pl.BlockSpec((1, tb, GLA_V_WIDTH), idx),
        scratch_shapes=[pltpu.VMEM((GLA_V_WIDTH, GLA_K_WIDTH), F32)],
        compiler_params=_cparams(("parallel", "arbitrary")),
        name="gla_backward" if backward else "gla_forward",
    )(gla_in)


def _out_kernel(h_ref, mod_ref, at_ref, hy_ref, gf_ref, gb_ref, gr_ref, gn_ref, gmat_ref, w_ref, o_ref, *,
                n_batch, ctx_tiles, tile_off, d_model):
    b, i = pl.program_id(0), pl.program_id(1)
    mod = _mod_row(mod_ref, i + tile_off < ctx_tiles, b, n_batch)
    gate = mod[:, 2 * d_model:3 * d_model]
    gmat = gmat_ref[...]
    gn = gn_ref[...]
    a0, a1 = ATTN_WIDTH, ATTN_WIDTH + HYENA_WIDTH
    ya = _head_rms(at_ref[0], gn[:, :a0], gmat)
    yh = _head_rms(hy_ref[0], gn[:, a0:a1], gmat)
    yg = _head_rms(gf_ref[0] + gb_ref[0], gn[:, a1:], gmat) * _silu(gr_ref[0])
    y = jnp.concatenate([ya, yh, yg], axis=1)
    o_ref[0] = h_ref[0] + gate * _bdot(y, w_ref[...])


def _out_projection(h, mod, attn, hy, gla_f, gla_b, gr, gn, gmat, w_out, *, ctx_len, skip_ctx):
    nb, t, d = h.shape
    tb = TOK_TILE
    off = ctx_len // tb if skip_ctx else 0
    kern = functools.partial(_out_kernel, n_batch=nb, ctx_tiles=ctx_len // tb, tile_off=off, d_model=d)
    const = lambda a: pl.BlockSpec(a.shape, lambda b, i: (0,) * a.ndim)
    tok = lambda w: pl.BlockSpec((1, tb, w), lambda b, i: (b, i + off, 0))
    return pl.pallas_call(
        kern,
        out_shape=jax.ShapeDtypeStruct((nb, t - off * tb, d), F32),
        grid=(nb, t // tb - off),
        in_specs=[tok(d), const(mod), tok(ATTN_WIDTH), tok(HYENA_WIDTH), tok(GLA_V_WIDTH), tok(GLA_V_WIDTH),
                  tok(GLA_V_WIDTH), const(gn), const(gmat), const(w_out)],
        out_specs=pl.BlockSpec((1, tb, d), lambda b, i: (b, i, 0)),
        compiler_params=_cparams(("parallel", "parallel")),
        name="out_projection",
    )(h, mod, attn, hy, gla_f, gla_b, gr, gn, gmat, w_out)


def _ffn_kernel(h_ref, mod_ref, g2_ref, w1_ref, w3_ref, w2_ref, fg_ref, o_ref, *, n_batch, ctx_tiles, tile_off,
                d_model, final):
    b, i = pl.program_id(0), pl.program_id(1)
    mod = _mod_row(mod_ref, i + tile_off < ctx_tiles, b, n_batch)
    shift, scale, gate = (mod[:, 3 * d_model:4 * d_model], mod[:, 4 * d_model:5 * d_model],
                          mod[:, 5 * d_model:6 * d_model])
    x = h_ref[0]
    xn = x * lax.rsqrt(jnp.mean(jnp.square(x), axis=-1, keepdims=True) + NORM_EPS) * g2_ref[...]
    u = (xn * (1.0 + scale) + shift).astype(BF16)
    a = jnp.dot(u, w1_ref[...], preferred_element_type=F32)
    c = jnp.dot(u, w3_ref[...], preferred_element_type=F32)
    hid = (_silu(a) * c).astype(BF16)
    y = x + gate * jnp.dot(hid, w2_ref[...], preferred_element_type=F32)
    if final:
        y = y * lax.rsqrt(jnp.mean(jnp.square(y), axis=-1, keepdims=True) + NORM_EPS) * fg_ref[...]
    o_ref[0] = y


def _ffn(h, mod, g2, w1, w3, w2, fg, *, ctx_len, lat_only, final):
    nb, t, d = h.shape
    tb = TOK_TILE
    off = ctx_len // tb if lat_only else 0
    kern = functools.partial(_ffn_kernel, n_batch=nb, ctx_tiles=ctx_len // tb, tile_off=off, d_model=d, final=final)
    const = lambda a: pl.BlockSpec(a.shape, lambda b, i: (0,) * a.ndim)
    tok = pl.BlockSpec((1, tb, d), lambda b, i: (b, i, 0))
    return pl.pallas_call(
        kern,
        out_shape=jax.ShapeDtypeStruct((nb, t, d), F32),
        grid=(nb, t // tb),
        in_specs=[tok, const(mod), const(g2), const(w1), const(w3), const(w2), const(fg)],
        out_specs=tok,
        compiler_params=_cparams(("parallel", "parallel")),
        name="swiglu_ffn",
    )(h, mod, g2, w1, w3, w2, fg)


def _rope_tables(n_lat, ctx_len):
    rows = n_lat // GRID_W
    row = jnp.broadcast_to(jnp.arange(rows, dtype=F32)[:, None], (rows, GRID_W)).reshape(-1)
    col = jnp.broadcast_to(jnp.arange(GRID_W, dtype=F32)[None, :], (rows, GRID_W)).reshape(-1)
    n_freq = HEAD_DIM // 4
    inv_freq = jnp.power(ROPE_THETA, -jnp.arange(n_freq, dtype=F32) / n_freq)
    ang = jnp.concatenate([row[:, None] * inv_freq, col[:, None] * inv_freq], axis=-1)
    cos, sin = jnp.cos(ang), jnp.sin(ang)
    cos = jnp.concatenate([jnp.ones((ctx_len, HEAD_DIM // 2), F32), cos], axis=0)
    sin = jnp.concatenate([jnp.zeros((ctx_len, HEAD_DIM // 2), F32), sin], axis=0)
    cosf = jnp.tile(cos, (1, 2 * LANES // HEAD_DIM))
    sinf = jnp.tile(jnp.concatenate([-sin, sin], axis=1), (1, LANES // HEAD_DIM))
    return cosf, sinf


def kernel(x, c, ctx, c_ctx, ada_w, ada_b, norm1_g, w_in, q_norm_g, k_norm_g, hy_conv_w, hy_conv_b, filt_w1, filt_b1, filt_w2, filt_b2, filt_w3, filt_freq, hy_bias, gla_gate_w, gla_gate_b, out_norm_g, w_out, norm2_g, ffn_w1, ffn_w3, ffn_w2, final_norm_g):
    nb, n_lat, d = x.shape
    ctx_len = ctx.shape[1]
    depth = w_in.shape[0]
    assert nb == 2 and ctx_len % TOK_TILE == 0 and n_lat % TOK_TILE == 0 and n_lat % DFT_INNER == 0

    h = jnp.concatenate([ctx, x], axis=1)
    cond = jnp.concatenate([c, c_ctx[None, :], jnp.zeros((8 - nb - 1, d), F32)], axis=0)
    mods = _modulation(cond, ada_w, ada_b)

    cosf, sinf = _rope_tables(n_lat, ctx_len)
    lane = np.arange(LANES)
    gmat = jnp.asarray((lane[:, None] // HEAD_DIM) == (lane[None, :] // HEAD_DIM), BF16)
    consts = _dft_consts(n_lat)
    cconsts = _ctx_dft_consts(ctx_len)
    zc_lat = _filter_features(n_lat)
    zc_ctx = _filter_features(ctx_len)
    deltas = jnp.abs(jnp.linspace(HYENA_MIN_DECAY, HYENA_MAX_DECAY, HYENA_WIDTH, dtype=F32))
    deltas = jnp.tile(deltas[None, :], (1, HYENA_ORDER))

    w_in_b = jnp.pad(w_in, ((0, 0), (0, 0), (0, IN_PAD - w_in.shape[2]))).astype(BF16)
    w_out_b = w_out.astype(BF16)
    w1_b, w3_b, w2_b = ffn_w1.astype(BF16), ffn_w3.astype(BF16), ffn_w2.astype(BF16)
    fg = final_norm_g.reshape(1, d)

    out = None
    for l in range(depth):
        last = l == depth - 1
        mod = mods[l]
        wg = jnp.zeros((LANES, 2 * GLA_K_WIDTH), F32)
        wg = wg.at[0:GLA_GATE_RANK, 0:GLA_K_WIDTH].set(gla_gate_w[l, 0])
        wg = wg.at[GLA_GATE_RANK:2 * GLA_GATE_RANK, GLA_K_WIDTH:].set(gla_gate_w[l, 1])
        bg = gla_gate_b[l].reshape(1, 2 * GLA_K_WIDTH)
        qg = jnp.tile(q_norm_g[l][None, :], (1, ATTN_HEADS))
        kg = jnp.tile(k_norm_g[l][None, :], (1, ATTN_KV_HEADS))

        q, k, v, hy, gla_in, gr = _in_projection(h, mod, norm1_g[l].reshape(1, d), w_in_b[l], qg, kg, cosf, sinf,
                                                 gmat, wg.astype(BF16), bg, ctx_len=ctx_len)
        attn = _attention(q, k, v, ctx_len=ctx_len)

        vx, u_ctx = _short_conv(hy, hy_conv_w[l], hy_conv_b[l], ctx_len=ctx_len)
        w1p = jnp.pad(filt_w1[l], ((0, LANES - filt_w1.shape[1]), (0, 0)))
        w3r = filt_w3[l].reshape(FILTER_HIDDEN, HYENA_ORDER, 2, HYENA_WIDTH).transpose(2, 0, 1, 3)
        w3r = w3r.reshape(2, FILTER_HIDDEN, HYENA_ORDER * HYENA_WIDTH)
        fargs = (w1p, filt_b1[l][None, :], filt_w2[l], filt_b2[l][None, :], w3r, filt_freq[l][None, :], deltas)
        h_lat = _hyena_filter(zc_lat, *fargs, n=n_lat)
        hy_lat = _hyena_latent(vx, h_lat, hy_bias[l], consts)
        if not last:
            h_ctx = _hyena_filter(zc_ctx, *fargs, n=ctx_len)
            hy_ctx = _hyena_ctx(u_ctx, h_ctx, hy_bias[l], cconsts)
        else:
            hy_ctx = jnp.zeros((nb, ctx_len, HYENA_WIDTH), F32)
        hy_out = jnp.concatenate([hy_ctx, hy_lat], axis=1)

        gla_f = _gla(gla_in, ctx_len=ctx_len, backward=False)
        gla_b = _gla(gla_in, ctx_len=ctx_len, backward=True)

        h = _out_projection(h, mod, attn, hy_out, gla_f, gla_b, gr, out_norm_g[l].reshape(1, -1), gmat, w_out_b[l],
                            ctx_len=ctx_len, skip_ctx=last)
        h = _ffn(h, mod, norm2_g[l].reshape(1, d), w1_b[l], w3_b[l], w2_b[l], fg, ctx_len=ctx_len, lat_only=last,
                 final=last)
    return h
```

```python
import functools
import math

import numpy as np
import jax
import jax.numpy as jnp
from jax import lax
from jax.experimental import pallas as pl
from jax.experimental.pallas import tpu as pltpu

F32 = jnp.float32
BF16 = jnp.bfloat16
HI = lax.Precision.HIGHEST

LANES = 128
VMEM_LIMIT = 56 * 1024 * 1024

HEAD_DIM = 64
ATTN_HEADS = 8
ATTN_KV_HEADS = 2
ATTN_WIDTH = ATTN_HEADS * HEAD_DIM
KV_WIDTH = ATTN_KV_HEADS * HEAD_DIM
HYENA_WIDTH = 256
HYENA_ORDER = 2
FILTER_BANDS = 16
FILTER_HIDDEN = 64
GLA_HEADS = 4
GLA_DK = 32
GLA_DV = 64
GLA_K_WIDTH = GLA_HEADS * GLA_DK
GLA_V_WIDTH = GLA_HEADS * GLA_DV
GLA_GATE_RANK = 16
GLA_GATE_TAU = 16.0
GLA_CHUNK = 64
GRID_W = 64
ROPE_THETA = 10000.0
N_MOD = 6
NORM_EPS = 1e-6
HYENA_TARGET = 1e-2
HYENA_MIN_DECAY = math.log(HYENA_TARGET) / 1.5
HYENA_MAX_DECAY = math.log(HYENA_TARGET) / 0.3

TOK_TILE = 256
DFT_INNER = 128
IN_PAD = 2432

_OQ, _OK, _OV, _OHY = 0, 512, 640, 768
_OGQ, _OGK, _OGV, _OGR, _OGA = 1536, 1664, 1792, 2048, 2304


def _cparams(sem, vmem=VMEM_LIMIT):
    return pltpu.CompilerParams(dimension_semantics=sem, vmem_limit_bytes=vmem)


def _bdot(a, b):
    return jnp.dot(a.astype(BF16), b.astype(BF16), preferred_element_type=F32)


def _hdot(a, b):
    return jnp.dot(a, b, preferred_element_type=F32, precision=HI)


def _group_mean_square(x, gmat):
    outs = []
    for j in range(x.shape[1] // LANES):
        sq = jnp.square(x[:, j * LANES:(j + 1) * LANES])
        hi = sq.astype(BF16)
        lo = (sq - hi.astype(F32)).astype(BF16)
        s = jnp.dot(hi, gmat, preferred_element_type=F32) + jnp.dot(lo, gmat, preferred_element_type=F32)
        outs.append(s * (1.0 / HEAD_DIM))
    return outs[0] if len(outs) == 1 else jnp.concatenate(outs, axis=1)


def _head_rms(x, gain, gmat):
    ms = _group_mean_square(x, gmat)
    return x * lax.rsqrt(ms + NORM_EPS) * gain


def _mod_row(mod_ref, is_ctx, b, n_batch):
    row = jnp.where(is_ctx, n_batch, b)
    return mod_ref[pl.ds(row, 1), :]


def _silu(x):
    return x * (1.0 / (1.0 + jnp.exp(-x)))


def _mod_kernel(c_ref, w_ref, b_ref, o_ref):
    s = _silu(c_ref[...])
    o_ref[0] = _bdot(s, w_ref[0]) + b_ref[0]


def _modulation(cond_rows, ada_w, ada_b):
    depth, d, w = ada_w.shape
    ct = 1536
    return pl.pallas_call(
        _mod_kernel,
        out_shape=jax.ShapeDtypeStruct((depth, 8, w), F32),
        grid=(depth, w // ct),
        in_specs=[pl.BlockSpec((8, d), lambda l, j: (0, 0)),
                  pl.BlockSpec((1, d, ct), lambda l, j: (l, 0, j)),
                  pl.BlockSpec((1, 1, ct), lambda l, j: (l, 0, j))],
        out_specs=pl.BlockSpec((1, 8, ct), lambda l, j: (l, 0, j)),
        compiler_params=_cparams(("arbitrary", "arbitrary")),
        name="adaln_modulation",
    )(cond_rows, ada_w, ada_b.reshape(depth, 1, w))


def _in_kernel(h_ref, mod_ref, g1_ref, w_ref, qg_ref, kg_ref, cos_ref, sin_ref, gmat_ref, wg_ref, bg_ref,
               q_ref, k_ref, v_ref, hy_ref, gla_ref, gr_ref, *, n_batch, ctx_tiles, d_model):
    b, i = pl.program_id(0), pl.program_id(1)
    mod = _mod_row(mod_ref, i < ctx_tiles, b, n_batch)
    shift, scale = mod[:, 0:d_model], mod[:, d_model:2 * d_model]
    x = h_ref[0]
    xn = x * lax.rsqrt(jnp.mean(jnp.square(x), axis=-1, keepdims=True) + NORM_EPS) * g1_ref[...]
    u = xn * (1.0 + scale) + shift
    proj = _bdot(u, w_ref[...])

    gmat = gmat_ref[...]
    cosf, sinf = cos_ref[...], sin_ref[...]
    lane = lax.broadcasted_iota(jnp.int32, (x.shape[0], LANES), 1)
    first_half = (lane % HEAD_DIM) < (HEAD_DIM // 2)
    low_head = lane < HEAD_DIM

    def rope(t):
        partner = jnp.where(first_half, pltpu.roll(t, LANES - HEAD_DIM // 2, 1), pltpu.roll(t, HEAD_DIM // 2, 1))
        return t * cosf + partner * sinf

    qn = _head_rms(proj[:, _OQ:_OQ + ATTN_WIDTH], qg_ref[...], gmat)
    zero = jnp.zeros((x.shape[0], LANES), F32)
    for p in range(ATTN_HEADS // 2):
        t = rope(qn[:, p * LANES:(p + 1) * LANES]) * (HEAD_DIM ** -0.5 * math.log2(math.e))
        tr = pltpu.roll(t, HEAD_DIM, 1)
        if 2 * p < ATTN_HEADS // ATTN_KV_HEADS:
            q_ref[0, 2 * p] = jnp.where(low_head, t, zero).astype(BF16)
            q_ref[0, 2 * p + 1] = jnp.where(low_head, tr, zero).astype(BF16)
        else:
            q_ref[0, 2 * p] = jnp.where(low_head, zero, tr).astype(BF16)
            q_ref[0, 2 * p + 1] = jnp.where(low_head, zero, t).astype(BF16)
    kn = _head_rms(proj[:, _OK:_OK + KV_WIDTH], kg_ref[...], gmat)
    k_ref[0] = rope(kn).astype(BF16)
    v_ref[0] = proj[:, _OV:_OV + KV_WIDTH].astype(BF16)
    hy_ref[0] = proj[:, _OHY:_OHY + 3 * HYENA_WIDTH]

    zg = _bdot(proj[:, _OGA:_OGA + LANES], wg_ref[...]) + bg_ref[...]
    log_gate = (jnp.minimum(zg, 0.0) - jnp.log(1.0 + jnp.exp(-jnp.abs(zg)))) * (1.0 / GLA_GATE_TAU)
    gla_ref[0, :, 0:GLA_K_WIDTH] = proj[:, _OGQ:_OGQ + GLA_K_WIDTH] * (GLA_DK ** -0.5)
    gla_ref[0, :, GLA_K_WIDTH:2 * GLA_K_WIDTH + GLA_V_WIDTH] = proj[:, _OGK:_OGK + GLA_K_WIDTH + GLA_V_WIDTH]
    gla_ref[0, :, 2 * GLA_K_WIDTH + GLA_V_WIDTH:] = log_gate
    gr_ref[0] = proj[:, _OGR:_OGR + GLA_V_WIDTH]


def _in_projection(h, mod, g1, w_in, qg, kg, cosf, sinf, gmat, wg, bg, *, ctx_len):
    nb, t, d = h.shape
    tb = TOK_TILE
    kern = functools.partial(_in_kernel, n_batch=nb, ctx_tiles=ctx_len // tb, d_model=d)
    const = lambda shape: pl.BlockSpec(shape, lambda b, i: (0,) * len(shape))
    tok = lambda w: pl.BlockSpec((1, tb, w), lambda b, i: (b, i, 0))
    gla_w = 2 * GLA_K_WIDTH + GLA_V_WIDTH + 2 * GLA_K_WIDTH
    return pl.pallas_call(
        kern,
        out_shape=(jax.ShapeDtypeStruct((nb, ATTN_HEADS, t, LANES), BF16),
                   jax.ShapeDtypeStruct((nb, t, KV_WIDTH), BF16),
                   jax.ShapeDtypeStruct((nb, t, KV_WIDTH), BF16),
                   jax.ShapeDtypeStruct((nb, t, 3 * HYENA_WIDTH), F32),
                   jax.ShapeDtypeStruct((nb, t, gla_w), F32),
                   jax.ShapeDtypeStruct((nb, t, GLA_V_WIDTH), F32)),
        grid=(nb, t // tb),
        in_specs=[tok(d), const(mod.shape), const(g1.shape), const(w_in.shape), const(qg.shape), const(kg.shape),
                  pl.BlockSpec((tb, LANES), lambda b, i: (i, 0)), pl.BlockSpec((tb, LANES), lambda b, i: (i, 0)),
                  const(gmat.shape), const(wg.shape), const(bg.shape)],
        out_specs=(pl.BlockSpec((1, ATTN_HEADS, tb, LANES), lambda b, i: (b, 0, i, 0)),
                   tok(KV_WIDTH), tok(KV_WIDTH), tok(3 * HYENA_WIDTH), tok(gla_w), tok(GLA_V_WIDTH)),
        compiler_params=_cparams(("parallel", "parallel")),
        name="in_projection",
    )(h, mod, g1, w_in, qg, kg, cosf, sinf, gmat, wg, bg)


def _attn_kernel(q_ref, k_ref, v_ref, o_ref, m_sc, acc_sc, s_sc, mx_sc, p_sc, *, ctx_len, ctx_tiles, kv_tile,
                 n_kv_tiles):
    i = pl.program_id(1)
    tq = q_ref.shape[2]
    per_kv = ATTN_HEADS // ATTN_KV_HEADS
    grp = per_kv * tq
    m_sc[...] = jnp.full(m_sc.shape, -jnp.inf, F32)
    acc_sc[...] = jnp.zeros(acc_sc.shape, F32)

    def scores(start, size, buf):
        q = q_ref[0].reshape(ATTN_HEADS * tq, LANES)
        kc = k_ref[0, pl.ds(start, size), :]
        s = lax.dot_general(q, kc, (((1,), (1,)), ((), ())), preferred_element_type=F32)
        s_sc[buf, :, 0:size] = s
        mx_sc[buf] = jnp.broadcast_to(jnp.max(s, axis=1, keepdims=True), mx_sc.shape[1:])

    def consume(start, size, buf):
        vc = v_ref[0, pl.ds(start, size), :]
        low = lax.broadcasted_iota(jnp.int32, vc.shape, 1) < HEAD_DIM
        one = jnp.ones((), BF16)
        v_ones = (jnp.where(low, vc, one), jnp.where(low, one, vc))
        for g in range(ATTN_KV_HEADS):
            alphas = []
            for hh in range(g * per_kv, (g + 1) * per_kv):
                rows = slice(hh * tq, (hh + 1) * tq)
                m_prev = m_sc[rows, :]
                m_new = jnp.maximum(m_prev, mx_sc[buf, rows, :])
                alphas.append(jnp.exp2(m_prev - m_new))
                for j in range(size // LANES):
                    cols = slice(j * LANES, (j + 1) * LANES)
                    p_sc[rows, cols] = jnp.exp2(s_sc[buf, rows, cols] - m_new).astype(BF16)
                m_sc[rows, :] = m_new
            rows_g = slice(g * grp, (g + 1) * grp)
            pv = jnp.dot(p_sc[rows_g, 0:size], v_ones[g], preferred_element_type=F32)
            acc_sc[rows_g, :] = jnp.concatenate(alphas, axis=0) * acc_sc[rows_g, :] + pv

    scores(0, ctx_len, 0)
    consume(0, ctx_len, 0)
    n_lat = jnp.where(i < ctx_tiles, 0, n_kv_tiles)
    align = math.gcd(ctx_len, kv_tile)

    def tile_start(j):
        return pl.multiple_of(ctx_len + jnp.minimum(j, n_kv_tiles - 1) * kv_tile, align)

    @pl.when(n_lat > 0)
    def _():
        scores(tile_start(0), kv_tile, 0)

    def lat_body(jj, carry):
        j = 2 * jj
        scores(tile_start(j + 1), kv_tile, 1)
        consume(tile_start(j), kv_tile, 0)
        scores(tile_start(j + 2), kv_tile, 0)
        consume(tile_start(j + 1), kv_tile, 1)
        return carry

    lax.fori_loop(0, n_lat // 2, lat_body, 0)

    low_head = lax.broadcasted_iota(jnp.int32, (tq, LANES), 1) < HEAD_DIM
    for p in range(ATTN_HEADS // 2):
        outs = []
        for hh in (2 * p, 2 * p + 1):
            acc = acc_sc[hh * tq:(hh + 1) * tq, :]
            o = acc / pltpu.roll(acc, HEAD_DIM, 1)
            src_low = hh < per_kv
            dst_low = hh % 2 == 0
            outs.append(o if src_low == dst_low else pltpu.roll(o, HEAD_DIM, 1))
        o_ref[0, :, p * LANES:(p + 1) * LANES] = jnp.where(low_head, outs[0], outs[1])


def _attention(q, k, v, *, ctx_len):
    nb, nh, t, _ = q.shape
    tq = TOK_TILE
    n_lat = t - ctx_len
    kv_tile = 1024 if n_lat % 2048 == 0 else (512 if n_lat % 1024 == 0 else 256)
    assert (n_lat // kv_tile) % 2 == 0 and ctx_len <= kv_tile
    rows = nh * tq
    kern = functools.partial(_attn_kernel, ctx_len=ctx_len, ctx_tiles=ctx_len // tq, kv_tile=kv_tile,
                             n_kv_tiles=n_lat // kv_tile)
    return pl.pallas_call(
        kern,
        out_shape=jax.ShapeDtypeStruct((nb, t, ATTN_WIDTH), F32),
        grid=(nb, t // tq),
        in_specs=[pl.BlockSpec((1, nh, tq, LANES), lambda b, i: (b, 0, i, 0)),
                  pl.BlockSpec((1, t, KV_WIDTH), lambda b, i: (b, 0, 0)),
                  pl.BlockSpec((1, t, KV_WIDTH), lambda b, i: (b, 0, 0))],
        out_specs=pl.BlockSpec((1, tq, ATTN_WIDTH), lambda b, i: (b, i, 0)),
        scratch_shapes=[pltpu.VMEM((rows, LANES), F32), pltpu.VMEM((rows, LANES), F32),
                        pltpu.VMEM((2, rows, kv_tile), F32), pltpu.VMEM((2, rows, LANES), F32),
                        pltpu.VMEM((rows, kv_tile), BF16)],
        compiler_params=_cparams(("parallel", "arbitrary")),
        name="gqa_attention",
    )(q, k, v)


def _short_conv_kernel(x_ref, w_ref, b_ref, lat_ref, ctx_ref, *, ctx_len):
    x = x_ref[0]
    t = x.shape[0]
    row = lax.broadcasted_iota(jnp.int32, x.shape, 0)
    prev = jnp.where((row == 0) | (row == ctx_len), 0.0, pltpu.roll(x, 1, 0))
    nxt = jnp.where((row == ctx_len - 1) | (row == t - 1), 0.0, pltpu.roll(x, t - 1, 0))
    w = w_ref[...]
    u = prev * w[0:1, :] + x * w[1:2, :] + nxt * w[2:3, :] + b_ref[...]
    ctx_ref[0] = u[:ctx_len]
    lat_ref[0, 0] = u[ctx_len:]


def _short_conv(hy, conv_w, conv_b, *, ctx_len):
    nb, t, w = hy.shape
    per = HYENA_WIDTH // LANES
    return pl.pallas_call(
        functools.partial(_short_conv_kernel, ctx_len=ctx_len),
        out_shape=(jax.ShapeDtypeStruct((3, nb, t - ctx_len, HYENA_WIDTH), F32),
                   jax.ShapeDtypeStruct((nb, ctx_len, w), F32)),
        grid=(nb, w // LANES),
        in_specs=[pl.BlockSpec((1, t, LANES), lambda b, j: (b, 0, j)),
                  pl.BlockSpec((3, LANES), lambda b, j: (0, j)),
                  pl.BlockSpec((1, LANES), lambda b, j: (0, j))],
        out_specs=(pl.BlockSpec((1, 1, t - ctx_len, LANES), lambda b, j: (j // per, b, 0, j % per)),
                   pl.BlockSpec((1, ctx_len, LANES), lambda b, j: (b, 0, j))),
        compiler_params=_cparams(("parallel", "parallel")),
        name="hyena_short_conv",
    )(hy, conv_w, conv_b.reshape(1, w))


def _filter_kernel(z_ref, w1_ref, b1_ref, w2_ref, b2_ref, w3_ref, fr_ref, dl_ref, o_ref, *, n):
    r = pl.program_id(0)
    z = z_ref[...]
    fr = fr_ref[...]
    a = jnp.sin(fr * (_hdot(z, w1_ref[...]) + b1_ref[...]))
    a = jnp.sin(fr * (_hdot(a, w2_ref[...]) + b2_ref[...]))
    h = _hdot(a, w3_ref[0])
    window = jnp.exp(-z[:, 0:1] * dl_ref[...])
    row = r * z.shape[0] + lax.broadcasted_iota(jnp.int32, h.shape, 0)
    o_ref[...] = jnp.where(row == n, 0.0, h * window)


def _filter_features(n):
    t = jnp.linspace(0.0, 1.0, n, dtype=F32)[:, None]
    omega = (2.0 * math.pi / n) * jnp.arange(n, dtype=F32)
    bands = jnp.linspace(1e-4, FILTER_BANDS - 1, FILTER_BANDS, dtype=F32)
    phase = omega[:, None] * bands[None, :]
    z = jnp.concatenate([t, jnp.cos(phase), -jnp.sin(phase)], axis=-1)
    src = np.concatenate([np.arange(n), [0], np.arange(n - 1, 0, -1)])
    zc = z[src]
    return jnp.pad(zc, ((0, 0), (0, LANES - zc.shape[1])))


def _hyena_filter(zc, w1p, b1, w2, b2, w3r, freq, deltas, *, n):
    rt = min(512, n)
    width = HYENA_ORDER * HYENA_WIDTH
    const = lambda shape: pl.BlockSpec(shape, lambda r: (0,) * len(shape))
    return pl.pallas_call(
        functools.partial(_filter_kernel, n=n),
        out_shape=jax.ShapeDtypeStruct((2 * n, width), F32),
        grid=(2 * n // rt,),
        in_specs=[pl.BlockSpec((rt, LANES), lambda r: (r, 0)), const(w1p.shape), const(b1.shape), const(w2.shape),
                  const(b2.shape), pl.BlockSpec((1, FILTER_HIDDEN, width), lambda r: (r // (n // rt), 0, 0)),
                  const(freq.shape), const(deltas.shape)],
        out_specs=pl.BlockSpec((rt, width), lambda r: (r, 0)),
        compiler_params=_cparams(("parallel",)),
        name="hyena_filter",
    )(zc, w1p, b1, w2, b2, w3r, freq, deltas)


def _dft_consts(n):
    big = 2 * n
    n1 = big // DFT_INNER
    half = n1 // 2
    k1 = np.arange(n1)[:, None]
    a = np.arange(n1)[None, :]
    ang = -2.0 * np.pi * ((k1 * a) % n1) / n1
    fr, fi = np.cos(ang), np.sin(ang)
    first = np.block([[fr[:, :half], -fi[:, :half]], [fi[:, :half], fr[:, :half]]])
    first_real = np.concatenate([fr, fi], axis=0)
    last = np.block([[fr[:half, :], fi[:half, :]], [-fi[:half, :], fr[:half, :]]]) / big
    k2 = np.arange(DFT_INNER)[:, None]
    s = np.arange(DFT_INNER)[None, :]
    ang2 = -2.0 * np.pi * ((k2 * s) % DFT_INNER) / DFT_INNER
    gr, gi = np.cos(ang2), np.sin(ang2)
    inner = np.block([[gr, -gi], [gi, gr]])
    inner_inv = np.block([[gr, gi], [-gi, gr]])
    prod = (jnp.arange(n1, dtype=jnp.int32)[:, None] * jnp.arange(DFT_INNER, dtype=jnp.int32)[None, :]) % big
    angt = prod.astype(F32) * (-2.0 * math.pi / big)
    tw = jnp.stack([jnp.cos(angt), jnp.sin(angt)])
    tw = jnp.broadcast_to(tw[..., None], tw.shape + (LANES,))
    f = lambda m: jnp.asarray(np.ascontiguousarray(m), F32)
    return dict(first=f(first), first_real=f(first_real), last=f(last), inner=f(inner), inner_inv=f(inner_inv),
                tw=tw, n1=n1)


def _left_matmul_kernel(*refs, mode):
    if mode == "first":
        m_ref, x_ref, o_ref = refs
        o_ref[...] = _hdot(m_ref[...], x_ref[...])
        return
    if mode == "last":
        m3_ref, b_ref, g_ref, v_ref, bias_ref, z_ref = refs
    else:
        m3_ref, b_ref, g_ref, v_ref, bias_ref, m1_ref, z_ref, a_ref = refs
    y = _hdot(m3_ref[...], b_ref[...])
    v = v_ref[...]
    z = g_ref[...] * (y + bias_ref[...] * v)
    z_ref[...] = z
    if mode == "last_first":
        a_ref[...] = _hdot(m1_ref[...], z)


def _dft_first(m1, x2):
    rows, wtot = x2.shape
    wt = 2048
    return pl.pallas_call(
        functools.partial(_left_matmul_kernel, mode="first"),
        out_shape=jax.ShapeDtypeStruct((m1.shape[0], wtot), F32),
        grid=(wtot // wt,),
        in_specs=[pl.BlockSpec(m1.shape, lambda j: (0, 0)), pl.BlockSpec((rows, wt), lambda j: (0, j))],
        out_specs=pl.BlockSpec((m1.shape[0], wt), lambda j: (0, j)),
        compiler_params=_cparams(("parallel",)),
        name="dft_outer_forward",
    )(m1, x2)


def _dft_last(m3, b2, gate, v, bias_flat, m1=None):
    rows, wtot = gate.shape
    wt = 2048
    col = lambda r: pl.BlockSpec((r, wt), lambda j: (0, j))
    const = lambda a: pl.BlockSpec(a.shape, lambda j: (0, 0))
    if m1 is None:
        return pl.pallas_call(
            functools.partial(_left_matmul_kernel, mode="last"),
            out_shape=jax.ShapeDtypeStruct((rows, wtot), F32),
            grid=(wtot // wt,),
            in_specs=[const(m3), col(b2.shape[0]), col(rows), col(rows), pl.BlockSpec((1, wt), lambda j: (0, 0))],
            out_specs=col(rows),
            compiler_params=_cparams(("parallel",)),
            name="dft_outer_inverse",
        )(m3, b2, gate, v, bias_flat)
    return pl.pallas_call(
        functools.partial(_left_matmul_kernel, mode="last_first"),
        out_shape=(jax.ShapeDtypeStruct((rows, wtot), F32), jax.ShapeDtypeStruct((m1.shape[0], wtot), F32)),
        grid=(wtot // wt,),
        in_specs=[const(m3), col(b2.shape[0]), col(rows), col(rows), pl.BlockSpec((1, wt), lambda j: (0, 0)),
                  const(m1)],
        out_specs=(col(rows), col(m1.shape[0])),
        compiler_params=_cparams(("parallel",)),
        name="dft_outer_inverse_forward",
    )(m3, b2, gate, v, bias_flat, m1)


def _dft_mid_kernel(a_ref, tw_ref, g_ref, *rest, slabs, with_filter):
    if with_filter:
        h_ref, gi_ref, o_ref = rest
    else:
        (o_ref,) = rest
    nl = a_ref.shape[3] // LANES
    for j in range(slabs):
        twr = jnp.concatenate([tw_ref[0, j]] * nl, axis=1)
        twi = jnp.concatenate([tw_ref[1, j]] * nl, axis=1)
        ar, ai = a_ref[0, j], a_ref[1, j]
        br = ar * twr - ai * twi
        bi = ar * twi + ai * twr
        x = _hdot(g_ref[...], jnp.concatenate([br, bi], axis=0))
        if not with_filter:
            o_ref[0, j] = x[:DFT_INNER]
            o_ref[1, j] = x[DFT_INNER:]
            continue
        xr, xi = x[:DFT_INNER], x[DFT_INNER:]
        hr, hi = h_ref[0, j], h_ref[1, j]
        yr = xr * hr - xi * hi
        yi = xr * hi + xi * hr
        zz = _hdot(gi_ref[...], jnp.concatenate([yr, yi], axis=0))
        zr, zi = zz[:DFT_INNER], zz[DFT_INNER:]
        o_ref[0, j] = zr * twr + zi * twi
        o_ref[1, j] = zi * twr - zr * twi


def _dft_mid(a4, tw, inner, spec=None, inner_inv=None, order=0):
    _, n1, _, c = a4.shape
    slabs = 8 if n1 % 8 == 0 else n1
    blk = lambda w: pl.BlockSpec((2, slabs, DFT_INNER, w), lambda j: (0, j, 0, 0))
    const = lambda a: pl.BlockSpec(a.shape, lambda j: (0, 0))
    in_specs = [blk(c), blk(LANES), const(inner)]
    args = [a4, tw, inner]
    if spec is not None:
        in_specs += [pl.BlockSpec((2, slabs, DFT_INNER, c), lambda j: (0, j, 0, order)), const(inner_inv)]
        args += [spec, inner_inv]
    return pl.pallas_call(
        functools.partial(_dft_mid_kernel, slabs=slabs, with_filter=spec is not None),
        out_shape=jax.ShapeDtypeStruct(a4.shape, F32),
        grid=(n1 // slabs,),
        in_specs=in_specs,
        out_specs=blk(c),
        compiler_params=_cparams(("parallel",)),
        name="dft_inner_filter" if spec is not None else "dft_inner_forward",
    )(*args)


def _hyena_latent(vx, h_circ, bias, consts):
    _, nb, n, c = vx.shape
    assert nb == 2, "the two samples are carried as the real and imaginary part of one transform"
    n1 = consts["n1"]
    half = n1 // 2
    wflat = DFT_INNER * c
    flat = vx.reshape(3, nb * half, wflat)
    v_f, x1_f, x2_f = flat[0], flat[1], flat[2]
    bias_flat = jnp.tile(bias, (1, 2048 // c))

    hf = h_circ.reshape(n1, DFT_INNER * h_circ.shape[1])
    ha = _dft_first(consts["first_real"], hf).reshape(2, n1, DFT_INNER, h_circ.shape[1])
    spec = _dft_mid(ha, consts["tw"], consts["inner"])

    a = _dft_first(consts["first"], v_f).reshape(2, n1, DFT_INNER, c)
    bm = _dft_mid(a, consts["tw"], consts["inner"], spec, consts["inner_inv"], order=0)
    z, a = _dft_last(consts["last"], bm.reshape(2 * n1, wflat), x1_f, v_f, bias_flat[0:1], consts["first"])
    bm = _dft_mid(a.reshape(2, n1, DFT_INNER, c), consts["tw"], consts["inner"], spec, consts["inner_inv"], order=1)
    y = _dft_last(consts["last"], bm.reshape(2 * n1, wflat), x2_f, z, bias_flat[1:2])
    return y.reshape(nb, n, c)


def _hyena_ctx_kernel(u_ref, h_ref, fs_ref, ff_ref, fi_ref, bias_ref, o_ref, *, n):
    big = 2 * n
    c = HYENA_WIDTH
    spec = _hdot(fs_ref[...], h_ref[...])
    u = u_ref[...]
    v, x1, x2 = u[:, 0:c], u[:, c:2 * c], u[:, 2 * c:3 * c]

    def conv(zin, order):
        x = _hdot(ff_ref[...], zin)
        xr, xi = x[:big], x[big:]
        hr, hi = spec[:big, order * c:(order + 1) * c], spec[big:, order * c:(order + 1) * c]
        y = jnp.concatenate([xr * hr - xi * hi, xr * hi + xi * hr], axis=0)
        return _hdot(fi_ref[...], y)

    bias = bias_ref[...]
    z = x1 * (conv(v, 0) + bias[0:1] * v)
    o_ref[...] = x2 * (conv(z, 1) + bias[1:2] * z)


def _ctx_dft_consts(n):
    big = 2 * n
    k = np.arange(big)[:, None]
    t = np.arange(big)[None, :]
    ang = -2.0 * np.pi * ((k * t) % big) / big
    fr, fi = np.cos(ang), np.sin(ang)
    spec_m = np.concatenate([fr, fi], axis=0)
    fwd = np.block([[fr[:, :n], -fi[:, :n]], [fi[:, :n], fr[:, :n]]])
    inv = np.block([[fr.T[:n, :], fi.T[:n, :]], [-fi.T[:n, :], fr.T[:n, :]]]) / big
    f = lambda m: jnp.asarray(np.ascontiguousarray(m), F32)
    return f(spec_m), f(fwd), f(inv)


def _hyena_ctx(u_ctx, h_circ, bias, cconsts):
    nb, n, w = u_ctx.shape
    assert nb == 2
    fs, ff, fi = cconsts
    full = lambda a: pl.BlockSpec(a.shape, lambda i: (0,) * a.ndim)
    u2 = u_ctx.reshape(nb * n, w)
    out = pl.pallas_call(
        functools.partial(_hyena_ctx_kernel, n=n),
        out_shape=jax.ShapeDtypeStruct((nb * n, HYENA_WIDTH), F32),
        grid=(1,),
        in_specs=[full(u2), full(h_circ), full(fs), full(ff), full(fi), full(bias)],
        out_specs=pl.BlockSpec((nb * n, HYENA_WIDTH), lambda i: (0, 0)),
        compiler_params=_cparams(("arbitrary",)),
        name="hyena_context",
    )(u2, h_circ, fs, ff, fi, bias)
    return out.reshape(nb, n, HYENA_WIDTH)


def _gla_kernel(x_ref, o_ref, st_ref, *, backward):
    s = pl.program_id(1)

    @pl.when(s == 0)
    def _():
        st_ref[...] = jnp.zeros(st_ref.shape, F32)

    c = GLA_CHUNK
    kw, vw = GLA_K_WIDTH, GLA_V_WIDTH
    ri = lax.broadcasted_iota(jnp.int32, (c, c), 0)
    ci = lax.broadcasted_iota(jnp.int32, (c, c), 1)
    tri = ((ri <= ci) if backward else (ri >= ci)).astype(F32)
    kr = lax.broadcasted_iota(jnp.int32, (GLA_HEADS * c, kw), 0) // c
    kc = lax.broadcasted_iota(jnp.int32, (GLA_HEADS * c, kw), 1) // GLA_DK
    mask_k = kr == kc
    vr = lax.broadcasted_iota(jnp.int32, (GLA_HEADS * c, vw), 0) // c
    vc = lax.broadcasted_iota(jnp.int32, (GLA_HEADS * c, vw), 1) // GLA_DV
    mask_v = vr == vc
    sr = lax.broadcasted_iota(jnp.int32, (vw, kw), 0) // GLA_DV
    sc = lax.broadcasted_iota(jnp.int32, (vw, kw), 1) // GLA_DK
    mask_s = sr == sc
    qi = lax.broadcasted_iota(jnp.int32, (c, GLA_HEADS * c), 0)
    kj = lax.broadcasted_iota(jnp.int32, (c, GLA_HEADS * c), 1) % c
    causal = (qi <= kj) if backward else (qi >= kj)

    n_chunks = x_ref.shape[1] // c
    order = range(n_chunks - 1, -1, -1) if backward else range(n_chunks)
    goff = 2 * kw + vw + (kw if backward else 0)
    for ch in order:
        r0 = ch * c
        q = x_ref[0, r0:r0 + c, 0:kw]
        k = x_ref[0, r0:r0 + c, kw:2 * kw]
        v = x_ref[0, r0:r0 + c, 2 * kw:2 * kw + vw]
        g = x_ref[0, r0:r0 + c, goff:goff + kw]
        gc = _hdot(tri, g)
        g_last = gc[0:1] if backward else gc[c - 1:c]
        q_dec = q * jnp.exp(gc)
        k_dec = k * jnp.exp(-gc)
        k_tail = k * jnp.exp(g_last - gc)
        k_blk = jnp.where(mask_k, jnp.concatenate([k_dec] * GLA_HEADS, axis=0), 0.0).astype(BF16)
        v_bf = v.astype(BF16)
        v_blk = jnp.where(mask_v, jnp.concatenate([v_bf] * GLA_HEADS, axis=0), jnp.zeros((), BF16))
        q_bf = q_dec.astype(BF16)
        scores = lax.dot_general(q_bf, k_blk, (((1,), (1,)), ((), ())), preferred_element_type=F32)
        scores = jnp.where(causal, scores, 0.0)
        st = st_ref[...]
        o = jnp.dot(scores.astype(BF16), v_blk, preferred_element_type=F32)
        o = o + lax.dot_general(q_bf, st.astype(BF16), (((1,), (1,)), ((), ())), preferred_element_type=F32)
        o_ref[0, r0:r0 + c, :] = o
        kv_t = lax.dot_general(v_bf, k_tail.astype(BF16), (((0,), (0,)), ((), ())), preferred_element_type=F32)
        st_ref[...] = st * jnp.exp(g_last) + jnp.where(mask_s, kv_t, 0.0)


def _gla(gla_in, *, ctx_len, backward):
    nb, t, w = gla_in.shape
    tb = TOK_TILE
    nblk = t // tb
    cblk = ctx_len // tb
    if backward:
        idx = lambda b, s: (b, jnp.where(s < cblk, cblk - 1 - s, nblk - 1 - (s - cblk)), 0)
    else:
        idx = lambda b, s: (b, s, 0)
    return pl.pallas_call(
        functools.partial(_gla_kernel, backward=backward),
        out_shape=jax.ShapeDtypeStruct((nb, t, GLA_V_WIDTH), F32),
        grid=(nb, nblk),
        in_specs=[pl.BlockSpec((1, tb, w), idx)],
        out_specs=pl.BlockSpec((1, tb, GLA_V_WIDTH), idx),
        scratch_shapes=[pltpu.VMEM((GLA_V_WIDTH, GLA_K_WIDTH), F32)],
        compiler_params=_cparams(("parallel", "arbitrary")),
        name="gla_backward" if backward else "gla_forward",
    )(gla_in)


def _out_kernel(h_ref, mod_ref, at_ref, hy_ref, gf_ref, gb_ref, gr_ref, gn_ref, gmat_ref, w_ref, o_ref, *,
                n_batch, ctx_tiles, tile_off, d_model):
    b, i = pl.program_id(0), pl.program_id(1)
    mod = _mod_row(mod_ref, i + tile_off < ctx_tiles, b, n_batch)
    gate = mod[:, 2 * d_model:3 * d_model]
    gmat = gmat_ref[...]
    gn = gn_ref[...]
    a0, a1 = ATTN_WIDTH, ATTN_WIDTH + HYENA_WIDTH
    ya = _head_rms(at_ref[0], gn[:, :a0], gmat)
    yh = _head_rms(hy_ref[0], gn[:, a0:a1], gmat)
    yg = _head_rms(gf_ref[0] + gb_ref[0], gn[:, a1:], gmat) * _silu(gr_ref[0])
    y = jnp.concatenate([ya, yh, yg], axis=1)
    o_ref[0] = h_ref[0] + gate * _bdot(y, w_ref[...])


def _out_projection(h, mod, attn, hy, gla_f, gla_b, gr, gn, gmat, w_out, *, ctx_len, skip_ctx):
    nb, t, d = h.shape
    tb = TOK_TILE
    off = ctx_len // tb if skip_ctx else 0
    kern = functools.partial(_out_kernel, n_batch=nb, ctx_tiles=ctx_len // tb, tile_off=off, d_model=d)
    const = lambda a: pl.BlockSpec(a.shape, lambda b, i: (0,) * a.ndim)
    tok = lambda w: pl.BlockSpec((1, tb, w), lambda b, i: (b, i + off, 0))
    return pl.pallas_call(
        kern,
        out_shape=jax.ShapeDtypeStruct((nb, t - off * tb, d), F32),
        grid=(nb, t // tb - off),
        in_specs=[tok(d), const(mod), tok(ATTN_WIDTH), tok(HYENA_WIDTH), tok(GLA_V_WIDTH), tok(GLA_V_WIDTH),
                  tok(GLA_V_WIDTH), const(gn), const(gmat), const(w_out)],
        out_specs=pl.BlockSpec((1, tb, d), lambda b, i: (b, i, 0)),
        compiler_params=_cparams(("parallel", "parallel")),
        name="out_projection",
    )(h, mod, attn, hy, gla_f, gla_b, gr, gn, gmat, w_out)


def _ffn_kernel(h_ref, mod_ref, g2_ref, w1_ref, w3_ref, w2_ref, fg_ref, o_ref, *, n_batch, ctx_tiles, tile_off,
                d_model, final):
    b, i = pl.program_id(0), pl.program_id(1)
    mod = _mod_row(mod_ref, i + tile_off < ctx_tiles, b, n_batch)
    shift, scale, gate = (mod[:, 3 * d_model:4 * d_model], mod[:, 4 * d_model:5 * d_model],
                          mod[:, 5 * d_model:6 * d_model])
    x = h_ref[0]
    xn = x * lax.rsqrt(jnp.mean(jnp.square(x), axis=-1, keepdims=True) + NORM_EPS) * g2_ref[...]
    u = (xn * (1.0 + scale) + shift).astype(BF16)
    a = jnp.dot(u, w1_ref[...], preferred_element_type=F32)
    c = jnp.dot(u, w3_ref[...], preferred_element_type=F32)
    hid = (_silu(a) * c).astype(BF16)
    y = x + gate * jnp.dot(hid, w2_ref[...], preferred_element_type=F32)
    if final:
        y = y * lax.rsqrt(jnp.mean(jnp.square(y), axis=-1, keepdims=True) + NORM_EPS) * fg_ref[...]
    o_ref[0] = y


def _ffn(h, mod, g2, w1, w3, w2, fg, *, ctx_len, lat_only, final):
    nb, t, d = h.shape
    tb = TOK_TILE
    off = ctx_len // tb if lat_only else 0
    kern = functools.partial(_ffn_kernel, n_batch=nb, ctx_tiles=ctx_len // tb, tile_off=off, d_model=d, final=final)
    const = lambda a: pl.BlockSpec(a.shape, lambda b, i: (0,) * a.ndim)
    tok = pl.BlockSpec((1, tb, d), lambda b, i: (b, i, 0))
    return pl.pallas_call(
        kern,
        out_shape=jax.ShapeDtypeStruct((nb, t, d), F32),
        grid=(nb, t // tb),
        in_specs=[tok, const(mod), const(g2), const(w1), const(w3), const(w2), const(fg)],
        out_specs=tok,
        compiler_params=_cparams(("parallel", "parallel")),
        name="swiglu_ffn",
    )(h, mod, g2, w1, w3, w2, fg)


def _rope_tables(n_lat, ctx_len):
    rows = n_lat // GRID_W
    row = jnp.broadcast_to(jnp.arange(rows, dtype=F32)[:, None], (rows, GRID_W)).reshape(-1)
    col = jnp.broadcast_to(jnp.arange(GRID_W, dtype=F32)[None, :], (rows, GRID_W)).reshape(-1)
    n_freq = HEAD_DIM // 4
    inv_freq = jnp.power(ROPE_THETA, -jnp.arange(n_freq, dtype=F32) / n_freq)
    ang = jnp.concatenate([row[:, None] * inv_freq, col[:, None] * inv_freq], axis=-1)
    cos, sin = jnp.cos(ang), jnp.sin(ang)
    cos = jnp.concatenate([jnp.ones((ctx_len, HEAD_DIM // 2), F32), cos], axis=0)
    sin = jnp.concatenate([jnp.zeros((ctx_len, HEAD_DIM // 2), F32), sin], axis=0)
    cosf = jnp.tile(cos, (1, 2 * LANES // HEAD_DIM))
    sinf = jnp.tile(jnp.concatenate([-sin, sin], axis=1), (1, LANES // HEAD_DIM))
    return cosf, sinf


def kernel(x, c, ctx, c_ctx, ada_w, ada_b, norm1_g, w_in, q_norm_g, k_norm_g, hy_conv_w, hy_conv_b, filt_w1, filt_b1, filt_w2, filt_b2, filt_w3, filt_freq, hy_bias, gla_gate_w, gla_gate_b, out_norm_g, w_out, norm2_g, ffn_w1, ffn_w3, ffn_w2, final_norm_g):
    nb, n_lat, d = x.shape
    ctx_len = ctx.shape[1]
    depth = w_in.shape[0]
    assert nb == 2 and ctx_len % TOK_TILE == 0 and n_lat % TOK_TILE == 0 and n_lat % DFT_INNER == 0

    h = jnp.concatenate([ctx, x], axis=1)
    cond = jnp.concatenate([c, c_ctx[None, :], jnp.zeros((8 - nb - 1, d), F32)], axis=0)
    mods = _modulation(cond, ada_w, ada_b)

    cosf, sinf = _rope_tables(n_lat, ctx_len)
    lane = np.arange(LANES)
    gmat = jnp.asarray((lane[:, None] // HEAD_DIM) == (lane[None, :] // HEAD_DIM), BF16)
    consts = _dft_consts(n_lat)
    cconsts = _ctx_dft_consts(ctx_len)
    zc_lat = _filter_features(n_lat)
    zc_ctx = _filter_features(ctx_len)
    deltas = jnp.abs(jnp.linspace(HYENA_MIN_DECAY, HYENA_MAX_DECAY, HYENA_WIDTH, dtype=F32))
    deltas = jnp.tile(deltas[None, :], (1, HYENA_ORDER))

    w_in_b = jnp.pad(w_in, ((0, 0), (0, 0), (0, IN_PAD - w_in.shape[2]))).astype(BF16)
    w_out_b = w_out.astype(BF16)
    w1_b, w3_b, w2_b = ffn_w1.astype(BF16), ffn_w3.astype(BF16), ffn_w2.astype(BF16)
    fg = final_norm_g.reshape(1, d)

    out = None
    for l in range(depth):
        last = l == depth - 1
        mod = mods[l]
        wg = jnp.zeros((LANES, 2 * GLA_K_WIDTH), F32)
        wg = wg.at[0:GLA_GATE_RANK, 0:GLA_K_WIDTH].set(gla_gate_w[l, 0])
        wg = wg.at[GLA_GATE_RANK:2 * GLA_GATE_RANK, GLA_K_WIDTH:].set(gla_gate_w[l, 1])
        bg = gla_gate_b[l].reshape(1, 2 * GLA_K_WIDTH)
        qg = jnp.tile(q_norm_g[l][None, :], (1, ATTN_HEADS))
        kg = jnp.tile(k_norm_g[l][None, :], (1, ATTN_KV_HEADS))

        q, k, v, hy, gla_in, gr = _in_projection(h, mod, norm1_g[l].reshape(1, d), w_in_b[l], qg, kg, cosf, sinf,
                                                 gmat, wg.astype(BF16), bg, ctx_len=ctx_len)
        attn = _attention(q, k, v, ctx_len=ctx_len)

        vx, u_ctx = _short_conv(hy, hy_conv_w[l], hy_conv_b[l], ctx_len=ctx_len)
        w1p = jnp.pad(filt_w1[l], ((0, LANES - filt_w1.shape[1]), (0, 0)))
        w3r = filt_w3[l].reshape(FILTER_HIDDEN, HYENA_ORDER, 2, HYENA_WIDTH).transpose(2, 0, 1, 3)
        w3r = w3r.reshape(2, FILTER_HIDDEN, HYENA_ORDER * HYENA_WIDTH)
        fargs = (w1p, filt_b1[l][None, :], filt_w2[l], filt_b2[l][None, :], w3r, filt_freq[l][None, :], deltas)
        h_lat = _hyena_filter(zc_lat, *fargs, n=n_lat)
        hy_lat = _hyena_latent(vx, h_lat, hy_bias[l], consts)
        if not last:
            h_ctx = _hyena_filter(zc_ctx, *fargs, n=ctx_len)
            hy_ctx = _hyena_ctx(u_ctx, h_ctx, hy_bias[l], cconsts)
        else:
            hy_ctx = jnp.zeros((nb, ctx_len, HYENA_WIDTH), F32)
        hy_out = jnp.concatenate([hy_ctx, hy_lat], axis=1)

        gla_f = _gla(gla_in, ctx_len=ctx_len, backward=False)
        gla_b = _gla(gla_in, ctx_len=ctx_len, backward=True)

        h = _out_projection(h, mod, attn, hy_out, gla_f, gla_b, gr, out_norm_g[l].reshape(1, -1), gmat, w_out_b[l],
                            ctx_len=ctx_len, skip_ctx=last)
        h = _ffn(h, mod, norm2_g[l].reshape(1, d), w1_b[l], w3_b[l], w2_b[l], fg, ctx_len=ctx_len, lat_only=last,
                 final=last)
    return h
```

```python
import functools
import math

import numpy as np
import jax
import jax.numpy as jnp
from jax import lax
from jax.experimental import pallas as pl
from jax.experimental.pallas import tpu as pltpu

F32 = jnp.float32
BF16 = jnp.bfloat16
HI = lax.Precision.HIGHEST

LANES = 128
VMEM_LIMIT = 56 * 1024 * 1024

HEAD_DIM = 64
ATTN_HEADS = 8
ATTN_KV_HEADS = 2
ATTN_WIDTH = ATTN_HEADS * HEAD_DIM
KV_WIDTH = ATTN_KV_HEADS * HEAD_DIM
HYENA_WIDTH = 256
HYENA_ORDER = 2
FILTER_BANDS = 16
FILTER_HIDDEN = 64
GLA_HEADS = 4
GLA_DK = 32
GLA_DV = 64
GLA_K_WIDTH = GLA_HEADS * GLA_DK
GLA_V_WIDTH = GLA_HEADS * GLA_DV
GLA_GATE_RANK = 16
GLA_GATE_TAU = 16.0
GLA_CHUNK = 64
GRID_W = 64
ROPE_THETA = 10000.0
N_MOD = 6
NORM_EPS = 1e-6
HYENA_TARGET = 1e-2
HYENA_MIN_DECAY = math.log(HYENA_TARGET) / 1.5
HYENA_MAX_DECAY = math.log(HYENA_TARGET) / 0.3

TOK_TILE = 256
DFT_INNER = 128
IN_PAD = 2432

_OQ, _OK, _OV, _OHY = 0, 512, 640, 768
_OGQ, _OGK, _OGV, _OGR, _OGA = 1536, 1664, 1792, 2048, 2304


def _cparams(sem, vmem=VMEM_LIMIT):
    return pltpu.CompilerParams(dimension_semantics=sem, vmem_limit_bytes=vmem)


def _bdot(a, b):
    return jnp.dot(a.astype(BF16), b.astype(BF16), preferred_element_type=F32)


def _hdot(a, b):
    return jnp.dot(a, b, preferred_element_type=F32, precision=HI)


def _group_mean_square(x, gmat):
    outs = []
    for j in range(x.shape[1] // LANES):
        sq = jnp.square(x[:, j * LANES:(j + 1) * LANES])
        hi = sq.astype(BF16)
        lo = (sq - hi.astype(F32)).astype(BF16)
        s = jnp.dot(hi, gmat, preferred_element_type=F32) + jnp.dot(lo, gmat, preferred_element_type=F32)
        outs.append(s * (1.0 / HEAD_DIM))
    return outs[0] if len(outs) == 1 else jnp.concatenate(outs, axis=1)


def _head_rms(x, gain, gmat):
    ms = _group_mean_square(x, gmat)
    return x * lax.rsqrt(ms + NORM_EPS) * gain


def _mod_row(mod_ref, is_ctx, b, n_batch):
    row = jnp.where(is_ctx, n_batch, b)
    return mod_ref[pl.ds(row, 1), :]


def _silu(x):
    return x * (1.0 / (1.0 + jnp.exp(-x)))


def _split3(x):
    hi = x.astype(BF16).astype(F32)
    mid = (x - hi).astype(BF16).astype(F32)
    lo = (x - hi - mid).astype(BF16).astype(F32)
    return hi, mid, lo


def _mod_kernel(c_ref, w_ref, b_ref, o_ref):
    s = _silu(c_ref[...])
    o_ref[0] = _bdot(s, w_ref[0]) + b_ref[0]


def _modulation(cond_rows, ada_w, ada_b):
    depth, d, w = ada_w.shape
    ct = 1536
    return pl.pallas_call(
        _mod_kernel,
        out_shape=jax.ShapeDtypeStruct((depth, 8, w), F32),
        grid=(depth, w // ct),
        in_specs=[pl.BlockSpec((8, d), lambda l, j: (0, 0)),
                  pl.BlockSpec((1, d, ct), lambda l, j: (l, 0, j)),
                  pl.BlockSpec((1, 1, ct), lambda l, j: (l, 0, j))],
        out_specs=pl.BlockSpec((1, 8, ct), lambda l, j: (l, 0, j)),
        compiler_params=_cparams(("arbitrary", "arbitrary")),
        name="adaln_modulation",
    )(cond_rows, ada_w, ada_b.reshape(depth, 1, w))


def _in_kernel(h_ref, mod_ref, g1_ref, w_ref, qg_ref, kg_ref, cos_ref, sin_ref, gmat_ref, wg_ref, bg_ref,
               q_ref, k_ref, v_ref, hy_ref, gla_ref, gr_ref, *, n_batch, ctx_tiles, d_model):
    b, i = pl.program_id(0), pl.program_id(1)
    mod = _mod_row(mod_ref, i < ctx_tiles, b, n_batch)
    shift, scale = mod[:, 0:d_model], mod[:, d_model:2 * d_model]
    x = h_ref[0]
    xn = x * lax.rsqrt(jnp.mean(jnp.square(x), axis=-1, keepdims=True) + NORM_EPS) * g1_ref[...]
    u = xn * (1.0 + scale) + shift
    proj = _bdot(u, w_ref[...])

    gmat = gmat_ref[...]
    cosf, sinf = cos_ref[...], sin_ref[...]
    lane = lax.broadcasted_iota(jnp.int32, (x.shape[0], LANES), 1)
    first_half = (lane % HEAD_DIM) < (HEAD_DIM // 2)
    low_head = lane < HEAD_DIM

    def rope(t):
        partner = jnp.where(first_half, pltpu.roll(t, LANES - HEAD_DIM // 2, 1), pltpu.roll(t, HEAD_DIM // 2, 1))
        return t * cosf + partner * sinf

    qn = _head_rms(proj[:, _OQ:_OQ + ATTN_WIDTH], qg_ref[...], gmat)
    zero = jnp.zeros((x.shape[0], LANES), F32)
    for p in range(ATTN_HEADS // 2):
        t = rope(qn[:, p * LANES:(p + 1) * LANES]) * (HEAD_DIM ** -0.5 * math.log2(math.e))
        tr = pltpu.roll(t, HEAD_DIM, 1)
        if 2 * p < ATTN_HEADS // ATTN_KV_HEADS:
            q_ref[0, 2 * p] = jnp.where(low_head, t, zero).astype(BF16)
            q_ref[0, 2 * p + 1] = jnp.where(low_head, tr, zero).astype(BF16)
        else:
            q_ref[0, 2 * p] = jnp.where(low_head, zero, tr).astype(BF16)
            q_ref[0, 2 * p + 1] = jnp.where(low_head, zero, t).astype(BF16)
    kn = _head_rms(proj[:, _OK:_OK + KV_WIDTH], kg_ref[...], gmat)
    k_ref[0] = rope(kn).astype(BF16)
    v_ref[0] = proj[:, _OV:_OV + KV_WIDTH].astype(BF16)
    hy_ref[0] = proj[:, _OHY:_OHY + 3 * HYENA_WIDTH]

    zg = _bdot(proj[:, _OGA:_OGA + LANES], wg_ref[...]) + bg_ref[...]
    log_gate = (jnp.minimum(zg, 0.0) - jnp.log(1.0 + jnp.exp(-jnp.abs(zg)))) * (1.0 / GLA_GATE_TAU)
    gla_ref[0, :, 0:GLA_K_WIDTH] = proj[:, _OGQ:_OGQ + GLA_K_WIDTH] * (GLA_DK ** -0.5)
    gla_ref[0, :, GLA_K_WIDTH:2 * GLA_K_WIDTH + GLA_V_WIDTH] = proj[:, _OGK:_OGK + GLA_K_WIDTH + GLA_V_WIDTH]
    gla_ref[0, :, 2 * GLA_K_WIDTH + GLA_V_WIDTH:] = log_gate
    gr_ref[0] = proj[:, _OGR:_OGR + GLA_V_WIDTH]


def _in_projection(h, mod, g1, w_in, qg, kg, cosf, sinf, gmat, wg, bg, *, ctx_len):
    nb, t, d = h.shape
    tb = TOK_TILE
    kern = functools.partial(_in_kernel, n_batch=nb, ctx_tiles=ctx_len // tb, d_model=d)
    const = lambda shape: pl.BlockSpec(shape, lambda b, i: (0,) * len(shape))
    tok = lambda w: pl.BlockSpec((1, tb, w), lambda b, i: (b, i, 0))
    gla_w = 2 * GLA_K_WIDTH + GLA_V_WIDTH + 2 * GLA_K_WIDTH
    return pl.pallas_call(
        kern,
        out_shape=(jax.ShapeDtypeStruct((nb, ATTN_HEADS, t, LANES), BF16),
                   jax.ShapeDtypeStruct((nb, t, KV_WIDTH), BF16),
                   jax.ShapeDtypeStruct((nb, t, KV_WIDTH), BF16),
                   jax.ShapeDtypeStruct((nb, t, 3 * HYENA_WIDTH), F32),
                   jax.ShapeDtypeStruct((nb, t, gla_w), F32),
                   jax.ShapeDtypeStruct((nb, t, GLA_V_WIDTH), F32)),
        grid=(nb, t // tb),
        in_specs=[tok(d), const(mod.shape), const(g1.shape), const(w_in.shape), const(qg.shape), const(kg.shape),
                  pl.BlockSpec((tb, LANES), lambda b, i: (i, 0)), pl.BlockSpec((tb, LANES), lambda b, i: (i, 0)),
                  const(gmat.shape), const(wg.shape), const(bg.shape)],
        out_specs=(pl.BlockSpec((1, ATTN_HEADS, tb, LANES), lambda b, i: (b, 0, i, 0)),
                   tok(KV_WIDTH), tok(KV_WIDTH), tok(3 * HYENA_WIDTH), tok(gla_w), tok(GLA_V_WIDTH)),
        compiler_params=_cparams(("parallel", "parallel")),
        name="in_projection",
    )(h, mod, g1, w_in, qg, kg, cosf, sinf, gmat, wg, bg)


def _attn_kernel(q_ref, k_ref, v_ref, o_ref, m_sc, acc_sc, s_sc, mx_sc, p_sc, *, ctx_len, ctx_tiles, kv_tile,
                 n_kv_tiles):
    i = pl.program_id(1)
    tq = q_ref.shape[2]
    per_kv = ATTN_HEADS // ATTN_KV_HEADS
    grp = per_kv * tq
    m_sc[...] = jnp.full(m_sc.shape, -jnp.inf, F32)
    acc_sc[...] = jnp.zeros(acc_sc.shape, F32)

    def scores(start, size, buf):
        q = q_ref[0].reshape(ATTN_HEADS * tq, LANES)
        kc = k_ref[0, pl.ds(start, size), :]
        s = lax.dot_general(q, kc, (((1,), (1,)), ((), ())), preferred_element_type=F32)
        s_sc[buf, :, 0:size] = s
        mx_sc[buf] = jnp.broadcast_to(jnp.max(s, axis=1, keepdims=True), mx_sc.shape[1:])

    def consume(start, size, buf):
        vc = v_ref[0, pl.ds(start, size), :]
        low = lax.broadcasted_iota(jnp.int32, vc.shape, 1) < HEAD_DIM
        one = jnp.ones((), BF16)
        v_ones = (jnp.where(low, vc, one), jnp.where(low, one, vc))
        for g in range(ATTN_KV_HEADS):
            alphas = []
            for hh in range(g * per_kv, (g + 1) * per_kv):
                rows = slice(hh * tq, (hh + 1) * tq)
                m_prev = m_sc[rows, :]
                m_new = jnp.maximum(m_prev, mx_sc[buf, rows, :])
                alphas.append(jnp.exp2(m_prev - m_new))
                for j in range(size // LANES):
                    cols = slice(j * LANES, (j + 1) * LANES)
                    p_sc[rows, cols] = jnp.exp2(s_sc[buf, rows, cols] - m_new).astype(BF16)
                m_sc[rows, :] = m_new
            rows_g = slice(g * grp, (g + 1) * grp)
            pv = jnp.dot(p_sc[rows_g, 0:size], v_ones[g], preferred_element_type=F32)
            acc_sc[rows_g, :] = jnp.concatenate(alphas, axis=0) * acc_sc[rows_g, :] + pv

    @pl.when(i < ctx_tiles)
    def _():
        scores(0, ctx_len, 0)
        consume(0, ctx_len, 0)

    @pl.when(i >= ctx_tiles)
    def _():
        def tile_start(j):
            return pl.multiple_of(j * kv_tile, kv_tile)

        def pair(jj, carry):
            j = 2 * jj
            scores(tile_start(j + 1), kv_tile, 1)
            consume(tile_start(j), kv_tile, 0)
            scores(tile_start(j + 2), kv_tile, 0)
            consume(tile_start(j + 1), kv_tile, 1)
            return carry

        scores(0, kv_tile, 0)
        lax.fori_loop(0, (n_kv_tiles - 1) // 2, pair, 0)
        last = (n_kv_tiles - 1) * kv_tile
        if n_kv_tiles % 2 == 0:
            scores(last, kv_tile, 1)
            consume(last - kv_tile, kv_tile, 0)
            consume(last, kv_tile, 1)
        else:
            consume(last, kv_tile, 0)

    low_head = lax.broadcasted_iota(jnp.int32, (tq, LANES), 1) < HEAD_DIM
    for p in range(ATTN_HEADS // 2):
        outs = []
        for hh in (2 * p, 2 * p + 1):
            acc = acc_sc[hh * tq:(hh + 1) * tq, :]
            o = acc / pltpu.roll(acc, HEAD_DIM, 1)
            src_low = hh < per_kv
            dst_low = hh % 2 == 0
            outs.append(o if src_low == dst_low else pltpu.roll(o, HEAD_DIM, 1))
        o_ref[0, :, p * LANES:(p + 1) * LANES] = jnp.where(low_head, outs[0], outs[1])


def _attention(q, k, v, *, ctx_len):
    nb, nh, t, _ = q.shape
    tq = TOK_TILE
    n_lat = t - ctx_len
    kv_tile = max(w for w in (256, 512, 768, 1024) if t % w == 0)
    assert ctx_len <= kv_tile
    rows = nh * tq
    kern = functools.partial(_attn_kernel, ctx_len=ctx_len, ctx_tiles=ctx_len // tq, kv_tile=kv_tile,
                             n_kv_tiles=t // kv_tile)
    return pl.pallas_call(
        kern,
        out_shape=jax.ShapeDtypeStruct((nb, t, ATTN_WIDTH), F32),
        grid=(nb, t // tq),
        in_specs=[pl.BlockSpec((1, nh, tq, LANES), lambda b, i: (b, 0, i, 0)),
                  pl.BlockSpec((1, t, KV_WIDTH), lambda b, i: (b, 0, 0)),
                  pl.BlockSpec((1, t, KV_WIDTH), lambda b, i: (b, 0, 0))],
        out_specs=pl.BlockSpec((1, tq, ATTN_WIDTH), lambda b, i: (b, i, 0)),
        scratch_shapes=[pltpu.VMEM((rows, LANES), F32), pltpu.VMEM((rows, LANES), F32),
                        pltpu.VMEM((2, rows, kv_tile), F32), pltpu.VMEM((2, rows, LANES), F32),
                        pltpu.VMEM((rows, kv_tile), BF16)],
        compiler_params=_cparams(("parallel", "arbitrary")),
        name="gqa_attention",
    )(q, k, v)


def _short_conv_kernel(x_ref, w_ref, b_ref, lat_ref, ctx_ref, *, ctx_len):
    x = x_ref[0]
    t = x.shape[0]
    row = lax.broadcasted_iota(jnp.int32, x.shape, 0)
    prev = jnp.where((row == 0) | (row == ctx_len), 0.0, pltpu.roll(x, 1, 0))
    nxt = jnp.where((row == ctx_len - 1) | (row == t - 1), 0.0, pltpu.roll(x, t - 1, 0))
    w = w_ref[...]
    u = prev * w[0:1, :] + x * w[1:2, :] + nxt * w[2:3, :] + b_ref[...]
    ctx_ref[0] = u[:ctx_len]
    lat_ref[0, 0] = u[ctx_len:]


def _short_conv(hy, conv_w, conv_b, *, ctx_len):
    nb, t, w = hy.shape
    per = HYENA_WIDTH // LANES
    return pl.pallas_call(
        functools.partial(_short_conv_kernel, ctx_len=ctx_len),
        out_shape=(jax.ShapeDtypeStruct((3, nb, t - ctx_len, HYENA_WIDTH), F32),
                   jax.ShapeDtypeStruct((nb, ctx_len, w), F32)),
        grid=(nb, w // LANES),
        in_specs=[pl.BlockSpec((1, t, LANES), lambda b, j: (b, 0, j)),
                  pl.BlockSpec((3, LANES), lambda b, j: (0, j)),
                  pl.BlockSpec((1, LANES), lambda b, j: (0, j))],
        out_specs=(pl.BlockSpec((1, 1, t - ctx_len, LANES), lambda b, j: (j // per, b, 0, j % per)),
                   pl.BlockSpec((1, ctx_len, LANES), lambda b, j: (b, 0, j))),
        compiler_params=_cparams(("parallel", "parallel")),
        name="hyena_short_conv",
    )(hy, conv_w, conv_b.reshape(1, w))


def _filter_kernel(z_ref, w1_ref, b1_ref, w2_ref, b2_ref, w3_ref, fr_ref, dl_ref, o_ref, *, n):
    r = pl.program_id(0)
    z = z_ref[...]
    fr = fr_ref[...]
    a = jnp.sin(fr * (_hdot(z, w1_ref[...]) + b1_ref[...]))
    a = jnp.sin(fr * (_hdot(a, w2_ref[...]) + b2_ref[...]))
    h = _hdot(a, w3_ref[0])
    window = jnp.exp(-z[:, 0:1] * dl_ref[...])
    row = r * z.shape[0] + lax.broadcasted_iota(jnp.int32, h.shape, 0)
    o_ref[...] = jnp.where(row == n, 0.0, h * window)


def _circular_features(r, n):
    p = jnp.where(r < n, r, 2 * n - r)
    p = jnp.where(r == n, 0, p).astype(F32)
    t = p / (n - 1)
    omega = (2.0 * math.pi / n) * p
    bands = jnp.linspace(1e-4, FILTER_BANDS - 1, FILTER_BANDS, dtype=F32).reshape((-1,) + (1,) * r.ndim)
    phase = omega[None] * bands
    return jnp.concatenate([t[None], jnp.cos(phase), -jnp.sin(phase)], axis=0)


def _filter_features(n):
    z = _circular_features(jnp.arange(2 * n, dtype=jnp.int32), n).T
    return jnp.pad(z, ((0, 0), (0, LANES - z.shape[1])))


def _hyena_filter(zc, w1p, b1, w2, b2, w3r, freq, deltas, *, n):
    rt = min(512, n)
    width = HYENA_ORDER * HYENA_WIDTH
    const = lambda shape: pl.BlockSpec(shape, lambda r: (0,) * len(shape))
    return pl.pallas_call(
        functools.partial(_filter_kernel, n=n),
        out_shape=jax.ShapeDtypeStruct((2 * n, width), F32),
        grid=(2 * n // rt,),
        in_specs=[pl.BlockSpec((rt, LANES), lambda r: (r, 0)), const(w1p.shape), const(b1.shape), const(w2.shape),
                  const(b2.shape), pl.BlockSpec((1, FILTER_HIDDEN, width), lambda r: (r // (n // rt), 0, 0)),
                  const(freq.shape), const(deltas.shape)],
        out_specs=pl.BlockSpec((rt, width), lambda r: (r, 0)),
        compiler_params=_cparams(("parallel",)),
        name="hyena_filter",
    )(zc, w1p, b1, w2, b2, w3r, freq, deltas)


FEAT_PAD = 40


def _filter_dft_kernel(zt_ref, w1t_ref, b1_ref, w2t_ref, b2_ref, w3_ref, fr_ref, e_ref, m1_ref, o_ref, *, st):
    j = pl.program_id(0)
    n1 = zt_ref.shape[2]
    width = w3_ref.shape[1]
    row = lax.broadcasted_iota(jnp.int32, (n1, width), 0)
    first_half = lax.broadcasted_iota(jnp.int32, (3 * FILTER_HIDDEN, n1), 1) < n1 // 2
    zero = jnp.zeros((), BF16)
    contract0 = (((0,), (0,)), ((), ()))
    fr = fr_ref[...]
    cols = []
    for q in range(st):
        zt = zt_ref[q]
        a = jnp.sin(fr * (_hdot(w1t_ref[...], zt) + b1_ref[...]))
        a = jnp.sin(fr * (_hdot(w2t_ref[...], a) + b2_ref[...]))
        a_hi = a.astype(BF16)
        a_lo = (a - a_hi.astype(F32)).astype(BF16)
        stack = jnp.concatenate([a_hi, a_hi, a_lo], axis=0)
        lhs = jnp.concatenate([jnp.where(first_half, stack, zero), jnp.where(first_half, zero, stack)], axis=0)
        h = lax.dot_general(lhs, w3_ref[...], contract0, preferred_element_type=F32)
        decay = lax.dot_general(zt.astype(BF16), e_ref[...], contract0, preferred_element_type=F32)
        h = h * jnp.exp(-decay)
        h = jnp.where((row == n1 // 2) & (j * st + q == 0), 0.0, h)
        cols.append(h.astype(BF16))
    o_ref[...] = jnp.dot(m1_ref[...], jnp.concatenate(cols, axis=1), preferred_element_type=F32).astype(BF16)


def _filter_outer_dft(zt, w1t, b1, w2t, b2, w3r, fr, e, m1):
    n_s, _, n1 = zt.shape
    width = w3r.shape[1]
    st = min(8, n_s)
    const = lambda a: pl.BlockSpec(a.shape, lambda j: (0,) * a.ndim)
    return pl.pallas_call(
        functools.partial(_filter_dft_kernel, st=st),
        out_shape=jax.ShapeDtypeStruct((m1.shape[0], n_s * width), BF16),
        grid=(n_s // st,),
        in_specs=[pl.BlockSpec((st, FEAT_PAD, n1), lambda j: (j, 0, 0)), const(w1t), const(b1), const(w2t), const(b2),
                  const(w3r), const(fr), const(e), const(m1)],
        out_specs=pl.BlockSpec((m1.shape[0], st * width), lambda j: (0, j)),
        compiler_params=_cparams(("parallel",)),
        name="hyena_filter_outer_dft",
    )(zt, w1t, b1, w2t, b2, w3r, fr, e, m1)


def _dft_consts(n):
    big = 2 * n
    n1 = big // DFT_INNER
    half = n1 // 2
    k1 = np.arange(n1)[:, None]
    a = np.arange(n1)[None, :]
    ang = -2.0 * np.pi * ((k1 * a) % n1) / n1
    fr, fi = np.cos(ang), np.sin(ang)
    first = np.block([[fr[:, :half], -fi[:, :half]], [fi[:, :half], fr[:, :half]]])
    first_real = np.concatenate([fr, fi], axis=0)
    last = np.block([[fr[:half, :], fi[:half, :]], [-fi[:half, :], fr[:half, :]]])
    k2 = np.arange(DFT_INNER)[:, None]
    s = np.arange(DFT_INNER)[None, :]
    ang2 = -2.0 * np.pi * ((k2 * s) % DFT_INNER) / DFT_INNER
    gr, gi = np.cos(ang2), np.sin(ang2)
    inner = np.block([[gr, -gi], [gi, gr]])
    inner_inv = np.block([[gr, gi], [-gi, gr]])
    prod = (jnp.arange(n1, dtype=jnp.int32)[:, None] * jnp.arange(DFT_INNER, dtype=jnp.int32)[None, :]) % big
    angt = prod.astype(F32) * (-2.0 * math.pi / big)
    tw = jnp.stack([jnp.cos(angt), jnp.sin(angt)])
    tw = jnp.broadcast_to(tw[..., None], tw.shape + (LANES,))
    f = lambda m: jnp.asarray(np.ascontiguousarray(m), F32).astype(BF16)
    return dict(first=f(first), first_real=f(first_real), last=f(last), inner=f(inner), inner_inv=f(inner_inv),
                tw=tw, n1=n1, inv_scale=1.0 / big)


def _left_matmul_kernel(*refs, mode, inv_scale):
    if mode == "first":
        m_ref, x_ref, o_ref = refs
        o_ref[...] = _bdot(m_ref[...], x_ref[0]).astype(BF16)
        return
    if mode == "last":
        m3_ref, b_ref, g_ref, v_ref, bias_ref, z_ref = refs
    else:
        m3_ref, b_ref, g_ref, v_ref, bias_ref, m1_ref, z_ref, a_ref = refs
    y = _bdot(m3_ref[...], b_ref[...]) * inv_scale
    z = g_ref[0] * (y + bias_ref[...] * v_ref[0])
    z_ref[0] = z
    if mode == "last_first":
        a_ref[...] = _bdot(m1_ref[...], z).astype(BF16)


DFT_COLS = 2048


def _dft_first(m1, x3, part):
    _, rows, wtot = x3.shape
    wt = DFT_COLS
    return pl.pallas_call(
        functools.partial(_left_matmul_kernel, mode="first", inv_scale=None),
        out_shape=jax.ShapeDtypeStruct((m1.shape[0], wtot), BF16),
        grid=(wtot // wt,),
        in_specs=[pl.BlockSpec(m1.shape, lambda j: (0, 0)), pl.BlockSpec((1, rows, wt), lambda j: (part, 0, j))],
        out_specs=pl.BlockSpec((m1.shape[0], wt), lambda j: (0, j)),
        compiler_params=_cparams(("parallel",)),
        name="dft_outer_forward",
    )(m1, x3)


def _dft_last(m3, b2, gate3, gate_part, v3, v_part, bias_flat, inv_scale, m1=None):
    _, rows, wtot = gate3.shape
    wt = DFT_COLS
    col = lambda r: pl.BlockSpec((r, wt), lambda j: (0, j))
    part = lambda p: pl.BlockSpec((1, rows, wt), lambda j: (p, 0, j))
    const = lambda a: pl.BlockSpec(a.shape, lambda j: (0, 0))
    z_shape = jax.ShapeDtypeStruct((1, rows, wtot), F32)
    in_specs = [const(m3), col(b2.shape[0]), part(gate_part), part(v_part), pl.BlockSpec((1, wt), lambda j: (0, 0))]
    if m1 is None:
        return pl.pallas_call(
            functools.partial(_left_matmul_kernel, mode="last", inv_scale=inv_scale),
            out_shape=z_shape,
            grid=(wtot // wt,),
            in_specs=in_specs,
            out_specs=part(0),
            compiler_params=_cparams(("parallel",)),
            name="dft_outer_inverse",
        )(m3, b2, gate3, v3, bias_flat)
    return pl.pallas_call(
        functools.partial(_left_matmul_kernel, mode="last_first", inv_scale=inv_scale),
        out_shape=(z_shape, jax.ShapeDtypeStruct((m1.shape[0], wtot), BF16)),
        grid=(wtot // wt,),
        in_specs=in_specs + [const(m1)],
        out_specs=(part(0), col(m1.shape[0])),
        compiler_params=_cparams(("parallel",)),
        name="dft_outer_inverse_forward",
    )(m3, b2, gate3, v3, bias_flat, m1)


def _dft_mid_kernel(a_ref, tw_ref, g_ref, *rest, slabs, with_filter):
    if with_filter:
        h_ref, gi_ref, o_ref = rest
    else:
        (o_ref,) = rest
    nl = a_ref.shape[3] // LANES
    for j in range(slabs):
        twr = jnp.concatenate([tw_ref[0, j]] * nl, axis=1)
        twi = jnp.concatenate([tw_ref[1, j]] * nl, axis=1)
        ar, ai = a_ref[0, j].astype(F32), a_ref[1, j].astype(F32)
        br = ar * twr - ai * twi
        bi = ar * twi + ai * twr
        x = _bdot(g_ref[...], jnp.concatenate([br.astype(BF16), bi.astype(BF16)], axis=0))
        if not with_filter:
            o_ref[0, j] = x[:DFT_INNER]
            o_ref[1, j] = x[DFT_INNER:]
            continue
        xr, xi = x[:DFT_INNER], x[DFT_INNER:]
        hr, hi = h_ref[0, j], h_ref[1, j]
        yr = xr * hr - xi * hi
        yi = xr * hi + xi * hr
        zz = _bdot(gi_ref[...], jnp.concatenate([yr.astype(BF16), yi.astype(BF16)], axis=0))
        zr, zi = zz[:DFT_INNER], zz[DFT_INNER:]
        o_ref[0, j] = (zr * twr + zi * twi).astype(BF16)
        o_ref[1, j] = (zi * twr - zr * twi).astype(BF16)


def _dft_mid(a4, tw, inner, spec=None, inner_inv=None, order=0):
    _, n1, _, c = a4.shape
    slabs = 8 if n1 % 8 == 0 else n1
    blk = lambda w: pl.BlockSpec((2, slabs, DFT_INNER, w), lambda j: (0, j, 0, 0))
    const = lambda a: pl.BlockSpec(a.shape, lambda j: (0, 0))
    in_specs = [blk(c), blk(LANES), const(inner)]
    args = [a4, tw, inner]
    if spec is not None:
        in_specs += [pl.BlockSpec((2, slabs, DFT_INNER, c), lambda j: (0, j, 0, order)), const(inner_inv)]
        args += [spec, inner_inv]
    return pl.pallas_call(
        functools.partial(_dft_mid_kernel, slabs=slabs, with_filter=spec is not None),
        out_shape=jax.ShapeDtypeStruct(a4.shape, BF16 if spec is not None else F32),
        grid=(n1 // slabs,),
        in_specs=in_specs,
        out_specs=blk(c),
        compiler_params=_cparams(("parallel",)),
        name="dft_inner_filter" if spec is not None else "dft_inner_forward",
    )(*args)


def _hyena_latent(vx, filt_a, bias, consts):
    _, nb, n, c = vx.shape
    assert nb == 2, "the two samples are carried as the real and imaginary part of one transform"
    n1 = consts["n1"]
    wflat = DFT_INNER * c
    flat = vx.reshape(3, nb * n1 // 2, wflat)
    bias_flat = jnp.tile(bias, (1, DFT_COLS // c))
    tw, inner, inner_inv, scale = consts["tw"], consts["inner"], consts["inner_inv"], consts["inv_scale"]

    spec = _dft_mid(filt_a.reshape(2, n1, DFT_INNER, HYENA_ORDER * c), tw, inner)

    a = _dft_first(consts["first"], flat, 0).reshape(2, n1, DFT_INNER, c)
    bm = _dft_mid(a, tw, inner, spec, inner_inv, order=0)
    z, a = _dft_last(consts["last"], bm.reshape(2 * n1, wflat), flat, 1, flat, 0, bias_flat[0:1], scale,
                     consts["first"])
    bm = _dft_mid(a.reshape(2, n1, DFT_INNER, c), tw, inner, spec, inner_inv, order=1)
    y = _dft_last(consts["last"], bm.reshape(2 * n1, wflat), flat, 2, z, 0, bias_flat[1:2], scale)
    return y.reshape(nb, n, c)


def _hyena_ctx_kernel(u_ref, h_ref, fs_ref, ff_ref, fi_ref, bias_ref, o_ref, *, n):
    big = 2 * n
    c = HYENA_WIDTH
    spec = _hdot(fs_ref[...], h_ref[...])
    u = u_ref[...]
    v, x1, x2 = u[:, 0:c], u[:, c:2 * c], u[:, 2 * c:3 * c]

    def conv(zin, order):
        x = _hdot(ff_ref[...], zin)
        xr, xi = x[:big], x[big:]
        hr, hi = spec[:big, order * c:(order + 1) * c], spec[big:, order * c:(order + 1) * c]
        y = jnp.concatenate([xr * hr - xi * hi, xr * hi + xi * hr], axis=0)
        return _hdot(fi_ref[...], y)

    bias = bias_ref[...]
    z = x1 * (conv(v, 0) + bias[0:1] * v)
    o_ref[...] = x2 * (conv(z, 1) + bias[1:2] * z)


def _ctx_dft_consts(n):
    big = 2 * n
    k = np.arange(big)[:, None]
    t = np.arange(big)[None, :]
    ang = -2.0 * np.pi * ((k * t) % big) / big
    fr, fi = np.cos(ang), np.sin(ang)
    spec_m = np.concatenate([fr, fi], axis=0)
    fwd = np.block([[fr[:, :n], -fi[:, :n]], [fi[:, :n], fr[:, :n]]])
    inv = np.block([[fr.T[:n, :], fi.T[:n, :]], [-fi.T[:n, :], fr.T[:n, :]]]) / big
    f = lambda m: jnp.asarray(np.ascontiguousarray(m), F32)
    return f(spec_m), f(fwd), f(inv)


def _hyena_ctx(u_ctx, h_circ, bias, cconsts):
    nb, n, w = u_ctx.shape
    assert nb == 2
    fs, ff, fi = cconsts
    full = lambda a: pl.BlockSpec(a.shape, lambda i: (0,) * a.ndim)
    u2 = u_ctx.reshape(nb * n, w)
    out = pl.pallas_call(
        functools.partial(_hyena_ctx_kernel, n=n),
        out_shape=jax.ShapeDtypeStruct((nb * n, HYENA_WIDTH), F32),
        grid=(1,),
        in_specs=[full(u2), full(h_circ), full(fs), full(ff), full(fi), full(bias)],
        out_specs=pl.BlockSpec((nb * n, HYENA_WIDTH), lambda i: (0, 0)),
        compiler_params=_cparams(("arbitrary",)),
        name="hyena_context",
    )(u2, h_circ, fs, ff, fi, bias)
    return out.reshape(nb, n, HYENA_WIDTH)


def _gla_kernel(x_ref, o_ref, st_ref, *, backward):
    s = pl.program_id(1)

    @pl.when(s == 0)
    def _():
        st_ref[...] = jnp.zeros(st_ref.shape, F32)

    c = GLA_CHUNK
    kw, vw = GLA_K_WIDTH, GLA_V_WIDTH
    ri = lax.broadcasted_iota(jnp.int32, (c, c), 0)
    ci = lax.broadcasted_iota(jnp.int32, (c, c), 1)
    tri = ((ri <= ci) if backward else (ri >= ci)).astype(F32)
    kr = lax.broadcasted_iota(jnp.int32, (GLA_HEADS * c, kw), 0) // c
    kc = lax.broadcasted_iota(jnp.int32, (GLA_HEADS * c, kw), 1) // GLA_DK
    mask_k = kr == kc
    vr = lax.broadcasted_iota(jnp.int32, (GLA_HEADS * c, vw), 0) // c
    vc = lax.broadcasted_iota(jnp.int32, (GLA_HEADS * c, vw), 1) // GLA_DV
    mask_v = vr == vc
    sr = lax.broadcasted_iota(jnp.int32, (vw, kw), 0) // GLA_DV
    sc = lax.broadcasted_iota(jnp.int32, (vw, kw), 1) // GLA_DK
    mask_s = sr == sc
    qi = lax.broadcasted_iota(jnp.int32, (c, GLA_HEADS * c), 0)
    kj = lax.broadcasted_iota(jnp.int32, (c, GLA_HEADS * c), 1) % c
    causal = (qi <= kj) if backward else (qi >= kj)

    n_chunks = x_ref.shape[1] // c
    order = range(n_chunks - 1, -1, -1) if backward else range(n_chunks)
    goff = 2 * kw + vw + (kw if backward else 0)
    for ch in order:
        r0 = ch * c
        q = x_ref[0, r0:r0 + c, 0:kw]
        k = x_ref[0, r0:r0 + c, kw:2 * kw]
        v = x_ref[0, r0:r0 + c, 2 * kw:2 * kw + vw]
        g = x_ref[0, r0:r0 + c, goff:goff + kw]
        gc = _hdot(tri, g)
        g_last = gc[0:1] if backward else gc[c - 1:c]
        q_dec = q * jnp.exp(gc)
        k_dec = k * jnp.exp(-gc)
        k_tail = k * jnp.exp(g_last - gc)
        k_blk = jnp.where(mask_k, jnp.concatenate([k_dec] * GLA_HEADS, axis=0), 0.0).astype(BF16)
        v_bf = v.astype(BF16)
        v_blk = jnp.where(mask_v, jnp.concatenate([v_bf] * GLA_HEADS, axis=0), jnp.zeros((), BF16))
        q_bf = q_dec.astype(BF16)
        scores = lax.dot_general(q_bf, k_blk, (((1,), (1,)), ((), ())), preferred_element_type=F32)
        scores = jnp.where(causal, scores, 0.0)
        st = st_ref[...]
        o = jnp.dot(scores.astype(BF16), v_blk, preferred_element_type=F32)
        o = o + lax.dot_general(q_bf, st.astype(BF16), (((1,), (1,)), ((), ())), preferred_element_type=F32)
        o_ref[0, r0:r0 + c, :] = o
        kv_t = lax.dot_general(v_bf, k_tail.astype(BF16), (((0,), (0,)), ((), ())), preferred_element_type=F32)
        st_ref[...] = st * jnp.exp(g_last) + jnp.where(mask_s, kv_t, 0.0)


def _gla(gla_in, *, ctx_len, backward):
    nb, t, w = gla_in.shape
    tb = TOK_TILE
    nblk = t // tb
    cblk = ctx_len // tb
    if backward:
        idx = lambda b, s: (b, jnp.where(s < cblk, cblk - 1 - s, nblk - 1 - (s - cblk)), 0)
    else:
        idx = lambda b, s: (b, s, 0)
    return pl.pallas_call(
        functools.partial(_gla_kernel, backward=backward),
        out_shape=jax.ShapeDtypeStruct((nb, t, GLA_V_WIDTH), F32),
        grid=(nb, nblk),
        in_specs=[pl.BlockSpec((1, tb, w), idx)],
        out_specs=pl.BlockSpec((1, tb, GLA_V_WIDTH), idx),
        scratch_shapes=[pltpu.VMEM((GLA_V_WIDTH, GLA_K_WIDTH), F32)],
        compiler_params=_cparams(("parallel", "arbitrary")),
        name="gla_backward" if backward else "gla_forward",
    )(gla_in)


def _out_kernel(h_ref, mod_ref, at_ref, hy_ref, gf_ref, gb_ref, gr_ref, gn_ref, gmat_ref, w_ref, o_ref, *,
                n_batch, ctx_tiles, tile_off, d_model):
    b, i = pl.program_id(0), pl.program_id(1)
    mod = _mod_row(mod_ref, i + tile_off < ctx_tiles, b, n_batch)
    gate = mod[:, 2 * d_model:3 * d_model]
    gmat = gmat_ref[...]
    gn = gn_ref[...]
    a0, a1 = ATTN_WIDTH, ATTN_WIDTH + HYENA_WIDTH
    ya = _head_rms(at_ref[0], gn[:, :a0], gmat)
    yh = _head_rms(hy_ref[0], gn[:, a0:a1], gmat)
    yg = _head_rms(gf_ref[0] + gb_ref[0], gn[:, a1:], gmat) * _silu(gr_ref[0])
    y = jnp.concatenate([ya, yh, yg], axis=1)
    o_ref[0] = h_ref[0] + gate * _bdot(y, w_ref[...])


def _out_projection(h, mod, attn, hy, gla_f, gla_b, gr, gn, gmat, w_out, *, ctx_len, skip_ctx):
    nb, t, d = h.shape
    tb = TOK_TILE
    off = ctx_len // tb if skip_ctx else 0
    kern = functools.partial(_out_kernel, n_batch=nb, ctx_tiles=ctx_len // tb, tile_off=off, d_model=d)
    const = lambda a: pl.BlockSpec(a.shape, lambda b, i: (0,) * a.ndim)
    tok = lambda w: pl.BlockSpec((1, tb, w), lambda b, i: (b, i + off, 0))
    return pl.pallas_call(
        kern,
        out_shape=jax.ShapeDtypeStruct((nb, t - off * tb, d), F32),
        grid=(nb, t // tb - off),
        in_specs=[tok(d), const(mod), tok(ATTN_WIDTH), tok(HYENA_WIDTH), tok(GLA_V_WIDTH), tok(GLA_V_WIDTH),
                  tok(GLA_V_WIDTH), const(gn), const(gmat), const(w_out)],
        out_specs=pl.BlockSpec((1, tb, d), lambda b, i: (b, i, 0)),
        compiler_params=_cparams(("parallel", "parallel")),
        name="out_projection",
    )(h, mod, attn, hy, gla_f, gla_b, gr, gn, gmat, w_out)


def _ffn_kernel(h_ref, mod_ref, g2_ref, w1_ref, w3_ref, w2_ref, fg_ref, o_ref, *, n_batch, ctx_tiles, tile_off,
                d_model, final):
    b, i = pl.program_id(0), pl.program_id(1)
    mod = _mod_row(mod_ref, i + tile_off < ctx_tiles, b, n_batch)
    shift, scale, gate = (mod[:, 3 * d_model:4 * d_model], mod[:, 4 * d_model:5 * d_model],
                          mod[:, 5 * d_model:6 * d_model])
    x = h_ref[0]
    xn = x * lax.rsqrt(jnp.mean(jnp.square(x), axis=-1, keepdims=True) + NORM_EPS) * g2_ref[...]
    u = (xn * (1.0 + scale) + shift).astype(BF16)
    a = jnp.dot(u, w1_ref[...], preferred_element_type=F32)
    c = jnp.dot(u, w3_ref[...], preferred_element_type=F32)
    hid = (_silu(a) * c).astype(BF16)
    y = x + gate * jnp.dot(hid, w2_ref[...], preferred_element_type=F32)
    if final:
        y = y * lax.rsqrt(jnp.mean(jnp.square(y), axis=-1, keepdims=True) + NORM_EPS) * fg_ref[...]
    o_ref[0] = y


def _ffn(h, mod, g2, w1, w3, w2, fg, *, ctx_len, lat_only, final):
    nb, t, d = h.shape
    tb = TOK_TILE
    off = ctx_len // tb if lat_only else 0
    kern = functools.partial(_ffn_kernel, n_batch=nb, ctx_tiles=ctx_len // tb, tile_off=off, d_model=d, final=final)
    const = lambda a: pl.BlockSpec(a.shape, lambda b, i: (0,) * a.ndim)
    tok = pl.BlockSpec((1, tb, d), lambda b, i: (b, i, 0))
    return pl.pallas_call(
        kern,
        out_shape=jax.ShapeDtypeStruct((nb, t, d), F32),
        grid=(nb, t // tb),
        in_specs=[tok, const(mod), const(g2), const(w1), const(w3), const(w2), const(fg)],
        out_specs=tok,
        compiler_params=_cparams(("parallel", "parallel")),
        name="swiglu_ffn",
    )(h, mod, g2, w1, w3, w2, fg)


def _rope_tables(n_lat, ctx_len):
    rows = n_lat // GRID_W
    row = jnp.broadcast_to(jnp.arange(rows, dtype=F32)[:, None], (rows, GRID_W)).reshape(-1)
    col = jnp.broadcast_to(jnp.arange(GRID_W, dtype=F32)[None, :], (rows, GRID_W)).reshape(-1)
    n_freq = HEAD_DIM // 4
    inv_freq = jnp.power(ROPE_THETA, -jnp.arange(n_freq, dtype=F32) / n_freq)
    ang = jnp.concatenate([row[:, None] * inv_freq, col[:, None] * inv_freq], axis=-1)
    cos, sin = jnp.cos(ang), jnp.sin(ang)
    cos = jnp.concatenate([jnp.ones((ctx_len, HEAD_DIM // 2), F32), cos], axis=0)
    sin = jnp.concatenate([jnp.zeros((ctx_len, HEAD_DIM // 2), F32), sin], axis=0)
    cosf = jnp.tile(cos, (1, 2 * LANES // HEAD_DIM))
    sinf = jnp.tile(jnp.concatenate([-sin, sin], axis=1), (1, LANES // HEAD_DIM))
    return cosf, sinf


def kernel(x, c, ctx, c_ctx, ada_w, ada_b, norm1_g, w_in, q_norm_g, k_norm_g, hy_conv_w, hy_conv_b, filt_w1, filt_b1, filt_w2, filt_b2, filt_w3, filt_freq, hy_bias, gla_gate_w, gla_gate_b, out_norm_g, w_out, norm2_g, ffn_w1, ffn_w3, ffn_w2, final_norm_g):
    nb, n_lat, d = x.shape
    ctx_len = ctx.shape[1]
    depth = w_in.shape[0]
    assert nb == 2 and ctx_len % TOK_TILE == 0 and n_lat % TOK_TILE == 0 and n_lat % DFT_INNER == 0

    h = jnp.concatenate([ctx, x], axis=1)
    cond = jnp.concatenate([c, c_ctx[None, :], jnp.zeros((8 - nb - 1, d), F32)], axis=0)
    mods = _modulation(cond, ada_w, ada_b)

    cosf, sinf = _rope_tables(n_lat, ctx_len)
    lane = np.arange(LANES)
    gmat = jnp.asarray((lane[:, None] // HEAD_DIM) == (lane[None, :] // HEAD_DIM), BF16)
    consts = _dft_consts(n_lat)
    cconsts = _ctx_dft_consts(ctx_len)
    zc_ctx = _filter_features(ctx_len)
    deltas = jnp.abs(jnp.linspace(HYENA_MIN_DECAY, HYENA_MAX_DECAY, HYENA_WIDTH, dtype=F32))
    deltas = jnp.tile(deltas[None, :], (1, HYENA_ORDER))
    n1 = consts["n1"]
    rows_sa = (DFT_INNER * jnp.arange(n1, dtype=jnp.int32)[None, :] + jnp.arange(DFT_INNER, dtype=jnp.int32)[:, None])
    zt_lat = _circular_features(rows_sa, n_lat).transpose(1, 0, 2)
    t_hi, t_mid, t_lo = _split3(zt_lat[:, 0, :])
    d_hi, d_mid, d_lo = _split3(deltas[0])
    zt_lat = jnp.concatenate([zt_lat] + [p[:, None, :] for p in (t_hi, t_hi, t_hi, t_mid, t_mid, t_lo)], axis=1)
    n_feat = zt_lat.shape[1]
    zt_lat = jnp.pad(zt_lat, ((0, 0), (0, FEAT_PAD - n_feat), (0, 0)))
    decay_rows = jnp.zeros((FEAT_PAD, deltas.shape[1]), F32).at[n_feat - 6:n_feat].set(
        jnp.stack([d_hi, d_mid, d_lo, d_hi, d_mid, d_hi])).astype(BF16)
    lanes_n1 = lambda vec: jnp.broadcast_to(vec[:, None], (vec.shape[0], n1))

    w_in_b = jnp.pad(w_in, ((0, 0), (0, 0), (0, IN_PAD - w_in.shape[2]))).astype(BF16)
    w_out_b = w_out.astype(BF16)
    w1_b, w3_b, w2_b = ffn_w1.astype(BF16), ffn_w3.astype(BF16), ffn_w2.astype(BF16)
    fg = final_norm_g.reshape(1, d)

    out = None
    for l in range(depth):
        last = l == depth - 1
        mod = mods[l]
        wg = jnp.zeros((LANES, 2 * GLA_K_WIDTH), F32)
        wg = wg.at[0:GLA_GATE_RANK, 0:GLA_K_WIDTH].set(gla_gate_w[l, 0])
        wg = wg.at[GLA_GATE_RANK:2 * GLA_GATE_RANK, GLA_K_WIDTH:].set(gla_gate_w[l, 1])
        bg = gla_gate_b[l].reshape(1, 2 * GLA_K_WIDTH)
        qg = jnp.tile(q_norm_g[l][None, :], (1, ATTN_HEADS))
        kg = jnp.tile(k_norm_g[l][None, :], (1, ATTN_KV_HEADS))

        q, k, v, hy, gla_in, gr = _in_projection(h, mod, norm1_g[l].reshape(1, d), w_in_b[l], qg, kg, cosf, sinf,
                                                 gmat, wg.astype(BF16), bg, ctx_len=ctx_len)
        attn = _attention(q, k, v, ctx_len=ctx_len)

        vx, u_ctx = _short_conv(hy, hy_conv_w[l], hy_conv_b[l], ctx_len=ctx_len)
        w1p = jnp.pad(filt_w1[l], ((0, LANES - filt_w1.shape[1]), (0, 0)))
        w3r = filt_w3[l].reshape(FILTER_HIDDEN, HYENA_ORDER, 2, HYENA_WIDTH).transpose(2, 0, 1, 3)
        w3r = w3r.reshape(2, FILTER_HIDDEN, HYENA_ORDER * HYENA_WIDTH)
        fargs = (w1p, filt_b1[l][None, :], filt_w2[l], filt_b2[l][None, :], w3r, filt_freq[l][None, :], deltas)
        w1t = jnp.pad(filt_w1[l].T, ((0, 0), (0, FEAT_PAD - filt_w1.shape[1])))
        w3_hi, w3_lo, _ = _split3(w3r)
        w3_stack = jnp.concatenate([w3_hi, w3_lo, w3_hi], axis=1).reshape(-1, w3r.shape[2]).astype(BF16)
        filt_a = _filter_outer_dft(zt_lat, w1t, lanes_n1(filt_b1[l]), filt_w2[l].T, lanes_n1(filt_b2[l]), w3_stack,
                                   lanes_n1(filt_freq[l]), decay_rows, consts["first_real"])
        hy_lat = _hyena_latent(vx, filt_a, hy_bias[l], consts)
        if not last:
            h_ctx = _hyena_filter(zc_ctx, *fargs, n=ctx_len)
            hy_ctx = _hyena_ctx(u_ctx, h_ctx, hy_bias[l], cconsts)
        else:
            hy_ctx = jnp.zeros((nb, ctx_len, HYENA_WIDTH), F32)
        hy_out = jnp.concatenate([hy_ctx, hy_lat], axis=1)

        gla_f = _gla(gla_in, ctx_len=ctx_len, backward=False)
        gla_b = _gla(gla_in, ctx_len=ctx_len, backward=True)

        h = _out_projection(h, mod, attn, hy_out, gla_f, gla_b, gr, out_norm_g[l].reshape(1, -1), gmat, w_out_b[l],
                            ctx_len=ctx_len, skip_ctx=last)
        h = _ffn(h, mod, norm2_g[l].reshape(1, d), w1_b[l], w3_b[l], w2_b[l], fg, ctx_len=ctx_len, lat_only=last,
                 final=last)
    return h
```

```python
import functools
import math

import numpy as np
import jax
import jax.numpy as jnp
from jax import lax
from jax.experimental import pallas as pl
from jax.experimental.pallas import tpu as pltpu

F32 = jnp.float32
BF16 = jnp.bfloat16
HI = lax.Precision.HIGHEST

LANES = 128
VMEM_LIMIT = 56 * 1024 * 1024

HEAD_DIM = 64
ATTN_HEADS = 8
ATTN_KV_HEADS = 2
ATTN_WIDTH = ATTN_HEADS * HEAD_DIM
KV_WIDTH = ATTN_KV_HEADS * HEAD_DIM
HYENA_WIDTH = 256
HYENA_ORDER = 2
FILTER_BANDS = 16
FILTER_HIDDEN = 64
GLA_HEADS = 4
GLA_DK = 32
GLA_DV = 64
GLA_K_WIDTH = GLA_HEADS * GLA_DK
GLA_V_WIDTH = GLA_HEADS * GLA_DV
GLA_GATE_RANK = 16
GLA_GATE_TAU = 16.0
GLA_CHUNK = 64
GRID_W = 64
ROPE_THETA = 10000.0
N_MOD = 6
NORM_EPS = 1e-6
HYENA_TARGET = 1e-2
HYENA_MIN_DECAY = math.log(HYENA_TARGET) / 1.5
HYENA_MAX_DECAY = math.log(HYENA_TARGET) / 0.3

TOK_TILE = 256
DFT_INNER = 128
IN_PAD = 2432

_OQ, _OK, _OV, _OHY = 0, 512, 640, 768
_OGQ, _OGK, _OGV, _OGR, _OGA = 1536, 1664, 1792, 2048, 2304


def _cparams(sem, vmem=VMEM_LIMIT):
    return pltpu.CompilerParams(dimension_semantics=sem, vmem_limit_bytes=vmem)


def _bdot(a, b):
    return jnp.dot(a.astype(BF16), b.astype(BF16), preferred_element_type=F32)


def _hdot(a, b):
    return jnp.dot(a, b, preferred_element_type=F32, precision=HI)


def _group_mean_square(x, gmat):
    outs = []
    for j in range(x.shape[1] // LANES):
        sq = jnp.square(x[:, j * LANES:(j + 1) * LANES])
        hi = sq.astype(BF16)
        lo = (sq - hi.astype(F32)).astype(BF16)
        s = jnp.dot(jnp.concatenate([hi, lo], axis=1), gmat, preferred_element_type=F32)
        outs.append(s * (1.0 / HEAD_DIM))
    return outs[0] if len(outs) == 1 else jnp.concatenate(outs, axis=1)


def _head_rms(x, gain, gmat):
    ms = _group_mean_square(x, gmat)
    return x * lax.rsqrt(ms + NORM_EPS) * gain


def _mod_row(mod_ref, is_ctx, b, n_batch):
    row = jnp.where(is_ctx, n_batch, b)
    return mod_ref[pl.ds(row, 1), :]


def _silu(x):
    return x * (1.0 / (1.0 + jnp.exp(-x)))


def _split3(x):
    hi = x.astype(BF16).astype(F32)
    mid = (x - hi).astype(BF16).astype(F32)
    lo = (x - hi - mid).astype(BF16).astype(F32)
    return hi, mid, lo


def _mod_kernel(c_ref, w_ref, b_ref, o_ref):
    s = _silu(c_ref[...])
    o_ref[0] = _bdot(s, w_ref[0]) + b_ref[0]


def _modulation(cond_rows, ada_w, ada_b):
    depth, d, w = ada_w.shape
    ct = 1536
    return pl.pallas_call(
        _mod_kernel,
        out_shape=jax.ShapeDtypeStruct((depth, 8, w), F32),
        grid=(depth, w // ct),
        in_specs=[pl.BlockSpec((8, d), lambda l, j: (0, 0)),
                  pl.BlockSpec((1, d, ct), lambda l, j: (l, 0, j)),
                  pl.BlockSpec((1, 1, ct), lambda l, j: (l, 0, j))],
        out_specs=pl.BlockSpec((1, 8, ct), lambda l, j: (l, 0, j)),
        compiler_params=_cparams(("arbitrary", "arbitrary")),
        name="adaln_modulation",
    )(cond_rows, ada_w, ada_b.reshape(depth, 1, w))


def _in_kernel(h_ref, mod_ref, g1_ref, w_ref, qg_ref, kg_ref, cos_ref, sin_ref, gmat_ref, wg_ref, bg_ref,
               q_ref, k_ref, v_ref, hy_ref, gla_ref, gr_ref, *, n_batch, ctx_tiles, d_model):
    b, i = pl.program_id(0), pl.program_id(1)
    mod = _mod_row(mod_ref, i < ctx_tiles, b, n_batch)
    shift, scale = mod[:, 0:d_model], mod[:, d_model:2 * d_model]
    x = h_ref[0]
    xn = x * lax.rsqrt(jnp.mean(jnp.square(x), axis=-1, keepdims=True) + NORM_EPS) * g1_ref[...]
    u = xn * (1.0 + scale) + shift
    proj = _bdot(u, w_ref[...])

    gmat = gmat_ref[...]
    cosf, sinf = cos_ref[...], sin_ref[...]
    lane = lax.broadcasted_iota(jnp.int32, (x.shape[0], LANES), 1)
    first_half = (lane % HEAD_DIM) < (HEAD_DIM // 2)
    low_head = lane < HEAD_DIM

    def rope(t):
        partner = jnp.where(first_half, pltpu.roll(t, LANES - HEAD_DIM // 2, 1), pltpu.roll(t, HEAD_DIM // 2, 1))
        return t * cosf + partner * sinf

    qn = _head_rms(proj[:, _OQ:_OQ + ATTN_WIDTH], qg_ref[...], gmat)
    zero = jnp.zeros((x.shape[0], LANES), F32)
    for p in range(ATTN_HEADS // 2):
        t = rope(qn[:, p * LANES:(p + 1) * LANES]) * (HEAD_DIM ** -0.5 * math.log2(math.e))
        tr = pltpu.roll(t, HEAD_DIM, 1)
        if 2 * p < ATTN_HEADS // ATTN_KV_HEADS:
            q_ref[0, 2 * p] = jnp.where(low_head, t, zero).astype(BF16)
            q_ref[0, 2 * p + 1] = jnp.where(low_head, tr, zero).astype(BF16)
        else:
            q_ref[0, 2 * p] = jnp.where(low_head, zero, tr).astype(BF16)
            q_ref[0, 2 * p + 1] = jnp.where(low_head, zero, t).astype(BF16)
    kn = _head_rms(proj[:, _OK:_OK + KV_WIDTH], kg_ref[...], gmat)
    k_ref[0] = rope(kn).astype(BF16)
    v_ref[0] = proj[:, _OV:_OV + KV_WIDTH].astype(BF16)
    hy_ref[0] = proj[:, _OHY:_OHY + 3 * HYENA_WIDTH]

    zg = _bdot(proj[:, _OGA:_OGA + LANES], wg_ref[...]) + bg_ref[...]
    log_gate = (jnp.minimum(zg, 0.0) - jnp.log(1.0 + jnp.exp(-jnp.abs(zg)))) * (1.0 / GLA_GATE_TAU)
    gla_ref[0, :, 0:GLA_K_WIDTH] = proj[:, _OGQ:_OGQ + GLA_K_WIDTH] * (GLA_DK ** -0.5)
    gla_ref[0, :, GLA_K_WIDTH:2 * GLA_K_WIDTH + GLA_V_WIDTH] = proj[:, _OGK:_OGK + GLA_K_WIDTH + GLA_V_WIDTH]
    gla_ref[0, :, 2 * GLA_K_WIDTH + GLA_V_WIDTH:] = log_gate
    gr_ref[0] = proj[:, _OGR:_OGR + GLA_V_WIDTH]


def _in_projection(h, mod, g1, w_in, qg, kg, cosf, sinf, gmat, wg, bg, *, ctx_len):
    nb, t, d = h.shape
    tb = TOK_TILE
    kern = functools.partial(_in_kernel, n_batch=nb, ctx_tiles=ctx_len // tb, d_model=d)
    const = lambda shape: pl.BlockSpec(shape, lambda b, i: (0,) * len(shape))
    tok = lambda w: pl.BlockSpec((1, tb, w), lambda b, i: (b, i, 0))
    gla_w = 2 * GLA_K_WIDTH + GLA_V_WIDTH + 2 * GLA_K_WIDTH
    return pl.pallas_call(
        kern,
        out_shape=(jax.ShapeDtypeStruct((nb, ATTN_HEADS, t, LANES), BF16),
                   jax.ShapeDtypeStruct((nb, t, KV_WIDTH), BF16),
                   jax.ShapeDtypeStruct((nb, t, KV_WIDTH), BF16),
                   jax.ShapeDtypeStruct((nb, t, 3 * HYENA_WIDTH), F32),
                   jax.ShapeDtypeStruct((nb, t, gla_w), F32),
                   jax.ShapeDtypeStruct((nb, t, GLA_V_WIDTH), F32)),
        grid=(nb, t // tb),
        in_specs=[tok(d), const(mod.shape), const(g1.shape), const(w_in.shape), const(qg.shape), const(kg.shape),
                  pl.BlockSpec((tb, LANES), lambda b, i: (i, 0)), pl.BlockSpec((tb, LANES), lambda b, i: (i, 0)),
                  const(gmat.shape), const(wg.shape), const(bg.shape)],
        out_specs=(pl.BlockSpec((1, ATTN_HEADS, tb, LANES), lambda b, i: (b, 0, i, 0)),
                   tok(KV_WIDTH), tok(KV_WIDTH), tok(3 * HYENA_WIDTH), tok(gla_w), tok(GLA_V_WIDTH)),
        compiler_params=_cparams(("parallel", "parallel")),
        name="in_projection",
    )(h, mod, g1, w_in, qg, kg, cosf, sinf, gmat, wg, bg)


KV_UNROLL = 2


def _attn_kernel(q_ref, k_ref, v_ref, o_ref, m_sc, acc_sc, s_sc, mx_sc, p_sc, *, ctx_len, ctx_tiles, kv_tile,
                 n_kv_tiles):
    i = pl.program_id(1)
    tq = q_ref.shape[2]
    per_kv = ATTN_HEADS // ATTN_KV_HEADS
    grp = per_kv * tq
    m_sc[...] = jnp.full(m_sc.shape, -jnp.inf, F32)
    acc_sc[...] = jnp.zeros(acc_sc.shape, F32)

    def scores(start, size, buf):
        q = q_ref[0].reshape(ATTN_HEADS * tq, LANES)
        kc = k_ref[0, pl.ds(start, size), :]
        s = lax.dot_general(q, kc, (((1,), (1,)), ((), ())), preferred_element_type=F32)
        s_sc[buf, :, 0:size] = s
        mx_sc[buf] = jnp.broadcast_to(jnp.max(s, axis=1, keepdims=True), mx_sc.shape[1:])

    def consume(start, size, buf):
        vc = v_ref[0, pl.ds(start, size), :]
        low = lax.broadcasted_iota(jnp.int32, vc.shape, 1) < HEAD_DIM
        one = jnp.ones((), BF16)
        v_ones = (jnp.where(low, vc, one), jnp.where(low, one, vc))
        for g in range(ATTN_KV_HEADS):
            alphas = []
            for hh in range(g * per_kv, (g + 1) * per_kv):
                rows = slice(hh * tq, (hh + 1) * tq)
                m_prev = m_sc[rows, :]
                m_new = jnp.maximum(m_prev, mx_sc[buf, rows, :])
                alphas.append(jnp.exp2(m_prev - m_new))
                for j in range(size // LANES):
                    cols = slice(j * LANES, (j + 1) * LANES)
                    p_sc[buf, rows, cols] = jnp.exp2(s_sc[buf, rows, cols] - m_new).astype(BF16)
                m_sc[rows, :] = m_new
            rows_g = slice(g * grp, (g + 1) * grp)
            pv = jnp.dot(p_sc[buf, rows_g, 0:size], v_ones[g], preferred_element_type=F32)
            acc_sc[rows_g, :] = jnp.concatenate(alphas, axis=0) * acc_sc[rows_g, :] + pv

    @pl.when(i < ctx_tiles)
    def _():
        scores(0, ctx_len, 0)
        consume(0, ctx_len, 0)

    @pl.when(i >= ctx_tiles)
    def _():
        def tile_start(j):
            return pl.multiple_of(j * kv_tile, kv_tile)

        def run(j0, count, tail):
            for u in range(count):
                if not (tail and u + 1 == count):
                    scores(tile_start(j0 + u + 1), kv_tile, (u + 1) % 2)
                consume(tile_start(j0 + u), kv_tile, u % 2)

        trips = (n_kv_tiles - 1) // KV_UNROLL
        scores(0, kv_tile, 0)
        lax.fori_loop(0, trips, lambda k, c: (run(k * KV_UNROLL, KV_UNROLL, False), c)[1], 0)
        run(trips * KV_UNROLL, n_kv_tiles - trips * KV_UNROLL, True)

    low_head = lax.broadcasted_iota(jnp.int32, (tq, LANES), 1) < HEAD_DIM
    for p in range(ATTN_HEADS // 2):
        outs = []
        for hh in (2 * p, 2 * p + 1):
            acc = acc_sc[hh * tq:(hh + 1) * tq, :]
            o = acc / pltpu.roll(acc, HEAD_DIM, 1)
            src_low = hh < per_kv
            dst_low = hh % 2 == 0
            outs.append(o if src_low == dst_low else pltpu.roll(o, HEAD_DIM, 1))
        o_ref[0, :, p * LANES:(p + 1) * LANES] = jnp.where(low_head, outs[0], outs[1])


def _attention(q, k, v, *, ctx_len):
    nb, nh, t, _ = q.shape
    tq = TOK_TILE
    n_lat = t - ctx_len
    kv_tile = max(w for w in (256, 512, 768, 1024) if t % w == 0)
    assert ctx_len <= kv_tile
    rows = nh * tq
    kern = functools.partial(_attn_kernel, ctx_len=ctx_len, ctx_tiles=ctx_len // tq, kv_tile=kv_tile,
                             n_kv_tiles=t // kv_tile)
    return pl.pallas_call(
        kern,
        out_shape=jax.ShapeDtypeStruct((nb, t, ATTN_WIDTH), F32),
        grid=(nb, t // tq),
        in_specs=[pl.BlockSpec((1, nh, tq, LANES), lambda b, i: (b, 0, i, 0)),
                  pl.BlockSpec((1, t, KV_WIDTH), lambda b, i: (b, 0, 0)),
                  pl.BlockSpec((1, t, KV_WIDTH), lambda b, i: (b, 0, 0))],
        out_specs=pl.BlockSpec((1, tq, ATTN_WIDTH), lambda b, i: (b, i, 0)),
        scratch_shapes=[pltpu.VMEM((rows, LANES), F32), pltpu.VMEM((rows, LANES), F32),
                        pltpu.VMEM((2, rows, kv_tile), F32), pltpu.VMEM((2, rows, LANES), F32),
                        pltpu.VMEM((2, rows, kv_tile), BF16)],
        compiler_params=_cparams(("parallel", "arbitrary")),
        name="gqa_attention",
    )(q, k, v)


def _short_conv_kernel(x_ref, w_ref, b_ref, lat_ref, ctx_ref, *, ctx_len):
    x = x_ref[0]
    t = x.shape[0]
    row = lax.broadcasted_iota(jnp.int32, x.shape, 0)
    prev = jnp.where((row == 0) | (row == ctx_len), 0.0, pltpu.roll(x, 1, 0))
    nxt = jnp.where((row == ctx_len - 1) | (row == t - 1), 0.0, pltpu.roll(x, t - 1, 0))
    w = w_ref[...]
    u = prev * w[0:1, :] + x * w[1:2, :] + nxt * w[2:3, :] + b_ref[...]
    ctx_ref[0] = u[:ctx_len]
    lat_ref[0, 0] = u[ctx_len:]


def _short_conv(hy, conv_w, conv_b, *, ctx_len):
    nb, t, w = hy.shape
    per = HYENA_WIDTH // LANES
    return pl.pallas_call(
        functools.partial(_short_conv_kernel, ctx_len=ctx_len),
        out_shape=(jax.ShapeDtypeStruct((3, nb, t - ctx_len, HYENA_WIDTH), F32),
                   jax.ShapeDtypeStruct((nb, ctx_len, w), F32)),
        grid=(nb, w // LANES),
        in_specs=[pl.BlockSpec((1, t, LANES), lambda b, j: (b, 0, j)),
                  pl.BlockSpec((3, LANES), lambda b, j: (0, j)),
                  pl.BlockSpec((1, LANES), lambda b, j: (0, j))],
        out_specs=(pl.BlockSpec((1, 1, t - ctx_len, LANES), lambda b, j: (j // per, b, 0, j % per)),
                   pl.BlockSpec((1, ctx_len, LANES), lambda b, j: (b, 0, j))),
        compiler_params=_cparams(("parallel", "parallel")),
        name="hyena_short_conv",
    )(hy, conv_w, conv_b.reshape(1, w))


def _filter_kernel(z_ref, w1_ref, b1_ref, w2_ref, b2_ref, w3_ref, fr_ref, dl_ref, o_ref, *, n):
    r = pl.program_id(0)
    z = z_ref[...]
    fr = fr_ref[...]
    a = jnp.sin(fr * (_hdot(z, w1_ref[...]) + b1_ref[...]))
    a = jnp.sin(fr * (_hdot(a, w2_ref[...]) + b2_ref[...]))
    h = _hdot(a, w3_ref[0])
    window = jnp.exp(-z[:, 0:1] * dl_ref[...])
    row = r * z.shape[0] + lax.broadcasted_iota(jnp.int32, h.shape, 0)
    o_ref[...] = jnp.where(row == n, 0.0, h * window)


def _circular_features(r, n):
    p = jnp.where(r < n, r, 2 * n - r)
    p = jnp.where(r == n, 0, p).astype(F32)
    t = p / (n - 1)
    omega = (2.0 * math.pi / n) * p
    bands = jnp.linspace(1e-4, FILTER_BANDS - 1, FILTER_BANDS, dtype=F32).reshape((-1,) + (1,) * r.ndim)
    phase = omega[None] * bands
    return jnp.concatenate([t[None], jnp.cos(phase), -jnp.sin(phase)], axis=0)


def _filter_features(n):
    z = _circular_features(jnp.arange(2 * n, dtype=jnp.int32), n).T
    return jnp.pad(z, ((0, 0), (0, LANES - z.shape[1])))


def _hyena_filter(zc, w1p, b1, w2, b2, w3r, freq, deltas, *, n):
    rt = min(512, n)
    width = HYENA_ORDER * HYENA_WIDTH
    const = lambda shape: pl.BlockSpec(shape, lambda r: (0,) * len(shape))
    return pl.pallas_call(
        functools.partial(_filter_kernel, n=n),
        out_shape=jax.ShapeDtypeStruct((2 * n, width), F32),
        grid=(2 * n // rt,),
        in_specs=[pl.BlockSpec((rt, LANES), lambda r: (r, 0)), const(w1p.shape), const(b1.shape), const(w2.shape),
                  const(b2.shape), pl.BlockSpec((1, FILTER_HIDDEN, width), lambda r: (r // (n // rt), 0, 0)),
                  const(freq.shape), const(deltas.shape)],
        out_specs=pl.BlockSpec((rt, width), lambda r: (r, 0)),
        compiler_params=_cparams(("parallel",)),
        name="hyena_filter",
    )(zc, w1p, b1, w2, b2, w3r, freq, deltas)


FEAT_PAD = 40


def _filter_dft_kernel(zt_ref, w1t_ref, b1_ref, w2t_ref, b2_ref, w3_ref, fr_ref, e_ref, m1_ref, o_ref, *, st):
    j = pl.program_id(0)
    n1 = zt_ref.shape[2]
    width = w3_ref.shape[1]
    row = lax.broadcasted_iota(jnp.int32, (n1, width), 0)
    first_half = lax.broadcasted_iota(jnp.int32, (3 * FILTER_HIDDEN, n1), 1) < n1 // 2
    zero = jnp.zeros((), BF16)
    contract0 = (((0,), (0,)), ((), ()))
    fr = fr_ref[...]
    cols = []
    for q in range(st):
        zt = zt_ref[q]
        a = jnp.sin(fr * (_hdot(w1t_ref[...], zt) + b1_ref[...]))
        a = jnp.sin(fr * (_hdot(w2t_ref[...], a) + b2_ref[...]))
        a_hi = a.astype(BF16)
        a_lo = (a - a_hi.astype(F32)).astype(BF16)
        stack = jnp.concatenate([a_hi, a_hi, a_lo], axis=0)
        lhs = jnp.concatenate([jnp.where(first_half, stack, zero), jnp.where(first_half, zero, stack)], axis=0)
        h = lax.dot_general(lhs, w3_ref[...], contract0, preferred_element_type=F32)
        decay = lax.dot_general(zt.astype(BF16), e_ref[...], contract0, preferred_element_type=F32)
        h = h * jnp.exp(-decay)
        h = jnp.where((row == n1 // 2) & (j * st + q == 0), 0.0, h)
        cols.append(h.astype(BF16))
    o_ref[...] = jnp.dot(m1_ref[...], jnp.concatenate(cols, axis=1), preferred_element_type=F32).astype(BF16)


def _filter_outer_dft(zt, w1t, b1, w2t, b2, w3r, fr, e, m1):
    n_s, _, n1 = zt.shape
    width = w3r.shape[1]
    st = min(8, n_s)
    const = lambda a: pl.BlockSpec(a.shape, lambda j: (0,) * a.ndim)
    return pl.pallas_call(
        functools.partial(_filter_dft_kernel, st=st),
        out_shape=jax.ShapeDtypeStruct((m1.shape[0], n_s * width), BF16),
        grid=(n_s // st,),
        in_specs=[pl.BlockSpec((st, FEAT_PAD, n1), lambda j: (j, 0, 0)), const(w1t), const(b1), const(w2t), const(b2),
                  const(w3r), const(fr), const(e), const(m1)],
        out_specs=pl.BlockSpec((m1.shape[0], st * width), lambda j: (0, j)),
        compiler_params=_cparams(("parallel",)),
        name="hyena_filter_outer_dft",
    )(zt, w1t, b1, w2t, b2, w3r, fr, e, m1)


def _dft_consts(n):
    big = 2 * n
    n1 = big // DFT_INNER
    half = n1 // 2
    k1 = np.arange(n1)[:, None]
    a = np.arange(n1)[None, :]
    ang = -2.0 * np.pi * ((k1 * a) % n1) / n1
    fr, fi = np.cos(ang), np.sin(ang)
    first = np.block([[fr[:, :half], -fi[:, :half]], [fi[:, :half], fr[:, :half]]])
    first_real = np.concatenate([fr, fi], axis=0)
    last = np.block([[fr[:half, :], fi[:half, :]], [-fi[:half, :], fr[:half, :]]])
    k2 = np.arange(DFT_INNER)[:, None]
    s = np.arange(DFT_INNER)[None, :]
    ang2 = -2.0 * np.pi * ((k2 * s) % DFT_INNER) / DFT_INNER
    gr, gi = np.cos(ang2), np.sin(ang2)
    inner = np.block([[gr, -gi], [gi, gr]])
    inner_inv = np.block([[gr, gi], [-gi, gr]])
    prod = (jnp.arange(n1, dtype=jnp.int32)[:, None] * jnp.arange(DFT_INNER, dtype=jnp.int32)[None, :]) % big
    angt = prod.astype(F32) * (-2.0 * math.pi / big)
    tw = jnp.stack([jnp.cos(angt), jnp.sin(angt)])
    tw = jnp.broadcast_to(tw[..., None], tw.shape + (LANES,))
    f = lambda m: jnp.asarray(np.ascontiguousarray(m), F32).astype(BF16)
    return dict(first=f(first), first_real=f(first_real), last=f(last), inner=f(inner), inner_inv=f(inner_inv),
                tw=tw, n1=n1, inv_scale=1.0 / big)


def _left_matmul_kernel(*refs, mode, inv_scale):
    if mode == "first":
        m_ref, x_ref, o_ref = refs
        o_ref[...] = _bdot(m_ref[...], x_ref[0]).astype(BF16)
        return
    if mode == "last":
        m3_ref, b_ref, g_ref, v_ref, bias_ref, z_ref = refs
    else:
        m3_ref, b_ref, g_ref, v_ref, bias_ref, m1_ref, z_ref, a_ref = refs
    y = _bdot(m3_ref[...], b_ref[...]) * inv_scale
    z = g_ref[0] * (y + bias_ref[...] * v_ref[0])
    z_ref[0] = z
    if mode == "last_first":
        a_ref[...] = _bdot(m1_ref[...], z).astype(BF16)


DFT_COLS = 2048


def _dft_first(m1, x3, part):
    _, rows, wtot = x3.shape
    wt = DFT_COLS
    return pl.pallas_call(
        functools.partial(_left_matmul_kernel, mode="first", inv_scale=None),
        out_shape=jax.ShapeDtypeStruct((m1.shape[0], wtot), BF16),
        grid=(wtot // wt,),
        in_specs=[pl.BlockSpec(m1.shape, lambda j: (0, 0)), pl.BlockSpec((1, rows, wt), lambda j: (part, 0, j))],
        out_specs=pl.BlockSpec((m1.shape[0], wt), lambda j: (0, j)),
        compiler_params=_cparams(("parallel",)),
        name="dft_outer_forward",
    )(m1, x3)


def _dft_last(m3, b2, gate3, gate_part, v3, v_part, bias_flat, inv_scale, m1=None):
    _, rows, wtot = gate3.shape
    wt = DFT_COLS
    col = lambda r: pl.BlockSpec((r, wt), lambda j: (0, j))
    part = lambda p: pl.BlockSpec((1, rows, wt), lambda j: (p, 0, j))
    const = lambda a: pl.BlockSpec(a.shape, lambda j: (0, 0))
    z_shape = jax.ShapeDtypeStruct((1, rows, wtot), F32)
    in_specs = [const(m3), col(b2.shape[0]), part(gate_part), part(v_part), pl.BlockSpec((1, wt), lambda j: (0, 0))]
    if m1 is None:
        return pl.pallas_call(
            functools.partial(_left_matmul_kernel, mode="last", inv_scale=inv_scale),
            out_shape=z_shape,
            grid=(wtot // wt,),
            in_specs=in_specs,
            out_specs=part(0),
            compiler_params=_cparams(("parallel",)),
            name="dft_outer_inverse",
        )(m3, b2, gate3, v3, bias_flat)
    return pl.pallas_call(
        functools.partial(_left_matmul_kernel, mode="last_first", inv_scale=inv_scale),
        out_shape=(z_shape, jax.ShapeDtypeStruct((m1.shape[0], wtot), BF16)),
        grid=(wtot // wt,),
        in_specs=in_specs + [const(m1)],
        out_specs=(part(0), col(m1.shape[0])),
        compiler_params=_cparams(("parallel",)),
        name="dft_outer_inverse_forward",
    )(m3, b2, gate3, v3, bias_flat, m1)


def _dft_mid_kernel(a_ref, tw_ref, g_ref, *rest, slabs, with_filter):
    if with_filter:
        h_ref, gi_ref, o_ref = rest
    else:
        (o_ref,) = rest
    nl = a_ref.shape[3] // LANES
    for j in range(slabs):
        twr = jnp.concatenate([tw_ref[0, j]] * nl, axis=1)
        twi = jnp.concatenate([tw_ref[1, j]] * nl, axis=1)
        ar, ai = a_ref[0, j].astype(F32), a_ref[1, j].astype(F32)
        br = ar * twr - ai * twi
        bi = ar * twi + ai * twr
        x = _bdot(g_ref[...], jnp.concatenate([br.astype(BF16), bi.astype(BF16)], axis=0))
        if not with_filter:
            o_ref[0, j] = x[:DFT_INNER]
            o_ref[1, j] = x[DFT_INNER:]
            continue
        xr, xi = x[:DFT_INNER], x[DFT_INNER:]
        hr, hi = h_ref[0, j], h_ref[1, j]
        yr = xr * hr - xi * hi
        yi = xr * hi + xi * hr
        zz = _bdot(gi_ref[...], jnp.concatenate([yr.astype(BF16), yi.astype(BF16)], axis=0))
        zr, zi = zz[:DFT_INNER], zz[DFT_INNER:]
        o_ref[0, j] = (zr * twr + zi * twi).astype(BF16)
        o_ref[1, j] = (zi * twr - zr * twi).astype(BF16)


def _dft_mid(a4, tw, inner, spec=None, inner_inv=None, order=0):
    _, n1, _, c = a4.shape
    slabs = 8 if n1 % 8 == 0 else n1
    blk = lambda w: pl.BlockSpec((2, slabs, DFT_INNER, w), lambda j: (0, j, 0, 0))
    const = lambda a: pl.BlockSpec(a.shape, lambda j: (0, 0))
    in_specs = [blk(c), blk(LANES), const(inner)]
    args = [a4, tw, inner]
    if spec is not None:
        in_specs += [pl.BlockSpec((2, slabs, DFT_INNER, c), lambda j: (0, j, 0, order)), const(inner_inv)]
        args += [spec, inner_inv]
    return pl.pallas_call(
        functools.partial(_dft_mid_kernel, slabs=slabs, with_filter=spec is not None),
        out_shape=jax.ShapeDtypeStruct(a4.shape, BF16 if spec is not None else F32),
        grid=(n1 // slabs,),
        in_specs=in_specs,
        out_specs=blk(c),
        compiler_params=_cparams(("parallel",)),
        name="dft_inner_filter" if spec is not None else "dft_inner_forward",
    )(*args)


def _hyena_latent(vx, filt_a, bias, consts):
    _, nb, n, c = vx.shape
    assert nb == 2, "the two samples are carried as the real and imaginary part of one transform"
    n1 = consts["n1"]
    wflat = DFT_INNER * c
    flat = vx.reshape(3, nb * n1 // 2, wflat)
    bias_flat = jnp.tile(bias, (1, DFT_COLS // c))
    tw, inner, inner_inv, scale = consts["tw"], consts["inner"], consts["inner_inv"], consts["inv_scale"]

    spec = _dft_mid(filt_a.reshape(2, n1, DFT_INNER, HYENA_ORDER * c), tw, inner)

    a = _dft_first(consts["first"], flat, 0).reshape(2, n1, DFT_INNER, c)
    bm = _dft_mid(a, tw, inner, spec, inner_inv, order=0)
    z, a = _dft_last(consts["last"], bm.reshape(2 * n1, wflat), flat, 1, flat, 0, bias_flat[0:1], scale,
                     consts["first"])
    bm = _dft_mid(a.reshape(2, n1, DFT_INNER, c), tw, inner, spec, inner_inv, order=1)
    y = _dft_last(consts["last"], bm.reshape(2 * n1, wflat), flat, 2, z, 0, bias_flat[1:2], scale)
    return y.reshape(nb, n, c)


def _hyena_ctx_kernel(u_ref, h_ref, fs_ref, ff_ref, fi_ref, bias_ref, o_ref, *, n):
    big = 2 * n
    c = HYENA_WIDTH
    spec = _hdot(fs_ref[...], h_ref[...])
    u = u_ref[...]
    v, x1, x2 = u[:, 0:c], u[:, c:2 * c], u[:, 2 * c:3 * c]

    def conv(zin, order):
        x = _hdot(ff_ref[...], zin)
        xr, xi = x[:big], x[big:]
        hr, hi = spec[:big, order * c:(order + 1) * c], spec[big:, order * c:(order + 1) * c]
        y = jnp.concatenate([xr * hr - xi * hi, xr * hi + xi * hr], axis=0)
        return _hdot(fi_ref[...], y)

    bias = bias_ref[...]
    z = x1 * (conv(v, 0) + bias[0:1] * v)
    o_ref[...] = x2 * (conv(z, 1) + bias[1:2] * z)


def _ctx_dft_consts(n):
    big = 2 * n
    k = np.arange(big)[:, None]
    t = np.arange(big)[None, :]
    ang = -2.0 * np.pi * ((k * t) % big) / big
    fr, fi = np.cos(ang), np.sin(ang)
    spec_m = np.concatenate([fr, fi], axis=0)
    fwd = np.block([[fr[:, :n], -fi[:, :n]], [fi[:, :n], fr[:, :n]]])
    inv = np.block([[fr.T[:n, :], fi.T[:n, :]], [-fi.T[:n, :], fr.T[:n, :]]]) / big
    f = lambda m: jnp.asarray(np.ascontiguousarray(m), F32)
    return f(spec_m), f(fwd), f(inv)


def _hyena_ctx(u_ctx, h_circ, bias, cconsts):
    nb, n, w = u_ctx.shape
    assert nb == 2
    fs, ff, fi = cconsts
    full = lambda a: pl.BlockSpec(a.shape, lambda i: (0,) * a.ndim)
    u2 = u_ctx.reshape(nb * n, w)
    out = pl.pallas_call(
        functools.partial(_hyena_ctx_kernel, n=n),
        out_shape=jax.ShapeDtypeStruct((nb * n, HYENA_WIDTH), F32),
        grid=(1,),
        in_specs=[full(u2), full(h_circ), full(fs), full(ff), full(fi), full(bias)],
        out_specs=pl.BlockSpec((nb * n, HYENA_WIDTH), lambda i: (0, 0)),
        compiler_params=_cparams(("arbitrary",)),
        name="hyena_context",
    )(u2, h_circ, fs, ff, fi, bias)
    return out.reshape(nb, n, HYENA_WIDTH)


def _gla_kernel(xf_ref, xb_ref, of_ref, ob_ref, st_ref):
    s = pl.program_id(0)

    @pl.when(s == 0)
    def _():
        st_ref[...] = jnp.zeros(st_ref.shape, F32)

    c = GLA_CHUNK
    kw, vw = GLA_K_WIDTH, GLA_V_WIDTH
    ri3 = lax.broadcasted_iota(jnp.int32, (c, 3 * c), 0)
    ci3 = lax.broadcasted_iota(jnp.int32, (c, 3 * c), 1) % c
    kr = lax.broadcasted_iota(jnp.int32, (GLA_HEADS * c, kw), 0) // c
    kc = lax.broadcasted_iota(jnp.int32, (GLA_HEADS * c, kw), 1) // GLA_DK
    mask_k = kr == kc
    vr = lax.broadcasted_iota(jnp.int32, (GLA_HEADS * c, vw), 0) // c
    vc = lax.broadcasted_iota(jnp.int32, (GLA_HEADS * c, vw), 1) // GLA_DV
    mask_v = vr == vc
    sr = lax.broadcasted_iota(jnp.int32, (vw, kw), 0) // GLA_DV
    sc = lax.broadcasted_iota(jnp.int32, (vw, kw), 1) // GLA_DK
    mask_s = sr == sc
    qi = lax.broadcasted_iota(jnp.int32, (c, GLA_HEADS * c), 0)
    kj = lax.broadcasted_iota(jnp.int32, (c, GLA_HEADS * c), 1) % c

    def chunk(x_ref, o_ref, d, b, ch):
        backward = d == 1
        tri3 = ((ri3 <= ci3) if backward else (ri3 >= ci3)).astype(BF16)
        causal = (qi <= kj) if backward else (qi >= kj)
        goff = 2 * kw + vw + (kw if backward else 0)
        r0 = ch * c
        q = x_ref[b, r0:r0 + c, 0:kw]
        k = x_ref[b, r0:r0 + c, kw:2 * kw]
        v = x_ref[b, r0:r0 + c, 2 * kw:2 * kw + vw]
        g = x_ref[b, r0:r0 + c, goff:goff + kw]
        g_hi = g.astype(BF16)
        g_mid = (g - g_hi.astype(F32)).astype(BF16)
        g_lo = (g - g_hi.astype(F32) - g_mid.astype(F32)).astype(BF16)
        gc = jnp.dot(tri3, jnp.concatenate([g_hi, g_mid, g_lo], axis=0), preferred_element_type=F32)
        g_last = gc[0:1] if backward else gc[c - 1:c]
        q_dec = q * jnp.exp(gc)
        k_dec = k * jnp.exp(-gc)
        k_tail = k * jnp.exp(g_last - gc)
        k_blk = jnp.where(mask_k, jnp.concatenate([k_dec] * GLA_HEADS, axis=0), 0.0).astype(BF16)
        v_bf = v.astype(BF16)
        v_blk = jnp.where(mask_v, jnp.concatenate([v_bf] * GLA_HEADS, axis=0), jnp.zeros((), BF16))
        q_bf = q_dec.astype(BF16)
        scores = lax.dot_general(q_bf, k_blk, (((1,), (1,)), ((), ())), preferred_element_type=F32)
        scores = jnp.where(causal, scores, 0.0)
        st = st_ref[d, b]
        o = jnp.dot(scores.astype(BF16), v_blk, preferred_element_type=F32)
        o = o + lax.dot_general(q_bf, st.astype(BF16), (((1,), (1,)), ((), ())), preferred_element_type=F32)
        o_ref[b, r0:r0 + c, :] = o
        kv_t = lax.dot_general(v_bf, k_tail.astype(BF16), (((0,), (0,)), ((), ())), preferred_element_type=F32)
        st_ref[d, b] = st * jnp.exp(g_last) + jnp.where(mask_s, kv_t, 0.0)

    n_chunks = xf_ref.shape[1] // c
    for u in range(n_chunks):
        for b in range(xf_ref.shape[0]):
            chunk(xf_ref, of_ref, 0, b, u)
            chunk(xb_ref, ob_ref, 1, b, n_chunks - 1 - u)


def _gla(gla_in, *, ctx_len):
    nb, t, w = gla_in.shape
    tb = TOK_TILE
    nblk = t // tb
    cblk = ctx_len // tb
    fwd = lambda s: (0, s, 0)
    bwd = lambda s: (0, jnp.where(s < cblk, cblk - 1 - s, nblk - 1 - (s - cblk)), 0)
    out = jax.ShapeDtypeStruct((nb, t, GLA_V_WIDTH), F32)
    return pl.pallas_call(
        _gla_kernel,
        out_shape=(out, out),
        grid=(nblk,),
        in_specs=[pl.BlockSpec((nb, tb, w), fwd), pl.BlockSpec((nb, tb, w), bwd)],
        out_specs=(pl.BlockSpec((nb, tb, GLA_V_WIDTH), fwd), pl.BlockSpec((nb, tb, GLA_V_WIDTH), bwd)),
        scratch_shapes=[pltpu.VMEM((2, nb, GLA_V_WIDTH, GLA_K_WIDTH), F32)],
        compiler_params=_cparams(("arbitrary",)),
        name="gla_scans",
    )(gla_in, gla_in)


def _out_kernel(h_ref, mod_ref, at_ref, hy_ref, gf_ref, gb_ref, gr_ref, gn_ref, gmat_ref, w_ref, o_ref, *,
                n_batch, ctx_tiles, tile_off, d_model):
    b, i = pl.program_id(0), pl.program_id(1)
    mod = _mod_row(mod_ref, i + tile_off < ctx_tiles, b, n_batch)
    gate = mod[:, 2 * d_model:3 * d_model]
    gmat = gmat_ref[...]
    gn = gn_ref[...]
    a0, a1 = ATTN_WIDTH, ATTN_WIDTH + HYENA_WIDTH
    ya = _head_rms(at_ref[0], gn[:, :a0], gmat)
    yh = _head_rms(hy_ref[0], gn[:, a0:a1], gmat)
    yg = _head_rms(gf_ref[0] + gb_ref[0], gn[:, a1:], gmat) * _silu(gr_ref[0])
    y = jnp.concatenate([ya, yh, yg], axis=1)
    o_ref[0] = h_ref[0] + gate * _bdot(y, w_ref[...])


def _out_projection(h, mod, attn, hy, gla_f, gla_b, gr, gn, gmat, w_out, *, ctx_len, skip_ctx):
    nb, t, d = h.shape
    tb = TOK_TILE
    off = ctx_len // tb if skip_ctx else 0
    kern = functools.partial(_out_kernel, n_batch=nb, ctx_tiles=ctx_len // tb, tile_off=off, d_model=d)
    const = lambda a: pl.BlockSpec(a.shape, lambda b, i: (0,) * a.ndim)
    tok = lambda w: pl.BlockSpec((1, tb, w), lambda b, i: (b, i + off, 0))
    return pl.pallas_call(
        kern,
        out_shape=jax.ShapeDtypeStruct((nb, t - off * tb, d), F32),
        grid=(nb, t // tb - off),
        in_specs=[tok(d), const(mod), tok(ATTN_WIDTH), tok(HYENA_WIDTH), tok(GLA_V_WIDTH), tok(GLA_V_WIDTH),
                  tok(GLA_V_WIDTH), const(gn), const(gmat), const(w_out)],
        out_specs=pl.BlockSpec((1, tb, d), lambda b, i: (b, i, 0)),
        compiler_params=_cparams(("parallel", "parallel")),
        name="out_projection",
    )(h, mod, attn, hy, gla_f, gla_b, gr, gn, gmat, w_out)


def _ffn_kernel(h_ref, mod_ref, g2_ref, w1_ref, w3_ref, w2_ref, fg_ref, o_ref, *, n_batch, ctx_tiles, tile_off,
                d_model, final):
    b, i = pl.program_id(0), pl.program_id(1)
    mod = _mod_row(mod_ref, i + tile_off < ctx_tiles, b, n_batch)
    shift, scale, gate = (mod[:, 3 * d_model:4 * d_model], mod[:, 4 * d_model:5 * d_model],
                          mod[:, 5 * d_model:6 * d_model])
    x = h_ref[0]
    xn = x * lax.rsqrt(jnp.mean(jnp.square(x), axis=-1, keepdims=True) + NORM_EPS) * g2_ref[...]
    u = (xn * (1.0 + scale) + shift).astype(BF16)
    a = jnp.dot(u, w1_ref[...], preferred_element_type=F32)
    c = jnp.dot(u, w3_ref[...], preferred_element_type=F32)
    hid = (_silu(a) * c).astype(BF16)
    y = x + gate * jnp.dot(hid, w2_ref[...], preferred_element_type=F32)
    if final:
        y = y * lax.rsqrt(jnp.mean(jnp.square(y), axis=-1, keepdims=True) + NORM_EPS) * fg_ref[...]
    o_ref[0] = y


def _ffn(h, mod, g2, w1, w3, w2, fg, *, ctx_len, lat_only, final):
    nb, t, d = h.shape
    tb = TOK_TILE
    off = ctx_len // tb if lat_only else 0
    kern = functools.partial(_ffn_kernel, n_batch=nb, ctx_tiles=ctx_len // tb, tile_off=off, d_model=d, final=final)
    const = lambda a: pl.BlockSpec(a.shape, lambda b, i: (0,) * a.ndim)
    tok = pl.BlockSpec((1, tb, d), lambda b, i: (b, i, 0))
    return pl.pallas_call(
        kern,
        out_shape=jax.ShapeDtypeStruct((nb, t, d), F32),
        grid=(nb, t // tb),
        in_specs=[tok, const(mod), const(g2), const(w1), const(w3), const(w2), const(fg)],
        out_specs=tok,
        compiler_params=_cparams(("parallel", "parallel")),
        name="swiglu_ffn",
    )(h, mod, g2, w1, w3, w2, fg)


def _rope_tables(n_lat, ctx_len):
    rows = n_lat // GRID_W
    row = jnp.broadcast_to(jnp.arange(rows, dtype=F32)[:, None], (rows, GRID_W)).reshape(-1)
    col = jnp.broadcast_to(jnp.arange(GRID_W, dtype=F32)[None, :], (rows, GRID_W)).reshape(-1)
    n_freq = HEAD_DIM // 4
    inv_freq = jnp.power(ROPE_THETA, -jnp.arange(n_freq, dtype=F32) / n_freq)
    ang = jnp.concatenate([row[:, None] * inv_freq, col[:, None] * inv_freq], axis=-1)
    cos, sin = jnp.cos(ang), jnp.sin(ang)
    cos = jnp.concatenate([jnp.ones((ctx_len, HEAD_DIM // 2), F32), cos], axis=0)
    sin = jnp.concatenate([jnp.zeros((ctx_len, HEAD_DIM // 2), F32), sin], axis=0)
    cosf = jnp.tile(cos, (1, 2 * LANES // HEAD_DIM))
    sinf = jnp.tile(jnp.concatenate([-sin, sin], axis=1), (1, LANES // HEAD_DIM))
    return cosf, sinf


def kernel(x, c, ctx, c_ctx, ada_w, ada_b, norm1_g, w_in, q_norm_g, k_norm_g, hy_conv_w, hy_conv_b, filt_w1, filt_b1, filt_w2, filt_b2, filt_w3, filt_freq, hy_bias, gla_gate_w, gla_gate_b, out_norm_g, w_out, norm2_g, ffn_w1, ffn_w3, ffn_w2, final_norm_g):
    nb, n_lat, d = x.shape
    ctx_len = ctx.shape[1]
    depth = w_in.shape[0]
    assert nb == 2 and ctx_len % TOK_TILE == 0 and n_lat % TOK_TILE == 0 and n_lat % DFT_INNER == 0

    h = jnp.concatenate([ctx, x], axis=1)
    cond = jnp.concatenate([c, c_ctx[None, :], jnp.zeros((8 - nb - 1, d), F32)], axis=0)
    mods = _modulation(cond, ada_w, ada_b)

    cosf, sinf = _rope_tables(n_lat, ctx_len)
    lane = np.arange(LANES)
    gmat = jnp.asarray(np.tile((lane[:, None] // HEAD_DIM) == (lane[None, :] // HEAD_DIM), (2, 1)), BF16)
    consts = _dft_consts(n_lat)
    cconsts = _ctx_dft_consts(ctx_len)
    zc_ctx = _filter_features(ctx_len)
    deltas = jnp.abs(jnp.linspace(HYENA_MIN_DECAY, HYENA_MAX_DECAY, HYENA_WIDTH, dtype=F32))
    deltas = jnp.tile(deltas[None, :], (1, HYENA_ORDER))
    n1 = consts["n1"]
    rows_sa = (DFT_INNER * jnp.arange(n1, dtype=jnp.int32)[None, :] + jnp.arange(DFT_INNER, dtype=jnp.int32)[:, None])
    zt_lat = _circular_features(rows_sa, n_lat).transpose(1, 0, 2)
    t_hi, t_mid, t_lo = _split3(zt_lat[:, 0, :])
    d_hi, d_mid, d_lo = _split3(deltas[0])
    zt_lat = jnp.concatenate([zt_lat] + [p[:, None, :] for p in (t_hi, t_hi, t_hi, t_mid, t_mid, t_lo)], axis=1)
    n_feat = zt_lat.shape[1]
    zt_lat = jnp.pad(zt_lat, ((0, 0), (0, FEAT_PAD - n_feat), (0, 0)))
    decay_rows = jnp.zeros((FEAT_PAD, deltas.shape[1]), F32).at[n_feat - 6:n_feat].set(
        jnp.stack([d_hi, d_mid, d_lo, d_hi, d_mid, d_hi])).astype(BF16)
    lanes_n1 = lambda vec: jnp.broadcast_to(vec[:, None], (vec.shape[0], n1))

    w_in_b = jnp.pad(w_in, ((0, 0), (0, 0), (0, IN_PAD - w_in.shape[2]))).astype(BF16)
    w_out_b = w_out.astype(BF16)
    w1_b, w3_b, w2_b = ffn_w1.astype(BF16), ffn_w3.astype(BF16), ffn_w2.astype(BF16)
    fg = final_norm_g.reshape(1, d)

    out = None
    for l in range(depth):
        last = l == depth - 1
        mod = mods[l]
        wg = jnp.zeros((LANES, 2 * GLA_K_WIDTH), F32)
        wg = wg.at[0:GLA_GATE_RANK, 0:GLA_K_WIDTH].set(gla_gate_w[l, 0])
        wg = wg.at[GLA_GATE_RANK:2 * GLA_GATE_RANK, GLA_K_WIDTH:].set(gla_gate_w[l, 1])
        bg = gla_gate_b[l].reshape(1, 2 * GLA_K_WIDTH)
        qg = jnp.tile(q_norm_g[l][None, :], (1, ATTN_HEADS))
        kg = jnp.tile(k_norm_g[l][None, :], (1, ATTN_KV_HEADS))

        q, k, v, hy, gla_in, gr = _in_projection(h, mod, norm1_g[l].reshape(1, d), w_in_b[l], qg, kg, cosf, sinf,
                                                 gmat, wg.astype(BF16), bg, ctx_len=ctx_len)
        attn = _attention(q, k, v, ctx_len=ctx_len)

        vx, u_ctx = _short_conv(hy, hy_conv_w[l], hy_conv_b[l], ctx_len=ctx_len)
        w1p = jnp.pad(filt_w1[l], ((0, LANES - filt_w1.shape[1]), (0, 0)))
        w3r = filt_w3[l].reshape(FILTER_HIDDEN, HYENA_ORDER, 2, HYENA_WIDTH).transpose(2, 0, 1, 3)
        w3r = w3r.reshape(2, FILTER_HIDDEN, HYENA_ORDER * HYENA_WIDTH)
        fargs = (w1p, filt_b1[l][None, :], filt_w2[l], filt_b2[l][None, :], w3r, filt_freq[l][None, :], deltas)
        w1t = jnp.pad(filt_w1[l].T, ((0, 0), (0, FEAT_PAD - filt_w1.shape[1])))
        w3_hi, w3_lo, _ = _split3(w3r)
        w3_stack = jnp.concatenate([w3_hi, w3_lo, w3_hi], axis=1).reshape(-1, w3r.shape[2]).astype(BF16)
        filt_a = _filter_outer_dft(zt_lat, w1t, lanes_n1(filt_b1[l]), filt_w2[l].T, lanes_n1(filt_b2[l]), w3_stack,
                                   lanes_n1(filt_freq[l]), decay_rows, consts["first_real"])
        hy_lat = _hyena_latent(vx, filt_a, hy_bias[l], consts)
        if not last:
            h_ctx = _hyena_filter(zc_ctx, *fargs, n=ctx_len)
            hy_ctx = _hyena_ctx(u_ctx, h_ctx, hy_bias[l], cconsts)
        else:
            hy_ctx = jnp.zeros((nb, ctx_len, HYENA_WIDTH), F32)
        hy_out = jnp.concatenate([hy_ctx, hy_lat], axis=1)

        gla_f, gla_b = _gla(gla_in, ctx_len=ctx_len)

        h = _out_projection(h, mod, attn, hy_out, gla_f, gla_b, gr, out_norm_g[l].reshape(1, -1), gmat, w_out_b[l],
                            ctx_len=ctx_len, skip_ctx=last)
        h = _ffn(h, mod, norm2_g[l].reshape(1, d), w1_b[l], w3_b[l], w2_b[l], fg, ctx_len=ctx_len, lat_only=last,
                 final=last)
    return h
```

```python
import functools
import math

import numpy as np
import jax
import jax.numpy as jnp
from jax import lax
from jax.experimental import pallas as pl
from jax.experimental.pallas import tpu as pltpu

F32 = jnp.float32
BF16 = jnp.bfloat16
HI = lax.Precision.HIGHEST

LANES = 128
VMEM_LIMIT = 56 * 1024 * 1024

HEAD_DIM = 64
ATTN_HEADS = 8
ATTN_KV_HEADS = 2
ATTN_WIDTH = ATTN_HEADS * HEAD_DIM
KV_WIDTH = ATTN_KV_HEADS * HEAD_DIM
HYENA_WIDTH = 256
HYENA_ORDER = 2
FILTER_BANDS = 16
FILTER_HIDDEN = 64
GLA_HEADS = 4
GLA_DK = 32
GLA_DV = 64
GLA_K_WIDTH = GLA_HEADS * GLA_DK
GLA_V_WIDTH = GLA_HEADS * GLA_DV
GLA_GATE_RANK = 16
GLA_GATE_TAU = 16.0
GLA_CHUNK = 64
GRID_W = 64
ROPE_THETA = 10000.0
N_MOD = 6
NORM_EPS = 1e-6
HYENA_TARGET = 1e-2
HYENA_MIN_DECAY = math.log(HYENA_TARGET) / 1.5
HYENA_MAX_DECAY = math.log(HYENA_TARGET) / 0.3

TOK_TILE = 256
DFT_INNER = 128
IN_PAD = 2432

_OQ, _OK, _OV, _OHY = 0, 512, 640, 768
_OGQ, _OGK, _OGV, _OGR, _OGA = 1536, 1664, 1792, 2048, 2304


def _cparams(sem, vmem=VMEM_LIMIT):
    return pltpu.CompilerParams(dimension_semantics=sem, vmem_limit_bytes=vmem)


def _bdot(a, b):
    return jnp.dot(a.astype(BF16), b.astype(BF16), preferred_element_type=F32)


def _hdot(a, b):
    return jnp.dot(a, b, preferred_element_type=F32, precision=HI)


def _group_mean_square(x, gmat):
    outs = []
    for j in range(x.shape[1] // LANES):
        sq = jnp.square(x[:, j * LANES:(j + 1) * LANES])
        hi = sq.astype(BF16)
        lo = (sq - hi.astype(F32)).astype(BF16)
        s = jnp.dot(jnp.concatenate([hi, lo], axis=1), gmat, preferred_element_type=F32)
        outs.append(s * (1.0 / HEAD_DIM))
    return outs[0] if len(outs) == 1 else jnp.concatenate(outs, axis=1)


def _head_rms(x, gain, gmat):
    ms = _group_mean_square(x, gmat)
    return x * lax.rsqrt(ms + NORM_EPS) * gain


def _mod_row(mod_ref, is_ctx, b, n_batch):
    row = jnp.where(is_ctx, n_batch, b)
    return mod_ref[pl.ds(row, 1), :]


def _silu(x):
    return x * (1.0 / (1.0 + jnp.exp(-x)))


def _split3(x):
    hi = x.astype(BF16).astype(F32)
    mid = (x - hi).astype(BF16).astype(F32)
    lo = (x - hi - mid).astype(BF16).astype(F32)
    return hi, mid, lo


def _mod_kernel(c_ref, w_ref, b_ref, o_ref):
    s = _silu(c_ref[...])
    o_ref[0] = _bdot(s, w_ref[0]) + b_ref[0]


def _modulation(cond_rows, ada_w, ada_b):
    depth, d, w = ada_w.shape
    ct = 1536
    return pl.pallas_call(
        _mod_kernel,
        out_shape=jax.ShapeDtypeStruct((depth, 8, w), F32),
        grid=(depth, w // ct),
        in_specs=[pl.BlockSpec((8, d), lambda l, j: (0, 0)),
                  pl.BlockSpec((1, d, ct), lambda l, j: (l, 0, j)),
                  pl.BlockSpec((1, 1, ct), lambda l, j: (l, 0, j))],
        out_specs=pl.BlockSpec((1, 8, ct), lambda l, j: (l, 0, j)),
        compiler_params=_cparams(("arbitrary", "arbitrary")),
        name="adaln_modulation",
    )(cond_rows, ada_w, ada_b.reshape(depth, 1, w))


def _in_kernel(h_ref, mod_ref, g1_ref, w_ref, qg_ref, kg_ref, cos_ref, sin_ref, gmat_ref, wg_ref, bg_ref,
               q_ref, k_ref, v_ref, hy_ref, gla_ref, gr_ref, *, n_batch, ctx_tiles, d_model):
    b, i = pl.program_id(0), pl.program_id(1)
    mod = _mod_row(mod_ref, i < ctx_tiles, b, n_batch)
    shift, scale = mod[:, 0:d_model], mod[:, d_model:2 * d_model]
    x = h_ref[0]
    xn = x * lax.rsqrt(jnp.mean(jnp.square(x), axis=-1, keepdims=True) + NORM_EPS) * g1_ref[...]
    u = xn * (1.0 + scale) + shift
    proj = _bdot(u, w_ref[...])

    gmat = gmat_ref[...]
    cosf, sinf = cos_ref[...], sin_ref[...]
    lane = lax.broadcasted_iota(jnp.int32, (x.shape[0], LANES), 1)
    first_half = (lane % HEAD_DIM) < (HEAD_DIM // 2)
    low_head = lane < HEAD_DIM

    def rope(t):
        partner = jnp.where(first_half, pltpu.roll(t, LANES - HEAD_DIM // 2, 1), pltpu.roll(t, HEAD_DIM // 2, 1))
        return t * cosf + partner * sinf

    qn = _head_rms(proj[:, _OQ:_OQ + ATTN_WIDTH], qg_ref[...], gmat)
    zero = jnp.zeros((x.shape[0], LANES), F32)
    for p in range(ATTN_HEADS // 2):
        t = rope(qn[:, p * LANES:(p + 1) * LANES]) * (HEAD_DIM ** -0.5 * math.log2(math.e))
        tr = pltpu.roll(t, HEAD_DIM, 1)
        if 2 * p < ATTN_HEADS // ATTN_KV_HEADS:
            q_ref[0, 2 * p] = jnp.where(low_head, t, zero).astype(BF16)
            q_ref[0, 2 * p + 1] = jnp.where(low_head, tr, zero).astype(BF16)
        else:
            q_ref[0, 2 * p] = jnp.where(low_head, zero, tr).astype(BF16)
            q_ref[0, 2 * p + 1] = jnp.where(low_head, zero, t).astype(BF16)
    kn = _head_rms(proj[:, _OK:_OK + KV_WIDTH], kg_ref[...], gmat)
    k_ref[0] = rope(kn).astype(BF16)
    v_ref[0] = proj[:, _OV:_OV + KV_WIDTH].astype(BF16)
    hy_ref[0] = proj[:, _OHY:_OHY + 3 * HYENA_WIDTH]

    zg = _bdot(proj[:, _OGA:_OGA + LANES], wg_ref[...]) + bg_ref[...]
    log_gate = (jnp.minimum(zg, 0.0) - jnp.log(1.0 + jnp.exp(-jnp.abs(zg)))) * (1.0 / GLA_GATE_TAU)
    gla_ref[0, :, 0:GLA_K_WIDTH] = proj[:, _OGQ:_OGQ + GLA_K_WIDTH] * (GLA_DK ** -0.5)
    gla_ref[0, :, GLA_K_WIDTH:2 * GLA_K_WIDTH + GLA_V_WIDTH] = proj[:, _OGK:_OGK + GLA_K_WIDTH + GLA_V_WIDTH]
    gla_ref[0, :, 2 * GLA_K_WIDTH + GLA_V_WIDTH:] = log_gate
    gr_ref[0] = proj[:, _OGR:_OGR + GLA_V_WIDTH]


def _in_projection(h, mod, g1, w_in, qg, kg, cosf, sinf, gmat, wg, bg, *, ctx_len):
    nb, t, d = h.shape
    tb = TOK_TILE
    kern = functools.partial(_in_kernel, n_batch=nb, ctx_tiles=ctx_len // tb, d_model=d)
    const = lambda shape: pl.BlockSpec(shape, lambda b, i: (0,) * len(shape))
    tok = lambda w: pl.BlockSpec((1, tb, w), lambda b, i: (b, i, 0))
    gla_w = 2 * GLA_K_WIDTH + GLA_V_WIDTH + 2 * GLA_K_WIDTH
    return pl.pallas_call(
        kern,
        out_shape=(jax.ShapeDtypeStruct((nb, ATTN_HEADS, t, LANES), BF16),
                   jax.ShapeDtypeStruct((nb, t, KV_WIDTH), BF16),
                   jax.ShapeDtypeStruct((nb, t, KV_WIDTH), BF16),
                   jax.ShapeDtypeStruct((nb, t, 3 * HYENA_WIDTH), F32),
                   jax.ShapeDtypeStruct((nb, t, gla_w), F32),
                   jax.ShapeDtypeStruct((nb, t, GLA_V_WIDTH), F32)),
        grid=(nb, t // tb),
        in_specs=[tok(d), const(mod.shape), const(g1.shape), const(w_in.shape), const(qg.shape), const(kg.shape),
                  pl.BlockSpec((tb, LANES), lambda b, i: (i, 0)), pl.BlockSpec((tb, LANES), lambda b, i: (i, 0)),
                  const(gmat.shape), const(wg.shape), const(bg.shape)],
        out_specs=(pl.BlockSpec((1, ATTN_HEADS, tb, LANES), lambda b, i: (b, 0, i, 0)),
                   tok(KV_WIDTH), tok(KV_WIDTH), tok(3 * HYENA_WIDTH), tok(gla_w), tok(GLA_V_WIDTH)),
        compiler_params=_cparams(("parallel", "parallel")),
        name="in_projection",
    )(h, mod, g1, w_in, qg, kg, cosf, sinf, gmat, wg, bg)


KV_UNROLL = 4


def _attn_kernel(q_ref, k_ref, v_ref, o_ref, m_sc, acc_sc, s_sc, mx_sc, p_sc, *, ctx_len, ctx_tiles, kv_tile,
                 n_kv_tiles):
    i = pl.program_id(1)
    tq = q_ref.shape[2]
    per_kv = ATTN_HEADS // ATTN_KV_HEADS
    grp = per_kv * tq
    m_sc[...] = jnp.full(m_sc.shape, -jnp.inf, F32)
    acc_sc[...] = jnp.zeros(acc_sc.shape, F32)

    def scores(start, size, buf):
        q = q_ref[0].reshape(ATTN_HEADS * tq, LANES)
        kc = k_ref[0, pl.ds(start, size), :]
        s = lax.dot_general(q, kc, (((1,), (1,)), ((), ())), preferred_element_type=F32)
        s_sc[buf, :, 0:size] = s
        mx_sc[buf] = jnp.broadcast_to(jnp.max(s, axis=1, keepdims=True), mx_sc.shape[1:])

    def consume(start, size, buf, pbuf):
        vc = v_ref[0, pl.ds(start, size), :]
        low = lax.broadcasted_iota(jnp.int32, vc.shape, 1) < HEAD_DIM
        one = jnp.ones((), BF16)
        v_ones = (jnp.where(low, vc, one), jnp.where(low, one, vc))
        for g in range(ATTN_KV_HEADS):
            alphas = []
            for hh in range(g * per_kv, (g + 1) * per_kv):
                rows = slice(hh * tq, (hh + 1) * tq)
                m_prev = m_sc[rows, :]
                m_new = jnp.maximum(m_prev, mx_sc[buf, rows, :])
                alphas.append(jnp.exp2(m_prev - m_new))
                for j in range(size // LANES):
                    cols = slice(j * LANES, (j + 1) * LANES)
                    p_sc[pbuf, rows, cols] = jnp.exp2(s_sc[buf, rows, cols] - m_new).astype(BF16)
                m_sc[rows, :] = m_new
            rows_g = slice(g * grp, (g + 1) * grp)
            pv = jnp.dot(p_sc[pbuf, rows_g, 0:size], v_ones[g], preferred_element_type=F32)
            acc_sc[rows_g, :] = jnp.concatenate(alphas, axis=0) * acc_sc[rows_g, :] + pv

    @pl.when(i < ctx_tiles)
    def _():
        scores(0, ctx_len, 0)
        consume(0, ctx_len, 0, 0)

    @pl.when(i >= ctx_tiles)
    def _():
        def tile_start(j):
            return pl.multiple_of(j * kv_tile, kv_tile)

        def run(j0, count, tail):
            for u in range(count):
                if not (tail and u + 1 == count):
                    scores(tile_start(j0 + u + 1), kv_tile, (u + 1) % 2)
                consume(tile_start(j0 + u), kv_tile, u % 2, u)

        trips = (n_kv_tiles - 1) // KV_UNROLL
        scores(0, kv_tile, 0)
        lax.fori_loop(0, trips, lambda k, c: (run(k * KV_UNROLL, KV_UNROLL, False), c)[1], 0)
        run(trips * KV_UNROLL, n_kv_tiles - trips * KV_UNROLL, True)

    low_head = lax.broadcasted_iota(jnp.int32, (tq, LANES), 1) < HEAD_DIM
    for p in range(ATTN_HEADS // 2):
        outs = []
        for hh in (2 * p, 2 * p + 1):
            acc = acc_sc[hh * tq:(hh + 1) * tq, :]
            o = acc / pltpu.roll(acc, HEAD_DIM, 1)
            src_low = hh < per_kv
            dst_low = hh % 2 == 0
            outs.append(o if src_low == dst_low else pltpu.roll(o, HEAD_DIM, 1))
        o_ref[0, :, p * LANES:(p + 1) * LANES] = jnp.where(low_head, outs[0], outs[1])


def _attention(q, k, v, *, ctx_len):
    nb, nh, t, _ = q.shape
    tq = TOK_TILE
    n_lat = t - ctx_len
    kv_tile = max(w for w in (256, 512, 768, 1024) if t % w == 0)
    assert ctx_len <= kv_tile
    rows = nh * tq
    kern = functools.partial(_attn_kernel, ctx_len=ctx_len, ctx_tiles=ctx_len // tq, kv_tile=kv_tile,
                             n_kv_tiles=t // kv_tile)
    return pl.pallas_call(
        kern,
        out_shape=jax.ShapeDtypeStruct((nb, t, ATTN_WIDTH), F32),
        grid=(nb, t // tq),
        in_specs=[pl.BlockSpec((1, nh, tq, LANES), lambda b, i: (b, 0, i, 0)),
                  pl.BlockSpec((1, t, KV_WIDTH), lambda b, i: (b, 0, 0)),
                  pl.BlockSpec((1, t, KV_WIDTH), lambda b, i: (b, 0, 0))],
        out_specs=pl.BlockSpec((1, tq, ATTN_WIDTH), lambda b, i: (b, i, 0)),
        scratch_shapes=[pltpu.VMEM((rows, LANES), F32), pltpu.VMEM((rows, LANES), F32),
                        pltpu.VMEM((2, rows, kv_tile), F32), pltpu.VMEM((2, rows, LANES), F32),
                        pltpu.VMEM((KV_UNROLL, rows, kv_tile), BF16)],
        compiler_params=_cparams(("parallel", "arbitrary")),
        name="gqa_attention",
    )(q, k, v)


def _short_conv_kernel(x_ref, w_ref, b_ref, lat_ref, ctx_ref, *, ctx_len):
    x = x_ref[0]
    t = x.shape[0]
    row = lax.broadcasted_iota(jnp.int32, x.shape, 0)
    prev = jnp.where((row == 0) | (row == ctx_len), 0.0, pltpu.roll(x, 1, 0))
    nxt = jnp.where((row == ctx_len - 1) | (row == t - 1), 0.0, pltpu.roll(x, t - 1, 0))
    w = w_ref[...]
    u = prev * w[0:1, :] + x * w[1:2, :] + nxt * w[2:3, :] + b_ref[...]
    ctx_ref[0] = u[:ctx_len]
    lat_ref[0, 0] = u[ctx_len:]


def _short_conv(hy, conv_w, conv_b, *, ctx_len):
    nb, t, w = hy.shape
    per = HYENA_WIDTH // LANES
    return pl.pallas_call(
        functools.partial(_short_conv_kernel, ctx_len=ctx_len),
        out_shape=(jax.ShapeDtypeStruct((3, nb, t - ctx_len, HYENA_WIDTH), F32),
                   jax.ShapeDtypeStruct((nb, ctx_len, w), F32)),
        grid=(nb, w // LANES),
        in_specs=[pl.BlockSpec((1, t, LANES), lambda b, j: (b, 0, j)),
                  pl.BlockSpec((3, LANES), lambda b, j: (0, j)),
                  pl.BlockSpec((1, LANES), lambda b, j: (0, j))],
        out_specs=(pl.BlockSpec((1, 1, t - ctx_len, LANES), lambda b, j: (j // per, b, 0, j % per)),
                   pl.BlockSpec((1, ctx_len, LANES), lambda b, j: (b, 0, j))),
        compiler_params=_cparams(("parallel", "parallel")),
        name="hyena_short_conv",
    )(hy, conv_w, conv_b.reshape(1, w))


def _filter_kernel(z_ref, w1_ref, b1_ref, w2_ref, b2_ref, w3_ref, fr_ref, dl_ref, o_ref, *, n):
    r = pl.program_id(0)
    z = z_ref[...]
    fr = fr_ref[...]
    a = jnp.sin(fr * (_hdot(z, w1_ref[...]) + b1_ref[...]))
    a = jnp.sin(fr * (_hdot(a, w2_ref[...]) + b2_ref[...]))
    h = _hdot(a, w3_ref[0])
    window = jnp.exp(-z[:, 0:1] * dl_ref[...])
    row = r * z.shape[0] + lax.broadcasted_iota(jnp.int32, h.shape, 0)
    o_ref[...] = jnp.where(row == n, 0.0, h * window)


def _circular_features(r, n):
    p = jnp.where(r < n, r, 2 * n - r)
    p = jnp.where(r == n, 0, p).astype(F32)
    t = p / (n - 1)
    omega = (2.0 * math.pi / n) * p
    bands = jnp.linspace(1e-4, FILTER_BANDS - 1, FILTER_BANDS, dtype=F32).reshape((-1,) + (1,) * r.ndim)
    phase = omega[None] * bands
    return jnp.concatenate([t[None], jnp.cos(phase), -jnp.sin(phase)], axis=0)


def _filter_features(n):
    z = _circular_features(jnp.arange(2 * n, dtype=jnp.int32), n).T
    return jnp.pad(z, ((0, 0), (0, LANES - z.shape[1])))


def _hyena_filter(zc, w1p, b1, w2, b2, w3r, freq, deltas, *, n):
    rt = min(512, n)
    width = HYENA_ORDER * HYENA_WIDTH
    const = lambda shape: pl.BlockSpec(shape, lambda r: (0,) * len(shape))
    return pl.pallas_call(
        functools.partial(_filter_kernel, n=n),
        out_shape=jax.ShapeDtypeStruct((2 * n, width), F32),
        grid=(2 * n // rt,),
        in_specs=[pl.BlockSpec((rt, LANES), lambda r: (r, 0)), const(w1p.shape), const(b1.shape), const(w2.shape),
                  const(b2.shape), pl.BlockSpec((1, FILTER_HIDDEN, width), lambda r: (r // (n // rt), 0, 0)),
                  const(freq.shape), const(deltas.shape)],
        out_specs=pl.BlockSpec((rt, width), lambda r: (r, 0)),
        compiler_params=_cparams(("parallel",)),
        name="hyena_filter",
    )(zc, w1p, b1, w2, b2, w3r, freq, deltas)


FEAT_PAD = 40


def _filter_dft_kernel(zt_ref, w1t_ref, b1_ref, w2t_ref, b2_ref, w3_ref, fr_ref, e_ref, m1_ref, o_ref, *, st):
    j = pl.program_id(0)
    n1 = zt_ref.shape[2]
    width = w3_ref.shape[1]
    row = lax.broadcasted_iota(jnp.int32, (n1, width), 0)
    first_half = lax.broadcasted_iota(jnp.int32, (3 * FILTER_HIDDEN, n1), 1) < n1 // 2
    zero = jnp.zeros((), BF16)
    contract0 = (((0,), (0,)), ((), ()))
    fr = fr_ref[...]
    cols = []
    for q in range(st):
        zt = zt_ref[q]
        a = jnp.sin(fr * (_hdot(w1t_ref[...], zt) + b1_ref[...]))
        a = jnp.sin(fr * (_hdot(w2t_ref[...], a) + b2_ref[...]))
        a_hi = a.astype(BF16)
        a_lo = (a - a_hi.astype(F32)).astype(BF16)
        stack = jnp.concatenate([a_hi, a_hi, a_lo], axis=0)
        lhs = jnp.concatenate([jnp.where(first_half, stack, zero), jnp.where(first_half, zero, stack)], axis=0)
        h = lax.dot_general(lhs, w3_ref[...], contract0, preferred_element_type=F32)
        decay = lax.dot_general(zt.astype(BF16), e_ref[...], contract0, preferred_element_type=F32)
        h = h * jnp.exp(-decay)
        h = jnp.where((row == n1 // 2) & (j * st + q == 0), 0.0, h)
        cols.append(h.astype(BF16))
    o_ref[...] = jnp.dot(m1_ref[...], jnp.concatenate(cols, axis=1), preferred_element_type=F32).astype(BF16)


def _filter_outer_dft(zt, w1t, b1, w2t, b2, w3r, fr, e, m1):
    n_s, _, n1 = zt.shape
    width = w3r.shape[1]
    st = min(8, n_s)
    const = lambda a: pl.BlockSpec(a.shape, lambda j: (0,) * a.ndim)
    return pl.pallas_call(
        functools.partial(_filter_dft_kernel, st=st),
        out_shape=jax.ShapeDtypeStruct((m1.shape[0], n_s * width), BF16),
        grid=(n_s // st,),
        in_specs=[pl.BlockSpec((st, FEAT_PAD, n1), lambda j: (j, 0, 0)), const(w1t), const(b1), const(w2t), const(b2),
                  const(w3r), const(fr), const(e), const(m1)],
        out_specs=pl.BlockSpec((m1.shape[0], st * width), lambda j: (0, j)),
        compiler_params=_cparams(("parallel",)),
        name="hyena_filter_outer_dft",
    )(zt, w1t, b1, w2t, b2, w3r, fr, e, m1)


def _dft_consts(n):
    big = 2 * n
    n1 = big // DFT_INNER
    half = n1 // 2
    k1 = np.arange(n1)[:, None]
    a = np.arange(n1)[None, :]
    ang = -2.0 * np.pi * ((k1 * a) % n1) / n1
    fr, fi = np.cos(ang), np.sin(ang)
    first = np.block([[fr[:, :half], -fi[:, :half]], [fi[:, :half], fr[:, :half]]])
    first_real = np.concatenate([fr, fi], axis=0)
    last = np.block([[fr[:half, :], fi[:half, :]], [-fi[:half, :], fr[:half, :]]])
    k2 = np.arange(DFT_INNER)[:, None]
    s = np.arange(DFT_INNER)[None, :]
    ang2 = -2.0 * np.pi * ((k2 * s) % DFT_INNER) / DFT_INNER
    gr, gi = np.cos(ang2), np.sin(ang2)
    inner = np.block([[gr, -gi], [gi, gr]])
    inner_inv = np.block([[gr, gi], [-gi, gr]])
    prod = (jnp.arange(n1, dtype=jnp.int32)[:, None] * jnp.arange(DFT_INNER, dtype=jnp.int32)[None, :]) % big
    angt = prod.astype(F32) * (-2.0 * math.pi / big)
    tw = jnp.stack([jnp.cos(angt), jnp.sin(angt)])
    tw = jnp.broadcast_to(tw[..., None], tw.shape + (LANES,))
    f = lambda m: jnp.asarray(np.ascontiguousarray(m), F32).astype(BF16)
    return dict(first=f(first), first_real=f(first_real), last=f(last), inner=f(inner), inner_inv=f(inner_inv),
                tw=tw, n1=n1, inv_scale=1.0 / big)


def _left_matmul_kernel(*refs, mode, inv_scale):
    if mode == "first":
        m_ref, x_ref, o_ref = refs
        o_ref[...] = _bdot(m_ref[...], x_ref[0]).astype(BF16)
        return
    if mode == "last":
        m3_ref, b_ref, g_ref, v_ref, bias_ref, z_ref = refs
    else:
        m3_ref, b_ref, g_ref, v_ref, bias_ref, m1_ref, z_ref, a_ref = refs
    y = _bdot(m3_ref[...], b_ref[...]) * inv_scale
    z = g_ref[0] * (y + bias_ref[...] * v_ref[0])
    z_ref[0] = z
    if mode == "last_first":
        a_ref[...] = _bdot(m1_ref[...], z).astype(BF16)


DFT_COLS = 2048


def _dft_first(m1, x3, part):
    _, rows, wtot = x3.shape
    wt = DFT_COLS
    return pl.pallas_call(
        functools.partial(_left_matmul_kernel, mode="first", inv_scale=None),
        out_shape=jax.ShapeDtypeStruct((m1.shape[0], wtot), BF16),
        grid=(wtot // wt,),
        in_specs=[pl.BlockSpec(m1.shape, lambda j: (0, 0)), pl.BlockSpec((1, rows, wt), lambda j: (part, 0, j))],
        out_specs=pl.BlockSpec((m1.shape[0], wt), lambda j: (0, j)),
        compiler_params=_cparams(("parallel",)),
        name="dft_outer_forward",
    )(m1, x3)


def _dft_last(m3, b2, gate3, gate_part, v3, v_part, bias_flat, inv_scale, m1=None):
    _, rows, wtot = gate3.shape
    wt = DFT_COLS
    col = lambda r: pl.BlockSpec((r, wt), lambda j: (0, j))
    part = lambda p: pl.BlockSpec((1, rows, wt), lambda j: (p, 0, j))
    const = lambda a: pl.BlockSpec(a.shape, lambda j: (0, 0))
    z_shape = jax.ShapeDtypeStruct((1, rows, wtot), F32)
    in_specs = [const(m3), col(b2.shape[0]), part(gate_part), part(v_part), pl.BlockSpec((1, wt), lambda j: (0, 0))]
    if m1 is None:
        return pl.pallas_call(
            functools.partial(_left_matmul_kernel, mode="last", inv_scale=inv_scale),
            out_shape=z_shape,
            grid=(wtot // wt,),
            in_specs=in_specs,
            out_specs=part(0),
            compiler_params=_cparams(("parallel",)),
            name="dft_outer_inverse",
        )(m3, b2, gate3, v3, bias_flat)
    return pl.pallas_call(
        functools.partial(_left_matmul_kernel, mode="last_first", inv_scale=inv_scale),
        out_shape=(z_shape, jax.ShapeDtypeStruct((m1.shape[0], wtot), BF16)),
        grid=(wtot // wt,),
        in_specs=in_specs + [const(m1)],
        out_specs=(part(0), col(m1.shape[0])),
        compiler_params=_cparams(("parallel",)),
        name="dft_outer_inverse_forward",
    )(m3, b2, gate3, v3, bias_flat, m1)


def _dft_mid_kernel(a_ref, tw_ref, g_ref, *rest, slabs, with_filter):
    if with_filter:
        h_ref, gi_ref, o_ref = rest
    else:
        (o_ref,) = rest
    nl = a_ref.shape[3] // LANES
    for j in range(slabs):
        twr = jnp.concatenate([tw_ref[0, j]] * nl, axis=1)
        twi = jnp.concatenate([tw_ref[1, j]] * nl, axis=1)
        ar, ai = a_ref[0, j].astype(F32), a_ref[1, j].astype(F32)
        br = ar * twr - ai * twi
        bi = ar * twi + ai * twr
        x = _bdot(g_ref[...], jnp.concatenate([br.astype(BF16), bi.astype(BF16)], axis=0))
        if not with_filter:
            o_ref[0, j] = x[:DFT_INNER].astype(BF16)
            o_ref[1, j] = x[DFT_INNER:].astype(BF16)
            continue
        xr, xi = x[:DFT_INNER], x[DFT_INNER:]
        hr, hi = h_ref[0, j].astype(F32), h_ref[1, j].astype(F32)
        yr = xr * hr - xi * hi
        yi = xr * hi + xi * hr
        zz = _bdot(gi_ref[...], jnp.concatenate([yr.astype(BF16), yi.astype(BF16)], axis=0))
        zr, zi = zz[:DFT_INNER], zz[DFT_INNER:]
        o_ref[0, j] = (zr * twr + zi * twi).astype(BF16)
        o_ref[1, j] = (zi * twr - zr * twi).astype(BF16)


def _dft_mid(a4, tw, inner, spec=None, inner_inv=None, order=0):
    _, n1, _, c = a4.shape
    slabs = 8 if n1 % 8 == 0 else n1
    blk = lambda w: pl.BlockSpec((2, slabs, DFT_INNER, w), lambda j: (0, j, 0, 0))
    const = lambda a: pl.BlockSpec(a.shape, lambda j: (0, 0))
    in_specs = [blk(c), blk(LANES), const(inner)]
    args = [a4, tw, inner]
    if spec is not None:
        in_specs += [pl.BlockSpec((2, slabs, DFT_INNER, c), lambda j: (0, j, 0, order)), const(inner_inv)]
        args += [spec, inner_inv]
    return pl.pallas_call(
        functools.partial(_dft_mid_kernel, slabs=slabs, with_filter=spec is not None),
        out_shape=jax.ShapeDtypeStruct(a4.shape, BF16),
        grid=(n1 // slabs,),
        in_specs=in_specs,
        out_specs=blk(c),
        compiler_params=_cparams(("parallel",)),
        name="dft_inner_filter" if spec is not None else "dft_inner_forward",
    )(*args)


def _hyena_latent(vx, filt_a, bias, consts):
    _, nb, n, c = vx.shape
    assert nb == 2, "the two samples are carried as the real and imaginary part of one transform"
    n1 = consts["n1"]
    wflat = DFT_INNER * c
    flat = vx.reshape(3, nb * n1 // 2, wflat)
    bias_flat = jnp.tile(bias, (1, DFT_COLS // c))
    tw, inner, inner_inv, scale = consts["tw"], consts["inner"], consts["inner_inv"], consts["inv_scale"]

    spec = _dft_mid(filt_a.reshape(2, n1, DFT_INNER, HYENA_ORDER * c), tw, inner)

    a = _dft_first(consts["first"], flat, 0).reshape(2, n1, DFT_INNER, c)
    bm = _dft_mid(a, tw, inner, spec, inner_inv, order=0)
    z, a = _dft_last(consts["last"], bm.reshape(2 * n1, wflat), flat, 1, flat, 0, bias_flat[0:1], scale,
                     consts["first"])
    bm = _dft_mid(a.reshape(2, n1, DFT_INNER, c), tw, inner, spec, inner_inv, order=1)
    y = _dft_last(consts["last"], bm.reshape(2 * n1, wflat), flat, 2, z, 0, bias_flat[1:2], scale)
    return y.reshape(nb, n, c)


def _hyena_ctx_kernel(u_ref, h_ref, fs_ref, ff_ref, fi_ref, bias_ref, o_ref, *, n):
    big = 2 * n
    c = HYENA_WIDTH
    spec = _hdot(fs_ref[...], h_ref[...])
    u = u_ref[...]
    v, x1, x2 = u[:, 0:c], u[:, c:2 * c], u[:, 2 * c:3 * c]

    def conv(zin, order):
        x = _hdot(ff_ref[...], zin)
        xr, xi = x[:big], x[big:]
        hr, hi = spec[:big, order * c:(order + 1) * c], spec[big:, order * c:(order + 1) * c]
        y = jnp.concatenate([xr * hr - xi * hi, xr * hi + xi * hr], axis=0)
        return _hdot(fi_ref[...], y)

    bias = bias_ref[...]
    z = x1 * (conv(v, 0) + bias[0:1] * v)
    o_ref[...] = x2 * (conv(z, 1) + bias[1:2] * z)


def _ctx_dft_consts(n):
    big = 2 * n
    k = np.arange(big)[:, None]
    t = np.arange(big)[None, :]
    ang = -2.0 * np.pi * ((k * t) % big) / big
    fr, fi = np.cos(ang), np.sin(ang)
    spec_m = np.concatenate([fr, fi], axis=0)
    fwd = np.block([[fr[:, :n], -fi[:, :n]], [fi[:, :n], fr[:, :n]]])
    inv = np.block([[fr.T[:n, :], fi.T[:n, :]], [-fi.T[:n, :], fr.T[:n, :]]]) / big
    f = lambda m: jnp.asarray(np.ascontiguousarray(m), F32)
    return f(spec_m), f(fwd), f(inv)


def _hyena_ctx(u_ctx, h_circ, bias, cconsts):
    nb, n, w = u_ctx.shape
    assert nb == 2
    fs, ff, fi = cconsts
    full = lambda a: pl.BlockSpec(a.shape, lambda i: (0,) * a.ndim)
    u2 = u_ctx.reshape(nb * n, w)
    out = pl.pallas_call(
        functools.partial(_hyena_ctx_kernel, n=n),
        out_shape=jax.ShapeDtypeStruct((nb * n, HYENA_WIDTH), F32),
        grid=(1,),
        in_specs=[full(u2), full(h_circ), full(fs), full(ff), full(fi), full(bias)],
        out_specs=pl.BlockSpec((nb * n, HYENA_WIDTH), lambda i: (0, 0)),
        compiler_params=_cparams(("arbitrary",)),
        name="hyena_context",
    )(u2, h_circ, fs, ff, fi, bias)
    return out.reshape(nb, n, HYENA_WIDTH)


def _gla_kernel(xf_ref, xb_ref, of_ref, ob_ref, st_ref):
    s = pl.program_id(0)

    @pl.when(s == 0)
    def _():
        st_ref[...] = jnp.zeros(st_ref.shape, F32)

    c = GLA_CHUNK
    kw, vw = GLA_K_WIDTH, GLA_V_WIDTH
    ri3 = lax.broadcasted_iota(jnp.int32, (c, 3 * c), 0)
    ci3 = lax.broadcasted_iota(jnp.int32, (c, 3 * c), 1) % c
    kr = lax.broadcasted_iota(jnp.int32, (GLA_HEADS * c, kw), 0) // c
    kc = lax.broadcasted_iota(jnp.int32, (GLA_HEADS * c, kw), 1) // GLA_DK
    mask_k = kr == kc
    vr = lax.broadcasted_iota(jnp.int32, (GLA_HEADS * c, vw), 0) // c
    vc = lax.broadcasted_iota(jnp.int32, (GLA_HEADS * c, vw), 1) // GLA_DV
    mask_v = vr == vc
    sr = lax.broadcasted_iota(jnp.int32, (vw, kw), 0) // GLA_DV
    sc = lax.broadcasted_iota(jnp.int32, (vw, kw), 1) // GLA_DK
    mask_s = sr == sc
    qi = lax.broadcasted_iota(jnp.int32, (c, GLA_HEADS * c), 0)
    kj = lax.broadcasted_iota(jnp.int32, (c, GLA_HEADS * c), 1) % c

    def chunk(x_ref, o_ref, d, b, ch):
        backward = d == 1
        tri3 = ((ri3 <= ci3) if backward else (ri3 >= ci3)).astype(BF16)
        causal = (qi <= kj) if backward else (qi >= kj)
        goff = 2 * kw + vw + (kw if backward else 0)
        r0 = ch * c
        q = x_ref[b, r0:r0 + c, 0:kw]
        k = x_ref[b, r0:r0 + c, kw:2 * kw]
        v = x_ref[b, r0:r0 + c, 2 * kw:2 * kw + vw]
        g = x_ref[b, r0:r0 + c, goff:goff + kw]
        g_hi = g.astype(BF16)
        g_mid = (g - g_hi.astype(F32)).astype(BF16)
        g_lo = (g - g_hi.astype(F32) - g_mid.astype(F32)).astype(BF16)
        gc = jnp.dot(tri3, jnp.concatenate([g_hi, g_mid, g_lo], axis=0), preferred_element_type=F32)
        g_last = gc[0:1] if backward else gc[c - 1:c]
        q_dec = q * jnp.exp(gc)
        k_dec = k * jnp.exp(-gc)
        k_tail = k * jnp.exp(g_last - gc)
        k_blk = jnp.where(mask_k, jnp.concatenate([k_dec] * GLA_HEADS, axis=0), 0.0).astype(BF16)
        v_bf = v.astype(BF16)
        v_blk = jnp.where(mask_v, jnp.concatenate([v_bf] * GLA_HEADS, axis=0), jnp.zeros((), BF16))
        q_bf = q_dec.astype(BF16)
        scores = lax.dot_general(q_bf, k_blk, (((1,), (1,)), ((), ())), preferred_element_type=F32)
        scores = jnp.where(causal, scores, 0.0)
        st = st_ref[d, b]
        o = jnp.dot(scores.astype(BF16), v_blk, preferred_element_type=F32)
        o = o + lax.dot_general(q_bf, st.astype(BF16), (((1,), (1,)), ((), ())), preferred_element_type=F32)
        o_ref[b, r0:r0 + c, :] = o
        kv_t = lax.dot_general(v_bf, k_tail.astype(BF16), (((0,), (0,)), ((), ())), preferred_element_type=F32)
        st_ref[d, b] = st * jnp.exp(g_last) + jnp.where(mask_s, kv_t, 0.0)

    n_chunks = xf_ref.shape[1] // c
    for u in range(n_chunks):
        for b in range(xf_ref.shape[0]):
            chunk(xf_ref, of_ref, 0, b, u)
            chunk(xb_ref, ob_ref, 1, b, n_chunks - 1 - u)


def _gla(gla_in, *, ctx_len):
    nb, t, w = gla_in.shape
    tb = TOK_TILE
    nblk = t // tb
    cblk = ctx_len // tb
    fwd = lambda s: (0, s, 0)
    bwd = lambda s: (0, jnp.where(s < cblk, cblk - 1 - s, nblk - 1 - (s - cblk)), 0)
    out = jax.ShapeDtypeStruct((nb, t, GLA_V_WIDTH), F32)
    return pl.pallas_call(
        _gla_kernel,
        out_shape=(out, out),
        grid=(nblk,),
        in_specs=[pl.BlockSpec((nb, tb, w), fwd), pl.BlockSpec((nb, tb, w), bwd)],
        out_specs=(pl.BlockSpec((nb, tb, GLA_V_WIDTH), fwd), pl.BlockSpec((nb, tb, GLA_V_WIDTH), bwd)),
        scratch_shapes=[pltpu.VMEM((2, nb, GLA_V_WIDTH, GLA_K_WIDTH), F32)],
        compiler_params=_cparams(("arbitrary",)),
        name="gla_scans",
    )(gla_in, gla_in)


def _out_kernel(h_ref, mod_ref, at_ref, hy_ref, gf_ref, gb_ref, gr_ref, gn_ref, gmat_ref, w_ref, o_ref, *,
                n_batch, ctx_tiles, tile_off, d_model):
    b, i = pl.program_id(0), pl.program_id(1)
    mod = _mod_row(mod_ref, i + tile_off < ctx_tiles, b, n_batch)
    gate = mod[:, 2 * d_model:3 * d_model]
    gmat = gmat_ref[...]
    gn = gn_ref[...]
    a0, a1 = ATTN_WIDTH, ATTN_WIDTH + HYENA_WIDTH
    ya = _head_rms(at_ref[0], gn[:, :a0], gmat)
    yh = _head_rms(hy_ref[0], gn[:, a0:a1], gmat)
    yg = _head_rms(gf_ref[0] + gb_ref[0], gn[:, a1:], gmat) * _silu(gr_ref[0])
    y = jnp.concatenate([ya, yh, yg], axis=1)
    o_ref[0] = h_ref[0] + gate * _bdot(y, w_ref[...])


def _out_projection(h, mod, attn, hy, gla_f, gla_b, gr, gn, gmat, w_out, *, ctx_len, skip_ctx):
    nb, t, d = h.shape
    tb = TOK_TILE
    off = ctx_len // tb if skip_ctx else 0
    kern = functools.partial(_out_kernel, n_batch=nb, ctx_tiles=ctx_len // tb, tile_off=off, d_model=d)
    const = lambda a: pl.BlockSpec(a.shape, lambda b, i: (0,) * a.ndim)
    tok = lambda w: pl.BlockSpec((1, tb, w), lambda b, i: (b, i + off, 0))
    return pl.pallas_call(
        kern,
        out_shape=jax.ShapeDtypeStruct((nb, t - off * tb, d), F32),
        grid=(nb, t // tb - off),
        in_specs=[tok(d), const(mod), tok(ATTN_WIDTH), tok(HYENA_WIDTH), tok(GLA_V_WIDTH), tok(GLA_V_WIDTH),
                  tok(GLA_V_WIDTH), const(gn), const(gmat), const(w_out)],
        out_specs=pl.BlockSpec((1, tb, d), lambda b, i: (b, i, 0)),
        compiler_params=_cparams(("parallel", "parallel")),
        name="out_projection",
    )(h, mod, attn, hy, gla_f, gla_b, gr, gn, gmat, w_out)


def _ffn_kernel(h_ref, mod_ref, g2_ref, w1_ref, w3_ref, w2_ref, fg_ref, o_ref, *, n_batch, ctx_tiles, tile_off,
                d_model, final):
    b, i = pl.program_id(0), pl.program_id(1)
    mod = _mod_row(mod_ref, i + tile_off < ctx_tiles, b, n_batch)
    shift, scale, gate = (mod[:, 3 * d_model:4 * d_model], mod[:, 4 * d_model:5 * d_model],
                          mod[:, 5 * d_model:6 * d_model])
    x = h_ref[0]
    xn = x * lax.rsqrt(jnp.mean(jnp.square(x), axis=-1, keepdims=True) + NORM_EPS) * g2_ref[...]
    u = (xn * (1.0 + scale) + shift).astype(BF16)
    a = jnp.dot(u, w1_ref[...], preferred_element_type=F32)
    c = jnp.dot(u, w3_ref[...], preferred_element_type=F32)
    hid = (_silu(a) * c).astype(BF16)
    y = x + gate * jnp.dot(hid, w2_ref[...], preferred_element_type=F32)
    if final:
        y = y * lax.rsqrt(jnp.mean(jnp.square(y), axis=-1, keepdims=True) + NORM_EPS) * fg_ref[...]
    o_ref[0] = y


def _ffn(h, mod, g2, w1, w3, w2, fg, *, ctx_len, lat_only, final):
    nb, t, d = h.shape
    tb = TOK_TILE
    off = ctx_len // tb if lat_only else 0
    kern = functools.partial(_ffn_kernel, n_batch=nb, ctx_tiles=ctx_len // tb, tile_off=off, d_model=d, final=final)
    const = lambda a: pl.BlockSpec(a.shape, lambda b, i: (0,) * a.ndim)
    tok = pl.BlockSpec((1, tb, d), lambda b, i: (b, i, 0))
    return pl.pallas_call(
        kern,
        out_shape=jax.ShapeDtypeStruct((nb, t, d), F32),
        grid=(nb, t // tb),
        in_specs=[tok, const(mod), const(g2), const(w1), const(w3), const(w2), const(fg)],
        out_specs=tok,
        compiler_params=_cparams(("parallel", "parallel")),
        name="swiglu_ffn",
    )(h, mod, g2, w1, w3, w2, fg)


def _rope_tables(n_lat, ctx_len):
    rows = n_lat // GRID_W
    row = jnp.broadcast_to(jnp.arange(rows, dtype=F32)[:, None], (rows, GRID_W)).reshape(-1)
    col = jnp.broadcast_to(jnp.arange(GRID_W, dtype=F32)[None, :], (rows, GRID_W)).reshape(-1)
    n_freq = HEAD_DIM // 4
    inv_freq = jnp.power(ROPE_THETA, -jnp.arange(n_freq, dtype=F32) / n_freq)
    ang = jnp.concatenate([row[:, None] * inv_freq, col[:, None] * inv_freq], axis=-1)
    cos, sin = jnp.cos(ang), jnp.sin(ang)
    cos = jnp.concatenate([jnp.ones((ctx_len, HEAD_DIM // 2), F32), cos], axis=0)
    sin = jnp.concatenate([jnp.zeros((ctx_len, HEAD_DIM // 2), F32), sin], axis=0)
    cosf = jnp.tile(cos, (1, 2 * LANES // HEAD_DIM))
    sinf = jnp.tile(jnp.concatenate([-sin, sin], axis=1), (1, LANES // HEAD_DIM))
    return cosf, sinf


def kernel(x, c, ctx, c_ctx, ada_w, ada_b, norm1_g, w_in, q_norm_g, k_norm_g, hy_conv_w, hy_conv_b, filt_w1, filt_b1, filt_w2, filt_b2, filt_w3, filt_freq, hy_bias, gla_gate_w, gla_gate_b, out_norm_g, w_out, norm2_g, ffn_w1, ffn_w3, ffn_w2, final_norm_g):
    nb, n_lat, d = x.shape
    ctx_len = ctx.shape[1]
    depth = w_in.shape[0]
    assert nb == 2 and ctx_len % TOK_TILE == 0 and n_lat % TOK_TILE == 0 and n_lat % DFT_INNER == 0

    h = jnp.concatenate([ctx, x], axis=1)
    cond = jnp.concatenate([c, c_ctx[None, :], jnp.zeros((8 - nb - 1, d), F32)], axis=0)
    mods = _modulation(cond, ada_w, ada_b)

    cosf, sinf = _rope_tables(n_lat, ctx_len)
    lane = np.arange(LANES)
    gmat = jnp.asarray(np.tile((lane[:, None] // HEAD_DIM) == (lane[None, :] // HEAD_DIM), (2, 1)), BF16)
    consts = _dft_consts(n_lat)
    cconsts = _ctx_dft_consts(ctx_len)
    zc_ctx = _filter_features(ctx_len)
    deltas = jnp.abs(jnp.linspace(HYENA_MIN_DECAY, HYENA_MAX_DECAY, HYENA_WIDTH, dtype=F32))
    deltas = jnp.tile(deltas[None, :], (1, HYENA_ORDER))
    n1 = consts["n1"]
    rows_sa = (DFT_INNER * jnp.arange(n1, dtype=jnp.int32)[None, :] + jnp.arange(DFT_INNER, dtype=jnp.int32)[:, None])
    zt_lat = _circular_features(rows_sa, n_lat).transpose(1, 0, 2)
    t_hi, t_mid, t_lo = _split3(zt_lat[:, 0, :])
    d_hi, d_mid, d_lo = _split3(deltas[0])
    zt_lat = jnp.concatenate([zt_lat] + [p[:, None, :] for p in (t_hi, t_hi, t_hi, t_mid, t_mid, t_lo)], axis=1)
    n_feat = zt_lat.shape[1]
    zt_lat = jnp.pad(zt_lat, ((0, 0), (0, FEAT_PAD - n_feat), (0, 0)))
    decay_rows = jnp.zeros((FEAT_PAD, deltas.shape[1]), F32).at[n_feat - 6:n_feat].set(
        jnp.stack([d_hi, d_mid, d_lo, d_hi, d_mid, d_hi])).astype(BF16)
    lanes_n1 = lambda vec: jnp.broadcast_to(vec[:, None], (vec.shape[0], n1))

    w_in_b = jnp.pad(w_in, ((0, 0), (0, 0), (0, IN_PAD - w_in.shape[2]))).astype(BF16)
    w_out_b = w_out.astype(BF16)
    w1_b, w3_b, w2_b = ffn_w1.astype(BF16), ffn_w3.astype(BF16), ffn_w2.astype(BF16)
    fg = final_norm_g.reshape(1, d)

    out = None
    for l in range(depth):
        last = l == depth - 1
        mod = mods[l]
        wg = jnp.zeros((LANES, 2 * GLA_K_WIDTH), F32)
        wg = wg.at[0:GLA_GATE_RANK, 0:GLA_K_WIDTH].set(gla_gate_w[l, 0])
        wg = wg.at[GLA_GATE_RANK:2 * GLA_GATE_RANK, GLA_K_WIDTH:].set(gla_gate_w[l, 1])
        bg = gla_gate_b[l].reshape(1, 2 * GLA_K_WIDTH)
        qg = jnp.tile(q_norm_g[l][None, :], (1, ATTN_HEADS))
        kg = jnp.tile(k_norm_g[l][None, :], (1, ATTN_KV_HEADS))

        q, k, v, hy, gla_in, gr = _in_projection(h, mod, norm1_g[l].reshape(1, d), w_in_b[l], qg, kg, cosf, sinf,
                                                 gmat, wg.astype(BF16), bg, ctx_len=ctx_len)
        attn = _attention(q, k, v, ctx_len=ctx_len)

        vx, u_ctx = _short_conv(hy, hy_conv_w[l], hy_conv_b[l], ctx_len=ctx_len)
        w1p = jnp.pad(filt_w1[l], ((0, LANES - filt_w1.shape[1]), (0, 0)))
        w3r = filt_w3[l].reshape(FILTER_HIDDEN, HYENA_ORDER, 2, HYENA_WIDTH).transpose(2, 0, 1, 3)
        w3r = w3r.reshape(2, FILTER_HIDDEN, HYENA_ORDER * HYENA_WIDTH)
        fargs = (w1p, filt_b1[l][None, :], filt_w2[l], filt_b2[l][None, :], w3r, filt_freq[l][None, :], deltas)
        w1t = jnp.pad(filt_w1[l].T, ((0, 0), (0, FEAT_PAD - filt_w1.shape[1])))
        w3_hi, w3_lo, _ = _split3(w3r)
        w3_stack = jnp.concatenate([w3_hi, w3_lo, w3_hi], axis=1).reshape(-1, w3r.shape[2]).astype(BF16)
        filt_a = _filter_outer_dft(zt_lat, w1t, lanes_n1(filt_b1[l]), filt_w2[l].T, lanes_n1(filt_b2[l]), w3_stack,
                                   lanes_n1(filt_freq[l]), decay_rows, consts["first_real"])
        hy_lat = _hyena_latent(vx, filt_a, hy_bias[l], consts)
        if not last:
            h_ctx = _hyena_filter(zc_ctx, *fargs, n=ctx_len)
            hy_ctx = _hyena_ctx(u_ctx, h_ctx, hy_bias[l], cconsts)
        else:
            hy_ctx = jnp.zeros((nb, ctx_len, HYENA_WIDTH), F32)
        hy_out = jnp.concatenate([hy_ctx, hy_lat], axis=1)

        gla_f, gla_b = _gla(gla_in, ctx_len=ctx_len)

        h = _out_projection(h, mod, attn, hy_out, gla_f, gla_b, gr, out_norm_g[l].reshape(1, -1), gmat, w_out_b[l],
                            ctx_len=ctx_len, skip_ctx=last)
        h = _ffn(h, mod, norm2_g[l].reshape(1, d), w1_b[l], w3_b[l], w2_b[l], fg, ctx_len=ctx_len, lat_only=last,
                 final=last)
    return h
```

```python
import functools
import math

import numpy as np
import jax
import jax.numpy as jnp
from jax import lax
from jax.experimental import pallas as pl
from jax.experimental.pallas import tpu as pltpu

F32 = jnp.float32
BF16 = jnp.bfloat16
HI = lax.Precision.HIGHEST

LANES = 128
VMEM_LIMIT = 56 * 1024 * 1024

HEAD_DIM = 64
ATTN_HEADS = 8
ATTN_KV_HEADS = 2
ATTN_WIDTH = ATTN_HEADS * HEAD_DIM
KV_WIDTH = ATTN_KV_HEADS * HEAD_DIM
HYENA_WIDTH = 256
HYENA_ORDER = 2
FILTER_BANDS = 16
FILTER_HIDDEN = 64
GLA_HEADS = 4
GLA_DK = 32
GLA_DV = 64
GLA_K_WIDTH = GLA_HEADS * GLA_DK
GLA_V_WIDTH = GLA_HEADS * GLA_DV
GLA_GATE_RANK = 16
GLA_GATE_TAU = 16.0
GLA_CHUNK = 64
GRID_W = 64
ROPE_THETA = 10000.0
N_MOD = 6
NORM_EPS = 1e-6
HYENA_TARGET = 1e-2
HYENA_MIN_DECAY = math.log(HYENA_TARGET) / 1.5
HYENA_MAX_DECAY = math.log(HYENA_TARGET) / 0.3

TOK_TILE = 256
DFT_INNER = 128
IN_PAD = 2432

_OQ, _OK, _OV, _OHY = 0, 512, 640, 768
_OGQ, _OGK, _OGV, _OGR, _OGA = 1536, 1664, 1792, 2048, 2304


def _cparams(sem, vmem=VMEM_LIMIT):
    return pltpu.CompilerParams(dimension_semantics=sem, vmem_limit_bytes=vmem)


def _bdot(a, b):
    return jnp.dot(a.astype(BF16), b.astype(BF16), preferred_element_type=F32)


def _hdot(a, b):
    return jnp.dot(a, b, preferred_element_type=F32, precision=HI)


def _group_mean_square(x, gmat):
    outs = []
    for j in range(x.shape[1] // LANES):
        sq = jnp.square(x[:, j * LANES:(j + 1) * LANES])
        hi = sq.astype(BF16)
        lo = (sq - hi.astype(F32)).astype(BF16)
        s = jnp.dot(jnp.concatenate([hi, lo], axis=1), gmat, preferred_element_type=F32)
        outs.append(s * (1.0 / HEAD_DIM))
    return outs[0] if len(outs) == 1 else jnp.concatenate(outs, axis=1)


def _head_rms(x, gain, gmat):
    ms = _group_mean_square(x, gmat)
    return x * lax.rsqrt(ms + NORM_EPS) * gain


def _mod_row(mod_ref, is_ctx, b, n_batch):
    row = jnp.where(is_ctx, n_batch, b)
    return mod_ref[pl.ds(row, 1), :]


def _silu(x):
    return x * (1.0 / (1.0 + jnp.exp(-x)))


def _split3(x):
    hi = x.astype(BF16).astype(F32)
    mid = (x - hi).astype(BF16).astype(F32)
    lo = (x - hi - mid).astype(BF16).astype(F32)
    return hi, mid, lo


def _mod_kernel(c_ref, w_ref, b_ref, o_ref):
    s = _silu(c_ref[...])
    o_ref[0] = _bdot(s, w_ref[0]) + b_ref[0]


def _modulation(cond_rows, ada_w, ada_b):
    depth, d, w = ada_w.shape
    ct = 1536
    return pl.pallas_call(
        _mod_kernel,
        out_shape=jax.ShapeDtypeStruct((depth, 8, w), F32),
        grid=(depth, w // ct),
        in_specs=[pl.BlockSpec((8, d), lambda l, j: (0, 0)),
                  pl.BlockSpec((1, d, ct), lambda l, j: (l, 0, j)),
                  pl.BlockSpec((1, 1, ct), lambda l, j: (l, 0, j))],
        out_specs=pl.BlockSpec((1, 8, ct), lambda l, j: (l, 0, j)),
        compiler_params=_cparams(("arbitrary", "arbitrary")),
        name="adaln_modulation",
    )(cond_rows, ada_w, ada_b.reshape(depth, 1, w))


def _in_kernel(h_ref, mod_ref, g1_ref, w_ref, qg_ref, kg_ref, cos_ref, sin_ref, gmat_ref, wg_ref, bg_ref,
               q_ref, k_ref, v_ref, hy_ref, gla_ref, gr_ref, *, n_batch, ctx_tiles, d_model):
    b, i = pl.program_id(0), pl.program_id(1)
    mod = _mod_row(mod_ref, i < ctx_tiles, b, n_batch)
    shift, scale = mod[:, 0:d_model], mod[:, d_model:2 * d_model]
    x = h_ref[0]
    xn = x * lax.rsqrt(jnp.mean(jnp.square(x), axis=-1, keepdims=True) + NORM_EPS) * g1_ref[...]
    u = xn * (1.0 + scale) + shift
    proj = _bdot(u, w_ref[...])

    gmat = gmat_ref[...]
    cosf, sinf = cos_ref[...], sin_ref[...]
    lane = lax.broadcasted_iota(jnp.int32, (x.shape[0], LANES), 1)
    first_half = (lane % HEAD_DIM) < (HEAD_DIM // 2)
    low_head = lane < HEAD_DIM

    def rope(t):
        partner = jnp.where(first_half, pltpu.roll(t, LANES - HEAD_DIM // 2, 1), pltpu.roll(t, HEAD_DIM // 2, 1))
        return t * cosf + partner * sinf

    qn = _head_rms(proj[:, _OQ:_OQ + ATTN_WIDTH], qg_ref[...], gmat)
    zero = jnp.zeros((x.shape[0], LANES), F32)
    for p in range(ATTN_HEADS // 2):
        t = rope(qn[:, p * LANES:(p + 1) * LANES]) * (HEAD_DIM ** -0.5 * math.log2(math.e))
        tr = pltpu.roll(t, HEAD_DIM, 1)
        if 2 * p < ATTN_HEADS // ATTN_KV_HEADS:
            q_ref[0, 2 * p] = jnp.where(low_head, t, zero).astype(BF16)
            q_ref[0, 2 * p + 1] = jnp.where(low_head, tr, zero).astype(BF16)
        else:
            q_ref[0, 2 * p] = jnp.where(low_head, zero, tr).astype(BF16)
            q_ref[0, 2 * p + 1] = jnp.where(low_head, zero, t).astype(BF16)
    kn = _head_rms(proj[:, _OK:_OK + KV_WIDTH], kg_ref[...], gmat)
    k_ref[0] = rope(kn).astype(BF16)
    v_ref[0] = proj[:, _OV:_OV + KV_WIDTH].astype(BF16)
    hy_ref[0] = proj[:, _OHY:_OHY + 3 * HYENA_WIDTH]

    zg = _bdot(proj[:, _OGA:_OGA + LANES], wg_ref[...]) + bg_ref[...]
    log_gate = (jnp.minimum(zg, 0.0) - jnp.log(1.0 + jnp.exp(-jnp.abs(zg)))) * (1.0 / GLA_GATE_TAU)
    gla_ref[0, :, 0:GLA_K_WIDTH] = proj[:, _OGQ:_OGQ + GLA_K_WIDTH] * (GLA_DK ** -0.5)
    gla_ref[0, :, GLA_K_WIDTH:2 * GLA_K_WIDTH + GLA_V_WIDTH] = proj[:, _OGK:_OGK + GLA_K_WIDTH + GLA_V_WIDTH]
    gla_ref[0, :, 2 * GLA_K_WIDTH + GLA_V_WIDTH:] = log_gate
    gr_ref[0] = proj[:, _OGR:_OGR + GLA_V_WIDTH].astype(BF16)


def _in_projection(h, mod, g1, w_in, qg, kg, cosf, sinf, gmat, wg, bg, *, ctx_len):
    nb, t, d = h.shape
    tb = TOK_TILE
    kern = functools.partial(_in_kernel, n_batch=nb, ctx_tiles=ctx_len // tb, d_model=d)
    const = lambda shape: pl.BlockSpec(shape, lambda b, i: (0,) * len(shape))
    tok = lambda w: pl.BlockSpec((1, tb, w), lambda b, i: (b, i, 0))
    gla_w = 2 * GLA_K_WIDTH + GLA_V_WIDTH + 2 * GLA_K_WIDTH
    return pl.pallas_call(
        kern,
        out_shape=(jax.ShapeDtypeStruct((nb, ATTN_HEADS, t, LANES), BF16),
                   jax.ShapeDtypeStruct((nb, t, KV_WIDTH), BF16),
                   jax.ShapeDtypeStruct((nb, t, KV_WIDTH), BF16),
                   jax.ShapeDtypeStruct((nb, t, 3 * HYENA_WIDTH), F32),
                   jax.ShapeDtypeStruct((nb, t, gla_w), F32),
                   jax.ShapeDtypeStruct((nb, t, GLA_V_WIDTH), BF16)),
        grid=(nb, t // tb),
        in_specs=[tok(d), const(mod.shape), const(g1.shape), const(w_in.shape), const(qg.shape), const(kg.shape),
                  pl.BlockSpec((tb, LANES), lambda b, i: (i, 0)), pl.BlockSpec((tb, LANES), lambda b, i: (i, 0)),
                  const(gmat.shape), const(wg.shape), const(bg.shape)],
        out_specs=(pl.BlockSpec((1, ATTN_HEADS, tb, LANES), lambda b, i: (b, 0, i, 0)),
                   tok(KV_WIDTH), tok(KV_WIDTH), tok(3 * HYENA_WIDTH), tok(gla_w), tok(GLA_V_WIDTH)),
        compiler_params=_cparams(("parallel", "parallel")),
        name="in_projection",
    )(h, mod, g1, w_in, qg, kg, cosf, sinf, gmat, wg, bg)


KV_UNROLL = 4


def _attn_kernel(q_ref, k_ref, v_ref, o_ref, m_sc, acc_sc, s_sc, mx_sc, p_sc, *, ctx_len, ctx_tiles, kv_tile,
                 n_kv_tiles):
    i = pl.program_id(1)
    tq = q_ref.shape[2]
    per_kv = ATTN_HEADS // ATTN_KV_HEADS
    grp = per_kv * tq
    m_sc[...] = jnp.full(m_sc.shape, -jnp.inf, F32)
    acc_sc[...] = jnp.zeros(acc_sc.shape, F32)

    def scores(start, size, buf):
        q = q_ref[0].reshape(ATTN_HEADS * tq, LANES)
        kc = k_ref[0, pl.ds(start, size), :]
        s = lax.dot_general(q, kc, (((1,), (1,)), ((), ())), preferred_element_type=F32)
        s_sc[buf, :, 0:size] = s
        mx_sc[buf] = jnp.broadcast_to(jnp.max(s, axis=1, keepdims=True), mx_sc.shape[1:])

    def consume(start, size, buf, pbuf):
        vc = v_ref[0, pl.ds(start, size), :]
        low = lax.broadcasted_iota(jnp.int32, vc.shape, 1) < HEAD_DIM
        one = jnp.ones((), BF16)
        v_ones = (jnp.where(low, vc, one), jnp.where(low, one, vc))
        for g in range(ATTN_KV_HEADS):
            alphas = []
            for hh in range(g * per_kv, (g + 1) * per_kv):
                rows = slice(hh * tq, (hh + 1) * tq)
                m_prev = m_sc[rows, :]
                m_new = jnp.maximum(m_prev, mx_sc[buf, rows, :])
                alphas.append(jnp.exp2(m_prev - m_new))
                for j in range(size // LANES):
                    cols = slice(j * LANES, (j + 1) * LANES)
                    p_sc[pbuf, rows, cols] = jnp.exp2(s_sc[buf, rows, cols] - m_new).astype(BF16)
                m_sc[rows, :] = m_new
            rows_g = slice(g * grp, (g + 1) * grp)
            pv = jnp.dot(p_sc[pbuf, rows_g, 0:size], v_ones[g], preferred_element_type=F32)
            acc_sc[rows_g, :] = jnp.concatenate(alphas, axis=0) * acc_sc[rows_g, :] + pv

    @pl.when(i < ctx_tiles)
    def _():
        scores(0, ctx_len, 0)
        consume(0, ctx_len, 0, 0)

    @pl.when(i >= ctx_tiles)
    def _():
        def tile_start(j):
            return pl.multiple_of(j * kv_tile, kv_tile)

        def run(j0, count, tail):
            for u in range(count):
                if not (tail and u + 1 == count):
                    scores(tile_start(j0 + u + 1), kv_tile, (u + 1) % 2)
                consume(tile_start(j0 + u), kv_tile, u % 2, u)

        trips = (n_kv_tiles - 1) // KV_UNROLL
        scores(0, kv_tile, 0)
        lax.fori_loop(0, trips, lambda k, c: (run(k * KV_UNROLL, KV_UNROLL, False), c)[1], 0)
        run(trips * KV_UNROLL, n_kv_tiles - trips * KV_UNROLL, True)

    low_head = lax.broadcasted_iota(jnp.int32, (tq, LANES), 1) < HEAD_DIM
    for p in range(ATTN_HEADS // 2):
        outs = []
        for hh in (2 * p, 2 * p + 1):
            acc = acc_sc[hh * tq:(hh + 1) * tq, :]
            o = acc / pltpu.roll(acc, HEAD_DIM, 1)
            src_low = hh < per_kv
            dst_low = hh % 2 == 0
            outs.append(o if src_low == dst_low else pltpu.roll(o, HEAD_DIM, 1))
        o_ref[0, :, p * LANES:(p + 1) * LANES] = jnp.where(low_head, outs[0], outs[1]).astype(BF16)


def _attention(q, k, v, *, ctx_len):
    nb, nh, t, _ = q.shape
    tq = TOK_TILE
    n_lat = t - ctx_len
    kv_tile = max(w for w in (256, 512, 768, 1024) if t % w == 0)
    assert ctx_len <= kv_tile
    rows = nh * tq
    kern = functools.partial(_attn_kernel, ctx_len=ctx_len, ctx_tiles=ctx_len // tq, kv_tile=kv_tile,
                             n_kv_tiles=t // kv_tile)
    return pl.pallas_call(
        kern,
        out_shape=jax.ShapeDtypeStruct((nb, t, ATTN_WIDTH), BF16),
        grid=(nb, t // tq),
        in_specs=[pl.BlockSpec((1, nh, tq, LANES), lambda b, i: (b, 0, i, 0)),
                  pl.BlockSpec((1, t, KV_WIDTH), lambda b, i: (b, 0, 0)),
                  pl.BlockSpec((1, t, KV_WIDTH), lambda b, i: (b, 0, 0))],
        out_specs=pl.BlockSpec((1, tq, ATTN_WIDTH), lambda b, i: (b, i, 0)),
        scratch_shapes=[pltpu.VMEM((rows, LANES), F32), pltpu.VMEM((rows, LANES), F32),
                        pltpu.VMEM((2, rows, kv_tile), F32), pltpu.VMEM((2, rows, LANES), F32),
                        pltpu.VMEM((KV_UNROLL, rows, kv_tile), BF16)],
        compiler_params=_cparams(("parallel", "arbitrary")),
        name="gqa_attention",
    )(q, k, v)


def _short_conv_kernel(x_ref, w_ref, b_ref, lat_ref, ctx_ref, *, ctx_len):
    x = x_ref[0]
    t = x.shape[0]
    row = lax.broadcasted_iota(jnp.int32, x.shape, 0)
    prev = jnp.where((row == 0) | (row == ctx_len), 0.0, pltpu.roll(x, 1, 0))
    nxt = jnp.where((row == ctx_len - 1) | (row == t - 1), 0.0, pltpu.roll(x, t - 1, 0))
    w = w_ref[...]
    u = prev * w[0:1, :] + x * w[1:2, :] + nxt * w[2:3, :] + b_ref[...]
    ctx_ref[0] = u[:ctx_len]
    lat_ref[0, 0] = u[ctx_len:].astype(BF16)


def _short_conv(hy, conv_w, conv_b, *, ctx_len):
    nb, t, w = hy.shape
    per = HYENA_WIDTH // LANES
    return pl.pallas_call(
        functools.partial(_short_conv_kernel, ctx_len=ctx_len),
        out_shape=(jax.ShapeDtypeStruct((3, nb, t - ctx_len, HYENA_WIDTH), BF16),
                   jax.ShapeDtypeStruct((nb, ctx_len, w), F32)),
        grid=(nb, w // LANES),
        in_specs=[pl.BlockSpec((1, t, LANES), lambda b, j: (b, 0, j)),
                  pl.BlockSpec((3, LANES), lambda b, j: (0, j)),
                  pl.BlockSpec((1, LANES), lambda b, j: (0, j))],
        out_specs=(pl.BlockSpec((1, 1, t - ctx_len, LANES), lambda b, j: (j // per, b, 0, j % per)),
                   pl.BlockSpec((1, ctx_len, LANES), lambda b, j: (b, 0, j))),
        compiler_params=_cparams(("parallel", "parallel")),
        name="hyena_short_conv",
    )(hy, conv_w, conv_b.reshape(1, w))


def _filter_kernel(z_ref, w1_ref, b1_ref, w2_ref, b2_ref, w3_ref, fr_ref, dl_ref, o_ref, *, n):
    r = pl.program_id(0)
    z = z_ref[...]
    fr = fr_ref[...]
    a = jnp.sin(fr * (_hdot(z, w1_ref[...]) + b1_ref[...]))
    a = jnp.sin(fr * (_hdot(a, w2_ref[...]) + b2_ref[...]))
    h = _hdot(a, w3_ref[0])
    window = jnp.exp(-z[:, 0:1] * dl_ref[...])
    row = r * z.shape[0] + lax.broadcasted_iota(jnp.int32, h.shape, 0)
    o_ref[...] = jnp.where(row == n, 0.0, h * window)


def _circular_features(r, n):
    p = jnp.where(r < n, r, 2 * n - r)
    p = jnp.where(r == n, 0, p).astype(F32)
    t = p / (n - 1)
    omega = (2.0 * math.pi / n) * p
    bands = jnp.linspace(1e-4, FILTER_BANDS - 1, FILTER_BANDS, dtype=F32).reshape((-1,) + (1,) * r.ndim)
    phase = omega[None] * bands
    return jnp.concatenate([t[None], jnp.cos(phase), -jnp.sin(phase)], axis=0)


def _filter_features(n):
    z = _circular_features(jnp.arange(2 * n, dtype=jnp.int32), n).T
    return jnp.pad(z, ((0, 0), (0, LANES - z.shape[1])))


def _hyena_filter(zc, w1p, b1, w2, b2, w3r, freq, deltas, *, n):
    rt = min(512, n)
    width = HYENA_ORDER * HYENA_WIDTH
    const = lambda shape: pl.BlockSpec(shape, lambda r: (0,) * len(shape))
    return pl.pallas_call(
        functools.partial(_filter_kernel, n=n),
        out_shape=jax.ShapeDtypeStruct((2 * n, width), F32),
        grid=(2 * n // rt,),
        in_specs=[pl.BlockSpec((rt, LANES), lambda r: (r, 0)), const(w1p.shape), const(b1.shape), const(w2.shape),
                  const(b2.shape), pl.BlockSpec((1, FILTER_HIDDEN, width), lambda r: (r // (n // rt), 0, 0)),
                  const(freq.shape), const(deltas.shape)],
        out_specs=pl.BlockSpec((rt, width), lambda r: (r, 0)),
        compiler_params=_cparams(("parallel",)),
        name="hyena_filter",
    )(zc, w1p, b1, w2, b2, w3r, freq, deltas)


FEAT_PAD = 40


def _filter_dft_kernel(zt_ref, w1t_ref, b1_ref, w2t_ref, b2_ref, w3_ref, fr_ref, e_ref, m1_ref, o_ref, *, st):
    j = pl.program_id(0)
    n1 = zt_ref.shape[2]
    width = w3_ref.shape[1]
    row = lax.broadcasted_iota(jnp.int32, (n1, width), 0)
    first_half = lax.broadcasted_iota(jnp.int32, (3 * FILTER_HIDDEN, n1), 1) < n1 // 2
    zero = jnp.zeros((), BF16)
    contract0 = (((0,), (0,)), ((), ()))
    fr = fr_ref[...]
    cols = []
    for q in range(st):
        zt = zt_ref[q]
        a = jnp.sin(fr * (_hdot(w1t_ref[...], zt) + b1_ref[...]))
        a = jnp.sin(fr * (_hdot(w2t_ref[...], a) + b2_ref[...]))
        a_hi = a.astype(BF16)
        a_lo = (a - a_hi.astype(F32)).astype(BF16)
        stack = jnp.concatenate([a_hi, a_hi, a_lo], axis=0)
        lhs = jnp.concatenate([jnp.where(first_half, stack, zero), jnp.where(first_half, zero, stack)], axis=0)
        h = lax.dot_general(lhs, w3_ref[...], contract0, preferred_element_type=F32)
        decay = lax.dot_general(zt.astype(BF16), e_ref[...], contract0, preferred_element_type=F32)
        h = h * jnp.exp(-decay)
        h = jnp.where((row == n1 // 2) & (j * st + q == 0), 0.0, h)
        cols.append(h.astype(BF16))
    o_ref[...] = jnp.dot(m1_ref[...], jnp.concatenate(cols, axis=1), preferred_element_type=F32).astype(BF16)


def _filter_outer_dft(zt, w1t, b1, w2t, b2, w3r, fr, e, m1):
    n_s, _, n1 = zt.shape
    width = w3r.shape[1]
    st = min(8, n_s)
    const = lambda a: pl.BlockSpec(a.shape, lambda j: (0,) * a.ndim)
    return pl.pallas_call(
        functools.partial(_filter_dft_kernel, st=st),
        out_shape=jax.ShapeDtypeStruct((m1.shape[0], n_s * width), BF16),
        grid=(n_s // st,),
        in_specs=[pl.BlockSpec((st, FEAT_PAD, n1), lambda j: (j, 0, 0)), const(w1t), const(b1), const(w2t), const(b2),
                  const(w3r), const(fr), const(e), const(m1)],
        out_specs=pl.BlockSpec((m1.shape[0], st * width), lambda j: (0, j)),
        compiler_params=_cparams(("parallel",)),
        name="hyena_filter_outer_dft",
    )(zt, w1t, b1, w2t, b2, w3r, fr, e, m1)


def _dft_consts(n):
    big = 2 * n
    n1 = big // DFT_INNER
    half = n1 // 2
    k1 = np.arange(n1)[:, None]
    a = np.arange(n1)[None, :]
    ang = -2.0 * np.pi * ((k1 * a) % n1) / n1
    fr, fi = np.cos(ang), np.sin(ang)
    first = np.block([[fr[:, :half], -fi[:, :half]], [fi[:, :half], fr[:, :half]]])
    first_real = np.concatenate([fr, fi], axis=0)
    last = np.block([[fr[:half, :], fi[:half, :]], [-fi[:half, :], fr[:half, :]]])
    k2 = np.arange(DFT_INNER)[:, None]
    s = np.arange(DFT_INNER)[None, :]
    ang2 = -2.0 * np.pi * ((k2 * s) % DFT_INNER) / DFT_INNER
    gr, gi = np.cos(ang2), np.sin(ang2)
    inner = np.block([[gr, -gi], [gi, gr]])
    inner_inv = np.block([[gr, gi], [-gi, gr]])
    prod = (jnp.arange(n1, dtype=jnp.int32)[:, None] * jnp.arange(DFT_INNER, dtype=jnp.int32)[None, :]) % big
    angt = prod.astype(F32) * (-2.0 * math.pi / big)
    tw = jnp.stack([jnp.cos(angt), jnp.sin(angt)])
    tw = jnp.broadcast_to(tw[..., None], tw.shape + (LANES,))
    f = lambda m: jnp.asarray(np.ascontiguousarray(m), F32).astype(BF16)
    return dict(first=f(first), first_real=f(first_real), last=f(last), inner=f(inner), inner_inv=f(inner_inv),
                tw=tw, n1=n1, inv_scale=1.0 / big)


def _left_matmul_kernel(*refs, mode, inv_scale):
    if mode == "first":
        m_ref, x_ref, o_ref = refs
        o_ref[...] = _bdot(m_ref[...], x_ref[0]).astype(BF16)
        return
    if mode == "last":
        m3_ref, b_ref, g_ref, v_ref, bias_ref, z_ref = refs
    else:
        m3_ref, b_ref, g_ref, v_ref, bias_ref, m1_ref, z_ref, a_ref = refs
    y = _bdot(m3_ref[...], b_ref[...]) * inv_scale
    z = g_ref[0].astype(F32) * (y + bias_ref[...] * v_ref[0].astype(F32))
    z_ref[0] = z.astype(z_ref.dtype)
    if mode == "last_first":
        a_ref[...] = _bdot(m1_ref[...], z).astype(BF16)


DFT_COLS = 2048


def _dft_first(m1, x3, part):
    _, rows, wtot = x3.shape
    wt = DFT_COLS
    return pl.pallas_call(
        functools.partial(_left_matmul_kernel, mode="first", inv_scale=None),
        out_shape=jax.ShapeDtypeStruct((m1.shape[0], wtot), BF16),
        grid=(wtot // wt,),
        in_specs=[pl.BlockSpec(m1.shape, lambda j: (0, 0)), pl.BlockSpec((1, rows, wt), lambda j: (part, 0, j))],
        out_specs=pl.BlockSpec((m1.shape[0], wt), lambda j: (0, j)),
        compiler_params=_cparams(("parallel",)),
        name="dft_outer_forward",
    )(m1, x3)


def _dft_last(m3, b2, gate3, gate_part, v3, v_part, bias_flat, inv_scale, m1=None):
    _, rows, wtot = gate3.shape
    wt = DFT_COLS
    col = lambda r: pl.BlockSpec((r, wt), lambda j: (0, j))
    part = lambda p: pl.BlockSpec((1, rows, wt), lambda j: (p, 0, j))
    const = lambda a: pl.BlockSpec(a.shape, lambda j: (0, 0))
    z_shape = jax.ShapeDtypeStruct((1, rows, wtot), BF16 if m1 is None else F32)
    in_specs = [const(m3), col(b2.shape[0]), part(gate_part), part(v_part), pl.BlockSpec((1, wt), lambda j: (0, 0))]
    if m1 is None:
        return pl.pallas_call(
            functools.partial(_left_matmul_kernel, mode="last", inv_scale=inv_scale),
            out_shape=z_shape,
            grid=(wtot // wt,),
            in_specs=in_specs,
            out_specs=part(0),
            compiler_params=_cparams(("parallel",)),
            name="dft_outer_inverse",
        )(m3, b2, gate3, v3, bias_flat)
    return pl.pallas_call(
        functools.partial(_left_matmul_kernel, mode="last_first", inv_scale=inv_scale),
        out_shape=(z_shape, jax.ShapeDtypeStruct((m1.shape[0], wtot), BF16)),
        grid=(wtot // wt,),
        in_specs=in_specs + [const(m1)],
        out_specs=(part(0), col(m1.shape[0])),
        compiler_params=_cparams(("parallel",)),
        name="dft_outer_inverse_forward",
    )(m3, b2, gate3, v3, bias_flat, m1)


def _dft_mid_kernel(a_ref, tw_ref, g_ref, *rest, slabs, with_filter):
    if with_filter:
        h_ref, gi_ref, o_ref = rest
    else:
        (o_ref,) = rest
    nl = a_ref.shape[3] // LANES
    for j in range(slabs):
        twr = jnp.concatenate([tw_ref[0, j]] * nl, axis=1)
        twi = jnp.concatenate([tw_ref[1, j]] * nl, axis=1)
        ar, ai = a_ref[0, j].astype(F32), a_ref[1, j].astype(F32)
        br = ar * twr - ai * twi
        bi = ar * twi + ai * twr
        x = _bdot(g_ref[...], jnp.concatenate([br.astype(BF16), bi.astype(BF16)], axis=0))
        if not with_filter:
            o_ref[0, j] = x[:DFT_INNER].astype(BF16)
            o_ref[1, j] = x[DFT_INNER:].astype(BF16)
            continue
        xr, xi = x[:DFT_INNER], x[DFT_INNER:]
        hr, hi = h_ref[0, j].astype(F32), h_ref[1, j].astype(F32)
        yr = xr * hr - xi * hi
        yi = xr * hi + xi * hr
        zz = _bdot(gi_ref[...], jnp.concatenate([yr.astype(BF16), yi.astype(BF16)], axis=0))
        zr, zi = zz[:DFT_INNER], zz[DFT_INNER:]
        o_ref[0, j] = (zr * twr + zi * twi).astype(BF16)
        o_ref[1, j] = (zi * twr - zr * twi).astype(BF16)


def _dft_mid(a4, tw, inner, spec=None, inner_inv=None, order=0):
    _, n1, _, c = a4.shape
    slabs = 8 if n1 % 8 == 0 else n1
    blk = lambda w: pl.BlockSpec((2, slabs, DFT_INNER, w), lambda j: (0, j, 0, 0))
    const = lambda a: pl.BlockSpec(a.shape, lambda j: (0, 0))
    in_specs = [blk(c), blk(LANES), const(inner)]
    args = [a4, tw, inner]
    if spec is not None:
        in_specs += [pl.BlockSpec((2, slabs, DFT_INNER, c), lambda j: (0, j, 0, order)), const(inner_inv)]
        args += [spec, inner_inv]
    return pl.pallas_call(
        functools.partial(_dft_mid_kernel, slabs=slabs, with_filter=spec is not None),
        out_shape=jax.ShapeDtypeStruct(a4.shape, BF16),
        grid=(n1 // slabs,),
        in_specs=in_specs,
        out_specs=blk(c),
        compiler_params=_cparams(("parallel",)),
        name="dft_inner_filter" if spec is not None else "dft_inner_forward",
    )(*args)


def _hyena_latent(vx, filt_a, bias, consts):
    _, nb, n, c = vx.shape
    assert nb == 2, "the two samples are carried as the real and imaginary part of one transform"
    n1 = consts["n1"]
    wflat = DFT_INNER * c
    flat = vx.reshape(3, nb * n1 // 2, wflat)
    bias_flat = jnp.tile(bias, (1, DFT_COLS // c))
    tw, inner, inner_inv, scale = consts["tw"], consts["inner"], consts["inner_inv"], consts["inv_scale"]

    spec = _dft_mid(filt_a.reshape(2, n1, DFT_INNER, HYENA_ORDER * c), tw, inner)

    a = _dft_first(consts["first"], flat, 0).reshape(2, n1, DFT_INNER, c)
    bm = _dft_mid(a, tw, inner, spec, inner_inv, order=0)
    z, a = _dft_last(consts["last"], bm.reshape(2 * n1, wflat), flat, 1, flat, 0, bias_flat[0:1], scale,
                     consts["first"])
    bm = _dft_mid(a.reshape(2, n1, DFT_INNER, c), tw, inner, spec, inner_inv, order=1)
    y = _dft_last(consts["last"], bm.reshape(2 * n1, wflat), flat, 2, z, 0, bias_flat[1:2], scale)
    return y.reshape(nb, n, c)


def _hyena_ctx_kernel(u_ref, h_ref, fs_ref, ff_ref, fi_ref, bias_ref, o_ref, *, n):
    big = 2 * n
    c = HYENA_WIDTH
    spec = _hdot(fs_ref[...], h_ref[...])
    u = u_ref[...]
    v, x1, x2 = u[:, 0:c], u[:, c:2 * c], u[:, 2 * c:3 * c]

    def conv(zin, order):
        x = _hdot(ff_ref[...], zin)
        xr, xi = x[:big], x[big:]
        hr, hi = spec[:big, order * c:(order + 1) * c], spec[big:, order * c:(order + 1) * c]
        y = jnp.concatenate([xr * hr - xi * hi, xr * hi + xi * hr], axis=0)
        return _hdot(fi_ref[...], y)

    bias = bias_ref[...]
    z = x1 * (conv(v, 0) + bias[0:1] * v)
    o_ref[...] = (x2 * (conv(z, 1) + bias[1:2] * z)).astype(BF16)


def _ctx_dft_consts(n):
    big = 2 * n
    k = np.arange(big)[:, None]
    t = np.arange(big)[None, :]
    ang = -2.0 * np.pi * ((k * t) % big) / big
    fr, fi = np.cos(ang), np.sin(ang)
    spec_m = np.concatenate([fr, fi], axis=0)
    fwd = np.block([[fr[:, :n], -fi[:, :n]], [fi[:, :n], fr[:, :n]]])
    inv = np.block([[fr.T[:n, :], fi.T[:n, :]], [-fi.T[:n, :], fr.T[:n, :]]]) / big
    f = lambda m: jnp.asarray(np.ascontiguousarray(m), F32)
    return f(spec_m), f(fwd), f(inv)


def _hyena_ctx(u_ctx, h_circ, bias, cconsts):
    nb, n, w = u_ctx.shape
    assert nb == 2
    fs, ff, fi = cconsts
    full = lambda a: pl.BlockSpec(a.shape, lambda i: (0,) * a.ndim)
    u2 = u_ctx.reshape(nb * n, w)
    out = pl.pallas_call(
        functools.partial(_hyena_ctx_kernel, n=n),
        out_shape=jax.ShapeDtypeStruct((nb * n, HYENA_WIDTH), BF16),
        grid=(1,),
        in_specs=[full(u2), full(h_circ), full(fs), full(ff), full(fi), full(bias)],
        out_specs=pl.BlockSpec((nb * n, HYENA_WIDTH), lambda i: (0, 0)),
        compiler_params=_cparams(("arbitrary",)),
        name="hyena_context",
    )(u2, h_circ, fs, ff, fi, bias)
    return out.reshape(nb, n, HYENA_WIDTH)


def _gla_kernel(xf_ref, xb_ref, of_ref, ob_ref, st_ref):
    s = pl.program_id(0)

    @pl.when(s == 0)
    def _():
        st_ref[...] = jnp.zeros(st_ref.shape, F32)

    c = GLA_CHUNK
    kw, vw = GLA_K_WIDTH, GLA_V_WIDTH
    ri3 = lax.broadcasted_iota(jnp.int32, (c, 3 * c), 0)
    ci3 = lax.broadcasted_iota(jnp.int32, (c, 3 * c), 1) % c
    kr = lax.broadcasted_iota(jnp.int32, (GLA_HEADS * c, kw), 0) // c
    kc = lax.broadcasted_iota(jnp.int32, (GLA_HEADS * c, kw), 1) // GLA_DK
    mask_k = kr == kc
    vr = lax.broadcasted_iota(jnp.int32, (GLA_HEADS * c, vw), 0) // c
    vc = lax.broadcasted_iota(jnp.int32, (GLA_HEADS * c, vw), 1) // GLA_DV
    mask_v = vr == vc
    sr = lax.broadcasted_iota(jnp.int32, (vw, kw), 0) // GLA_DV
    sc = lax.broadcasted_iota(jnp.int32, (vw, kw), 1) // GLA_DK
    mask_s = sr == sc
    qi = lax.broadcasted_iota(jnp.int32, (c, GLA_HEADS * c), 0)
    kj = lax.broadcasted_iota(jnp.int32, (c, GLA_HEADS * c), 1) % c

    def chunk(x_ref, o_ref, d, b, ch):
        backward = d == 1
        tri3 = ((ri3 <= ci3) if backward else (ri3 >= ci3)).astype(BF16)
        causal = (qi <= kj) if backward else (qi >= kj)
        goff = 2 * kw + vw + (kw if backward else 0)
        r0 = ch * c
        q = x_ref[b, r0:r0 + c, 0:kw]
        k = x_ref[b, r0:r0 + c, kw:2 * kw]
        v = x_ref[b, r0:r0 + c, 2 * kw:2 * kw + vw]
        g = x_ref[b, r0:r0 + c, goff:goff + kw]
        g_hi = g.astype(BF16)
        g_mid = (g - g_hi.astype(F32)).astype(BF16)
        g_lo = (g - g_hi.astype(F32) - g_mid.astype(F32)).astype(BF16)
        gc = jnp.dot(tri3, jnp.concatenate([g_hi, g_mid, g_lo], axis=0), preferred_element_type=F32)
        g_last = gc[0:1] if backward else gc[c - 1:c]
        q_dec = q * jnp.exp(gc)
        k_dec = k * jnp.exp(-gc)
        k_tail = k * jnp.exp(g_last - gc)
        k_blk = jnp.where(mask_k, jnp.concatenate([k_dec] * GLA_HEADS, axis=0), 0.0).astype(BF16)
        v_bf = v.astype(BF16)
        v_blk = jnp.where(mask_v, jnp.concatenate([v_bf] * GLA_HEADS, axis=0), jnp.zeros((), BF16))
        q_bf = q_dec.astype(BF16)
        scores = lax.dot_general(q_bf, k_blk, (((1,), (1,)), ((), ())), preferred_element_type=F32)
        scores = jnp.where(causal, scores, 0.0)
        st = st_ref[d, b]
        o = jnp.dot(scores.astype(BF16), v_blk, preferred_element_type=F32)
        o = o + lax.dot_general(q_bf, st.astype(BF16), (((1,), (1,)), ((), ())), preferred_element_type=F32)
        o_ref[b, r0:r0 + c, :] = o.astype(BF16)
        kv_t = lax.dot_general(v_bf, k_tail.astype(BF16), (((0,), (0,)), ((), ())), preferred_element_type=F32)
        st_ref[d, b] = st * jnp.exp(g_last) + jnp.where(mask_s, kv_t, 0.0)

    n_chunks = xf_ref.shape[1] // c
    for u in range(n_chunks):
        for b in range(xf_ref.shape[0]):
            chunk(xf_ref, of_ref, 0, b, u)
            chunk(xb_ref, ob_ref, 1, b, n_chunks - 1 - u)


def _gla(gla_in, *, ctx_len):
    nb, t, w = gla_in.shape
    tb = TOK_TILE
    nblk = t // tb
    cblk = ctx_len // tb
    fwd = lambda s: (0, s, 0)
    bwd = lambda s: (0, jnp.where(s < cblk, cblk - 1 - s, nblk - 1 - (s - cblk)), 0)
    out = jax.ShapeDtypeStruct((nb, t, GLA_V_WIDTH), BF16)
    return pl.pallas_call(
        _gla_kernel,
        out_shape=(out, out),
        grid=(nblk,),
        in_specs=[pl.BlockSpec((nb, tb, w), fwd), pl.BlockSpec((nb, tb, w), bwd)],
        out_specs=(pl.BlockSpec((nb, tb, GLA_V_WIDTH), fwd), pl.BlockSpec((nb, tb, GLA_V_WIDTH), bwd)),
        scratch_shapes=[pltpu.VMEM((2, nb, GLA_V_WIDTH, GLA_K_WIDTH), F32)],
        compiler_params=_cparams(("arbitrary",)),
        name="gla_scans",
    )(gla_in, gla_in)


def _out_kernel(h_ref, mod_ref, at_ref, hy_ref, gf_ref, gb_ref, gr_ref, gn_ref, gmat_ref, w_ref, o_ref, *,
                n_batch, ctx_tiles, tile_off, d_model):
    b, i = pl.program_id(0), pl.program_id(1)
    mod = _mod_row(mod_ref, i + tile_off < ctx_tiles, b, n_batch)
    gate = mod[:, 2 * d_model:3 * d_model]
    gmat = gmat_ref[...]
    gn = gn_ref[...]
    a0, a1 = ATTN_WIDTH, ATTN_WIDTH + HYENA_WIDTH
    ya = _head_rms(at_ref[0].astype(F32), gn[:, :a0], gmat)
    yh = _head_rms(hy_ref[0].astype(F32), gn[:, a0:a1], gmat)
    yg = _head_rms(gf_ref[0].astype(F32) + gb_ref[0].astype(F32), gn[:, a1:], gmat) * _silu(gr_ref[0].astype(F32))
    y = jnp.concatenate([ya, yh, yg], axis=1)
    o_ref[0] = h_ref[0] + gate * _bdot(y, w_ref[...])


def _out_projection(h, mod, attn, hy, gla_f, gla_b, gr, gn, gmat, w_out, *, ctx_len, skip_ctx):
    nb, t, d = h.shape
    tb = TOK_TILE
    off = ctx_len // tb if skip_ctx else 0
    kern = functools.partial(_out_kernel, n_batch=nb, ctx_tiles=ctx_len // tb, tile_off=off, d_model=d)
    const = lambda a: pl.BlockSpec(a.shape, lambda b, i: (0,) * a.ndim)
    tok = lambda w: pl.BlockSpec((1, tb, w), lambda b, i: (b, i + off, 0))
    return pl.pallas_call(
        kern,
        out_shape=jax.ShapeDtypeStruct((nb, t - off * tb, d), F32),
        grid=(nb, t // tb - off),
        in_specs=[tok(d), const(mod), tok(ATTN_WIDTH), tok(HYENA_WIDTH), tok(GLA_V_WIDTH), tok(GLA_V_WIDTH),
                  tok(GLA_V_WIDTH), const(gn), const(gmat), const(w_out)],
        out_specs=pl.BlockSpec((1, tb, d), lambda b, i: (b, i, 0)),
        compiler_params=_cparams(("parallel", "parallel")),
        name="out_projection",
    )(h, mod, attn, hy, gla_f, gla_b, gr, gn, gmat, w_out)


def _ffn_kernel(h_ref, mod_ref, g2_ref, w1_ref, w3_ref, w2_ref, fg_ref, o_ref, *, n_batch, ctx_tiles, tile_off,
                d_model, final):
    b, i = pl.program_id(0), pl.program_id(1)
    mod = _mod_row(mod_ref, i + tile_off < ctx_tiles, b, n_batch)
    shift, scale, gate = (mod[:, 3 * d_model:4 * d_model], mod[:, 4 * d_model:5 * d_model],
                          mod[:, 5 * d_model:6 * d_model])
    x = h_ref[0]
    xn = x * lax.rsqrt(jnp.mean(jnp.square(x), axis=-1, keepdims=True) + NORM_EPS) * g2_ref[...]
    u = (xn * (1.0 + scale) + shift).astype(BF16)
    a = jnp.dot(u, w1_ref[...], preferred_element_type=F32)
    c = jnp.dot(u, w3_ref[...], preferred_element_type=F32)
    hid = (_silu(a) * c).astype(BF16)
    y = x + gate * jnp.dot(hid, w2_ref[...], preferred_element_type=F32)
    if final:
        y = y * lax.rsqrt(jnp.mean(jnp.square(y), axis=-1, keepdims=True) + NORM_EPS) * fg_ref[...]
    o_ref[0] = y


def _ffn(h, mod, g2, w1, w3, w2, fg, *, ctx_len, lat_only, final):
    nb, t, d = h.shape
    tb = TOK_TILE
    off = ctx_len // tb if lat_only else 0
    kern = functools.partial(_ffn_kernel, n_batch=nb, ctx_tiles=ctx_len // tb, tile_off=off, d_model=d, final=final)
    const = lambda a: pl.BlockSpec(a.shape, lambda b, i: (0,) * a.ndim)
    tok = pl.BlockSpec((1, tb, d), lambda b, i: (b, i, 0))
    return pl.pallas_call(
        kern,
        out_shape=jax.ShapeDtypeStruct((nb, t, d), F32),
        grid=(nb, t // tb),
        in_specs=[tok, const(mod), const(g2), const(w1), const(w3), const(w2), const(fg)],
        out_specs=tok,
        compiler_params=_cparams(("parallel", "parallel")),
        name="swiglu_ffn",
    )(h, mod, g2, w1, w3, w2, fg)


def _rope_tables(n_lat, ctx_len):
    rows = n_lat // GRID_W
    row = jnp.broadcast_to(jnp.arange(rows, dtype=F32)[:, None], (rows, GRID_W)).reshape(-1)
    col = jnp.broadcast_to(jnp.arange(GRID_W, dtype=F32)[None, :], (rows, GRID_W)).reshape(-1)
    n_freq = HEAD_DIM // 4
    inv_freq = jnp.power(ROPE_THETA, -jnp.arange(n_freq, dtype=F32) / n_freq)
    ang = jnp.concatenate([row[:, None] * inv_freq, col[:, None] * inv_freq], axis=-1)
    cos, sin = jnp.cos(ang), jnp.sin(ang)
    cos = jnp.concatenate([jnp.ones((ctx_len, HEAD_DIM // 2), F32), cos], axis=0)
    sin = jnp.concatenate([jnp.zeros((ctx_len, HEAD_DIM // 2), F32), sin], axis=0)
    cosf = jnp.tile(cos, (1, 2 * LANES // HEAD_DIM))
    sinf = jnp.tile(jnp.concatenate([-sin, sin], axis=1), (1, LANES // HEAD_DIM))
    return cosf, sinf


def kernel(x, c, ctx, c_ctx, ada_w, ada_b, norm1_g, w_in, q_norm_g, k_norm_g, hy_conv_w, hy_conv_b, filt_w1, filt_b1, filt_w2, filt_b2, filt_w3, filt_freq, hy_bias, gla_gate_w, gla_gate_b, out_norm_g, w_out, norm2_g, ffn_w1, ffn_w3, ffn_w2, final_norm_g):
    nb, n_lat, d = x.shape
    ctx_len = ctx.shape[1]
    depth = w_in.shape[0]
    assert nb == 2 and ctx_len % TOK_TILE == 0 and n_lat % TOK_TILE == 0 and n_lat % DFT_INNER == 0

    h = jnp.concatenate([ctx, x], axis=1)
    cond = jnp.concatenate([c, c_ctx[None, :], jnp.zeros((8 - nb - 1, d), F32)], axis=0)
    mods = _modulation(cond, ada_w, ada_b)

    cosf, sinf = _rope_tables(n_lat, ctx_len)
    lane = np.arange(LANES)
    gmat = jnp.asarray(np.tile((lane[:, None] // HEAD_DIM) == (lane[None, :] // HEAD_DIM), (2, 1)), BF16)
    consts = _dft_consts(n_lat)
    cconsts = _ctx_dft_consts(ctx_len)
    zc_ctx = _filter_features(ctx_len)
    deltas = jnp.abs(jnp.linspace(HYENA_MIN_DECAY, HYENA_MAX_DECAY, HYENA_WIDTH, dtype=F32))
    deltas = jnp.tile(deltas[None, :], (1, HYENA_ORDER))
    n1 = consts["n1"]
    rows_sa = (DFT_INNER * jnp.arange(n1, dtype=jnp.int32)[None, :] + jnp.arange(DFT_INNER, dtype=jnp.int32)[:, None])
    zt_lat = _circular_features(rows_sa, n_lat).transpose(1, 0, 2)
    t_hi, t_mid, t_lo = _split3(zt_lat[:, 0, :])
    d_hi, d_mid, d_lo = _split3(deltas[0])
    zt_lat = jnp.concatenate([zt_lat] + [p[:, None, :] for p in (t_hi, t_hi, t_hi, t_mid, t_mid, t_lo)], axis=1)
    n_feat = zt_lat.shape[1]
    zt_lat = jnp.pad(zt_lat, ((0, 0), (0, FEAT_PAD - n_feat), (0, 0)))
    decay_rows = jnp.zeros((FEAT_PAD, deltas.shape[1]), F32).at[n_feat - 6:n_feat].set(
        jnp.stack([d_hi, d_mid, d_lo, d_hi, d_mid, d_hi])).astype(BF16)
    lanes_n1 = lambda vec: jnp.broadcast_to(vec[:, None], (vec.shape[0], n1))

    w_in_b = jnp.pad(w_in, ((0, 0), (0, 0), (0, IN_PAD - w_in.shape[2]))).astype(BF16)
    w_out_b = w_out.astype(BF16)
    w1_b, w3_b, w2_b = ffn_w1.astype(BF16), ffn_w3.astype(BF16), ffn_w2.astype(BF16)
    fg = final_norm_g.reshape(1, d)

    out = None
    for l in range(depth):
        last = l == depth - 1
        mod = mods[l]
        wg = jnp.zeros((LANES, 2 * GLA_K_WIDTH), F32)
        wg = wg.at[0:GLA_GATE_RANK, 0:GLA_K_WIDTH].set(gla_gate_w[l, 0])
        wg = wg.at[GLA_GATE_RANK:2 * GLA_GATE_RANK, GLA_K_WIDTH:].set(gla_gate_w[l, 1])
        bg = gla_gate_b[l].reshape(1, 2 * GLA_K_WIDTH)
        qg = jnp.tile(q_norm_g[l][None, :], (1, ATTN_HEADS))
        kg = jnp.tile(k_norm_g[l][None, :], (1, ATTN_KV_HEADS))

        q, k, v, hy, gla_in, gr = _in_projection(h, mod, norm1_g[l].reshape(1, d), w_in_b[l], qg, kg, cosf, sinf,
                                                 gmat, wg.astype(BF16), bg, ctx_len=ctx_len)
        attn = _attention(q, k, v, ctx_len=ctx_len)

        vx, u_ctx = _short_conv(hy, hy_conv_w[l], hy_conv_b[l], ctx_len=ctx_len)
        w1p = jnp.pad(filt_w1[l], ((0, LANES - filt_w1.shape[1]), (0, 0)))
        w3r = filt_w3[l].reshape(FILTER_HIDDEN, HYENA_ORDER, 2, HYENA_WIDTH).transpose(2, 0, 1, 3)
        w3r = w3r.reshape(2, FILTER_HIDDEN, HYENA_ORDER * HYENA_WIDTH)
        fargs = (w1p, filt_b1[l][None, :], filt_w2[l], filt_b2[l][None, :], w3r, filt_freq[l][None, :], deltas)
        w1t = jnp.pad(filt_w1[l].T, ((0, 0), (0, FEAT_PAD - filt_w1.shape[1])))
        w3_hi, w3_lo, _ = _split3(w3r)
        w3_stack = jnp.concatenate([w3_hi, w3_lo, w3_hi], axis=1).reshape(-1, w3r.shape[2]).astype(BF16)
        filt_a = _filter_outer_dft(zt_lat, w1t, lanes_n1(filt_b1[l]), filt_w2[l].T, lanes_n1(filt_b2[l]), w3_stack,
                                   lanes_n1(filt_freq[l]), decay_rows, consts["first_real"])
        hy_lat = _hyena_latent(vx, filt_a, hy_bias[l], consts)
        if not last:
            h_ctx = _hyena_filter(zc_ctx, *fargs, n=ctx_len)
            hy_ctx = _hyena_ctx(u_ctx, h_ctx, hy_bias[l], cconsts)
        else:
            hy_ctx = jnp.zeros((nb, ctx_len, HYENA_WIDTH), BF16)
        hy_out = jnp.concatenate([hy_ctx, hy_lat], axis=1)

        gla_f, gla_b = _gla(gla_in, ctx_len=ctx_len)

        h = _out_projection(h, mod, attn, hy_out, gla_f, gla_b, gr, out_norm_g[l].reshape(1, -1), gmat, w_out_b[l],
                            ctx_len=ctx_len, skip_ctx=last)
        h = _ffn(h, mod, norm2_g[l].reshape(1, d), w1_b[l], w3_b[l], w2_b[l], fg, ctx_len=ctx_len, lat_only=last,
                 final=last)
    return h
```

```python
import functools
import math

import numpy as np
import jax
import jax.numpy as jnp
from jax import lax
from jax.experimental import pallas as pl
from jax.experimental.pallas import tpu as pltpu

F32 = jnp.float32
BF16 = jnp.bfloat16
HI = lax.Precision.HIGHEST

LANES = 128
VMEM_LIMIT = 56 * 1024 * 1024

HEAD_DIM = 64
ATTN_HEADS = 8
ATTN_KV_HEADS = 2
ATTN_WIDTH = ATTN_HEADS * HEAD_DIM
KV_WIDTH = ATTN_KV_HEADS * HEAD_DIM
HYENA_WIDTH = 256
HYENA_ORDER = 2
FILTER_BANDS = 16
FILTER_HIDDEN = 64
GLA_HEADS = 4
GLA_DK = 32
GLA_DV = 64
GLA_K_WIDTH = GLA_HEADS * GLA_DK
GLA_V_WIDTH = GLA_HEADS * GLA_DV
GLA_GATE_RANK = 16
GLA_GATE_TAU = 16.0
GLA_CHUNK = 64
GRID_W = 64
ROPE_THETA = 10000.0
N_MOD = 6
NORM_EPS = 1e-6
HYENA_TARGET = 1e-2
HYENA_MIN_DECAY = math.log(HYENA_TARGET) / 1.5
HYENA_MAX_DECAY = math.log(HYENA_TARGET) / 0.3

TOK_TILE = 256
DFT_INNER = 128
IN_PAD = 2432

_OQ, _OK, _OV, _OHY = 0, 512, 640, 768
_OGQ, _OGK, _OGV, _OGR, _OGA = 1536, 1664, 1792, 2048, 2304


def _cparams(sem, vmem=VMEM_LIMIT):
    return pltpu.CompilerParams(dimension_semantics=sem, vmem_limit_bytes=vmem)


def _bdot(a, b):
    return jnp.dot(a.astype(BF16), b.astype(BF16), preferred_element_type=F32)


def _hdot(a, b):
    return jnp.dot(a, b, preferred_element_type=F32, precision=HI)


def _group_mean_square(x, gmat):
    outs = []
    for j in range(x.shape[1] // LANES):
        sq = jnp.square(x[:, j * LANES:(j + 1) * LANES])
        hi = sq.astype(BF16)
        lo = (sq - hi.astype(F32)).astype(BF16)
        s = jnp.dot(jnp.concatenate([hi, lo], axis=1), gmat, preferred_element_type=F32)
        outs.append(s * (1.0 / HEAD_DIM))
    return outs[0] if len(outs) == 1 else jnp.concatenate(outs, axis=1)


def _head_rms(x, gain, gmat):
    ms = _group_mean_square(x, gmat)
    return x * lax.rsqrt(ms + NORM_EPS) * gain


def _mod_row(mod_ref, is_ctx, b, n_batch):
    row = jnp.where(is_ctx, n_batch, b)
    return mod_ref[pl.ds(row, 1), :]


def _silu(x):
    return x * (1.0 / (1.0 + jnp.exp(-x)))


def _split3(x):
    hi = x.astype(BF16).astype(F32)
    mid = (x - hi).astype(BF16).astype(F32)
    lo = (x - hi - mid).astype(BF16).astype(F32)
    return hi, mid, lo


def _mod_kernel(c_ref, w_ref, b_ref, o_ref):
    s = _silu(c_ref[...])
    o_ref[0] = _bdot(s, w_ref[0]) + b_ref[0]


def _modulation(cond_rows, ada_w, ada_b):
    depth, d, w = ada_w.shape
    ct = 1536
    return pl.pallas_call(
        _mod_kernel,
        out_shape=jax.ShapeDtypeStruct((depth, 8, w), F32),
        grid=(depth, w // ct),
        in_specs=[pl.BlockSpec((8, d), lambda l, j: (0, 0)),
                  pl.BlockSpec((1, d, ct), lambda l, j: (l, 0, j)),
                  pl.BlockSpec((1, 1, ct), lambda l, j: (l, 0, j))],
        out_specs=pl.BlockSpec((1, 8, ct), lambda l, j: (l, 0, j)),
        compiler_params=_cparams(("arbitrary", "arbitrary")),
        name="adaln_modulation",
    )(cond_rows, ada_w, ada_b.reshape(depth, 1, w))


def _in_kernel(h_ref, mod_ref, g1_ref, w_ref, qg_ref, kg_ref, cos_ref, sin_ref, gmat_ref, wg_ref, bg_ref,
               q_ref, k_ref, v_ref, hy_ref, gla_ref, gr_ref, *, n_batch, ctx_tiles, d_model):
    b, i = pl.program_id(0), pl.program_id(1)
    mod = _mod_row(mod_ref, i < ctx_tiles, b, n_batch)
    shift, scale = mod[:, 0:d_model], mod[:, d_model:2 * d_model]
    x = h_ref[0]
    xn = x * lax.rsqrt(jnp.mean(jnp.square(x), axis=-1, keepdims=True) + NORM_EPS) * g1_ref[...]
    u = xn * (1.0 + scale) + shift
    proj = _bdot(u, w_ref[...])

    gmat = gmat_ref[...]
    cosf, sinf = cos_ref[...], sin_ref[...]
    lane = lax.broadcasted_iota(jnp.int32, (x.shape[0], LANES), 1)
    first_half = (lane % HEAD_DIM) < (HEAD_DIM // 2)
    low_head = lane < HEAD_DIM

    def rope(t):
        partner = jnp.where(first_half, pltpu.roll(t, LANES - HEAD_DIM // 2, 1), pltpu.roll(t, HEAD_DIM // 2, 1))
        return t * cosf + partner * sinf

    qn = _head_rms(proj[:, _OQ:_OQ + ATTN_WIDTH], qg_ref[...], gmat)
    zero = jnp.zeros((x.shape[0], LANES), F32)
    for p in range(ATTN_HEADS // 2):
        t = rope(qn[:, p * LANES:(p + 1) * LANES]) * (HEAD_DIM ** -0.5 * math.log2(math.e))
        tr = pltpu.roll(t, HEAD_DIM, 1)
        if 2 * p < ATTN_HEADS // ATTN_KV_HEADS:
            q_ref[0, 2 * p] = jnp.where(low_head, t, zero).astype(BF16)
            q_ref[0, 2 * p + 1] = jnp.where(low_head, tr, zero).astype(BF16)
        else:
            q_ref[0, 2 * p] = jnp.where(low_head, zero, tr).astype(BF16)
            q_ref[0, 2 * p + 1] = jnp.where(low_head, zero, t).astype(BF16)
    kn = _head_rms(proj[:, _OK:_OK + KV_WIDTH], kg_ref[...], gmat)
    k_ref[0] = rope(kn).astype(BF16)
    v_ref[0] = proj[:, _OV:_OV + KV_WIDTH].astype(BF16)
    hy_ref[0] = proj[:, _OHY:_OHY + 3 * HYENA_WIDTH]

    zg = _bdot(proj[:, _OGA:_OGA + LANES], wg_ref[...]) + bg_ref[...]
    log_gate = (jnp.minimum(zg, 0.0) - jnp.log(1.0 + jnp.exp(-jnp.abs(zg)))) * (1.0 / GLA_GATE_TAU)
    gla_ref[0, :, 0:GLA_K_WIDTH] = proj[:, _OGQ:_OGQ + GLA_K_WIDTH] * (GLA_DK ** -0.5)
    gla_ref[0, :, GLA_K_WIDTH:2 * GLA_K_WIDTH + GLA_V_WIDTH] = proj[:, _OGK:_OGK + GLA_K_WIDTH + GLA_V_WIDTH]
    gla_ref[0, :, 2 * GLA_K_WIDTH + GLA_V_WIDTH:] = log_gate
    gr_ref[0] = proj[:, _OGR:_OGR + GLA_V_WIDTH].astype(BF16)


def _in_projection(h, mod, g1, w_in, qg, kg, cosf, sinf, gmat, wg, bg, *, ctx_len):
    nb, t, d = h.shape
    tb = TOK_TILE
    kern = functools.partial(_in_kernel, n_batch=nb, ctx_tiles=ctx_len // tb, d_model=d)
    const = lambda shape: pl.BlockSpec(shape, lambda b, i: (0,) * len(shape))
    tok = lambda w: pl.BlockSpec((1, tb, w), lambda b, i: (b, i, 0))
    gla_w = 2 * GLA_K_WIDTH + GLA_V_WIDTH + 2 * GLA_K_WIDTH
    return pl.pallas_call(
        kern,
        out_shape=(jax.ShapeDtypeStruct((nb, ATTN_HEADS, t, LANES), BF16),
                   jax.ShapeDtypeStruct((nb, t, KV_WIDTH), BF16),
                   jax.ShapeDtypeStruct((nb, t, KV_WIDTH), BF16),
                   jax.ShapeDtypeStruct((nb, t, 3 * HYENA_WIDTH), F32),
                   jax.ShapeDtypeStruct((nb, t, gla_w), F32),
                   jax.ShapeDtypeStruct((nb, t, GLA_V_WIDTH), BF16)),
        grid=(nb, t // tb),
        in_specs=[tok(d), const(mod.shape), const(g1.shape), const(w_in.shape), const(qg.shape), const(kg.shape),
                  pl.BlockSpec((tb, LANES), lambda b, i: (i, 0)), pl.BlockSpec((tb, LANES), lambda b, i: (i, 0)),
                  const(gmat.shape), const(wg.shape), const(bg.shape)],
        out_specs=(pl.BlockSpec((1, ATTN_HEADS, tb, LANES), lambda b, i: (b, 0, i, 0)),
                   tok(KV_WIDTH), tok(KV_WIDTH), tok(3 * HYENA_WIDTH), tok(gla_w), tok(GLA_V_WIDTH)),
        compiler_params=_cparams(("parallel", "parallel")),
        name="in_projection",
    )(h, mod, g1, w_in, qg, kg, cosf, sinf, gmat, wg, bg)


KV_UNROLL = 4


def _attn_kernel(q_ref, k_ref, v_ref, o_ref, m_sc, acc_sc, s_sc, mx_sc, p_sc, *, ctx_len, ctx_tiles, kv_tile,
                 n_kv_tiles):
    i = pl.program_id(1)
    tq = q_ref.shape[2]
    per_kv = ATTN_HEADS // ATTN_KV_HEADS
    grp = per_kv * tq
    m_sc[...] = jnp.full(m_sc.shape, -jnp.inf, F32)
    acc_sc[...] = jnp.zeros(acc_sc.shape, F32)

    def scores(start, size, buf):
        q = q_ref[0].reshape(ATTN_HEADS * tq, LANES)
        kc = k_ref[0, pl.ds(start, size), :]
        s = lax.dot_general(q, kc, (((1,), (1,)), ((), ())), preferred_element_type=F32)
        s_sc[buf, :, 0:size] = s
        mx_sc[buf] = jnp.broadcast_to(jnp.max(s, axis=1, keepdims=True), mx_sc.shape[1:])

    def consume(start, size, buf, pbuf):
        vc = v_ref[0, pl.ds(start, size), :]
        low = lax.broadcasted_iota(jnp.int32, vc.shape, 1) < HEAD_DIM
        one = jnp.ones((), BF16)
        v_ones = (jnp.where(low, vc, one), jnp.where(low, one, vc))
        for g in range(ATTN_KV_HEADS):
            alphas = []
            for hh in range(g * per_kv, (g + 1) * per_kv):
                rows = slice(hh * tq, (hh + 1) * tq)
                m_prev = m_sc[rows, :]
                m_new = jnp.maximum(m_prev, mx_sc[buf, rows, :])
                alphas.append(jnp.exp2(m_prev - m_new))
                for j in range(size // LANES):
                    cols = slice(j * LANES, (j + 1) * LANES)
                    p_sc[pbuf, rows, cols] = jnp.exp2(s_sc[buf, rows, cols] - m_new).astype(BF16)
                m_sc[rows, :] = m_new
            rows_g = slice(g * grp, (g + 1) * grp)
            pv = jnp.dot(p_sc[pbuf, rows_g, 0:size], v_ones[g], preferred_element_type=F32)
            acc_sc[rows_g, :] = jnp.concatenate(alphas, axis=0) * acc_sc[rows_g, :] + pv

    @pl.when(i < ctx_tiles)
    def _():
        scores(0, ctx_len, 0)
        consume(0, ctx_len, 0, 0)

    @pl.when(i >= ctx_tiles)
    def _():
        def tile_start(j):
            return pl.multiple_of(j * kv_tile, kv_tile)

        def run(j0, count, tail):
            for u in range(count):
                if not (tail and u + 1 == count):
                    scores(tile_start(j0 + u + 1), kv_tile, (u + 1) % 2)
                consume(tile_start(j0 + u), kv_tile, u % 2, u)

        trips = (n_kv_tiles - 1) // KV_UNROLL
        scores(0, kv_tile, 0)
        lax.fori_loop(0, trips, lambda k, c: (run(k * KV_UNROLL, KV_UNROLL, False), c)[1], 0)
        run(trips * KV_UNROLL, n_kv_tiles - trips * KV_UNROLL, True)

    low_head = lax.broadcasted_iota(jnp.int32, (tq, LANES), 1) < HEAD_DIM
    for p in range(ATTN_HEADS // 2):
        outs = []
        for hh in (2 * p, 2 * p + 1):
            acc = acc_sc[hh * tq:(hh + 1) * tq, :]
            o = acc / pltpu.roll(acc, HEAD_DIM, 1)
            src_low = hh < per_kv
            dst_low = hh % 2 == 0
            outs.append(o if src_low == dst_low else pltpu.roll(o, HEAD_DIM, 1))
        o_ref[0, :, p * LANES:(p + 1) * LANES] = jnp.where(low_head, outs[0], outs[1]).astype(BF16)


def _attention(q, k, v, *, ctx_len):
    nb, nh, t, _ = q.shape
    tq = TOK_TILE
    n_lat = t - ctx_len
    kv_tile = max(w for w in (256, 512, 768, 1024) if t % w == 0)
    assert ctx_len <= kv_tile
    rows = nh * tq
    kern = functools.partial(_attn_kernel, ctx_len=ctx_len, ctx_tiles=ctx_len // tq, kv_tile=kv_tile,
                             n_kv_tiles=t // kv_tile)
    return pl.pallas_call(
        kern,
        out_shape=jax.ShapeDtypeStruct((nb, t, ATTN_WIDTH), BF16),
        grid=(nb, t // tq),
        in_specs=[pl.BlockSpec((1, nh, tq, LANES), lambda b, i: (b, 0, i, 0)),
                  pl.BlockSpec((1, t, KV_WIDTH), lambda b, i: (b, 0, 0)),
                  pl.BlockSpec((1, t, KV_WIDTH), lambda b, i: (b, 0, 0))],
        out_specs=pl.BlockSpec((1, tq, ATTN_WIDTH), lambda b, i: (b, i, 0)),
        scratch_shapes=[pltpu.VMEM((rows, LANES), F32), pltpu.VMEM((rows, LANES), F32),
                        pltpu.VMEM((2, rows, kv_tile), F32), pltpu.VMEM((2, rows, LANES), F32),
                        pltpu.VMEM((KV_UNROLL, rows, kv_tile), BF16)],
        compiler_params=_cparams(("parallel", "arbitrary")),
        name="gqa_attention",
    )(q, k, v)


def _short_conv_kernel(x_ref, w_ref, b_ref, lat_ref, ctx_ref, *, ctx_len):
    x = x_ref[0]
    t = x.shape[0]
    row = lax.broadcasted_iota(jnp.int32, x.shape, 0)
    prev = jnp.where((row == 0) | (row == ctx_len), 0.0, pltpu.roll(x, 1, 0))
    nxt = jnp.where((row == ctx_len - 1) | (row == t - 1), 0.0, pltpu.roll(x, t - 1, 0))
    w = w_ref[...]
    u = prev * w[0:1, :] + x * w[1:2, :] + nxt * w[2:3, :] + b_ref[...]
    ctx_ref[0] = u[:ctx_len]
    lat_ref[0, 0] = u[ctx_len:].astype(BF16)


def _short_conv(hy, conv_w, conv_b, *, ctx_len):
    nb, t, w = hy.shape
    per = HYENA_WIDTH // LANES
    return pl.pallas_call(
        functools.partial(_short_conv_kernel, ctx_len=ctx_len),
        out_shape=(jax.ShapeDtypeStruct((3, nb, t - ctx_len, HYENA_WIDTH), BF16),
                   jax.ShapeDtypeStruct((nb, ctx_len, w), F32)),
        grid=(nb, w // LANES),
        in_specs=[pl.BlockSpec((1, t, LANES), lambda b, j: (b, 0, j)),
                  pl.BlockSpec((3, LANES), lambda b, j: (0, j)),
                  pl.BlockSpec((1, LANES), lambda b, j: (0, j))],
        out_specs=(pl.BlockSpec((1, 1, t - ctx_len, LANES), lambda b, j: (j // per, b, 0, j % per)),
                   pl.BlockSpec((1, ctx_len, LANES), lambda b, j: (b, 0, j))),
        compiler_params=_cparams(("parallel", "parallel")),
        name="hyena_short_conv",
    )(hy, conv_w, conv_b.reshape(1, w))


def _filter_kernel(z_ref, w1_ref, b1_ref, w2_ref, b2_ref, w3_ref, fr_ref, dl_ref, o_ref, *, n):
    r = pl.program_id(0)
    z = z_ref[...]
    fr = fr_ref[...]
    a = jnp.sin(fr * (_hdot(z, w1_ref[...]) + b1_ref[...]))
    a = jnp.sin(fr * (_hdot(a, w2_ref[...]) + b2_ref[...]))
    h = _hdot(a, w3_ref[0])
    window = jnp.exp(-z[:, 0:1] * dl_ref[...])
    row = r * z.shape[0] + lax.broadcasted_iota(jnp.int32, h.shape, 0)
    o_ref[...] = jnp.where(row == n, 0.0, h * window)


def _circular_features(r, n):
    p = jnp.where(r < n, r, 2 * n - r)
    p = jnp.where(r == n, 0, p).astype(F32)
    t = p / (n - 1)
    omega = (2.0 * math.pi / n) * p
    bands = jnp.linspace(1e-4, FILTER_BANDS - 1, FILTER_BANDS, dtype=F32).reshape((-1,) + (1,) * r.ndim)
    phase = omega[None] * bands
    return jnp.concatenate([t[None], jnp.cos(phase), -jnp.sin(phase)], axis=0)


def _filter_features(n):
    z = _circular_features(jnp.arange(2 * n, dtype=jnp.int32), n).T
    return jnp.pad(z, ((0, 0), (0, LANES - z.shape[1])))


def _hyena_filter(zc, w1p, b1, w2, b2, w3r, freq, deltas, *, n):
    rt = min(512, n)
    width = HYENA_ORDER * HYENA_WIDTH
    const = lambda shape: pl.BlockSpec(shape, lambda r: (0,) * len(shape))
    return pl.pallas_call(
        functools.partial(_filter_kernel, n=n),
        out_shape=jax.ShapeDtypeStruct((2 * n, width), F32),
        grid=(2 * n // rt,),
        in_specs=[pl.BlockSpec((rt, LANES), lambda r: (r, 0)), const(w1p.shape), const(b1.shape), const(w2.shape),
                  const(b2.shape), pl.BlockSpec((1, FILTER_HIDDEN, width), lambda r: (r // (n // rt), 0, 0)),
                  const(freq.shape), const(deltas.shape)],
        out_specs=pl.BlockSpec((rt, width), lambda r: (r, 0)),
        compiler_params=_cparams(("parallel",)),
        name="hyena_filter",
    )(zc, w1p, b1, w2, b2, w3r, freq, deltas)


FEAT_PAD = 40


def _filter_hidden_kernel(zt_ref, w1t_ref, b1_ref, w2t_ref, b2_ref, fr_ref, o_ref):
    fr = fr_ref[...]
    for q in range(zt_ref.shape[0]):
        a = jnp.sin(fr * (_hdot(w1t_ref[...], zt_ref[q]) + b1_ref[...]))
        o_ref[q] = jnp.sin(fr * (_hdot(w2t_ref[...], a) + b2_ref[...]))


def _filter_hidden(zt, w1t, b1, w2t, b2, fr):
    n_blk, _, lanes = zt.shape
    sp = min(8, n_blk)
    const = lambda a: pl.BlockSpec(a.shape, lambda j: (0,) * a.ndim)
    return pl.pallas_call(
        _filter_hidden_kernel,
        out_shape=jax.ShapeDtypeStruct((n_blk, FILTER_HIDDEN, lanes), F32),
        grid=(n_blk // sp,),
        in_specs=[pl.BlockSpec((sp, FEAT_PAD, lanes), lambda j: (j, 0, 0)), const(w1t), const(b1), const(w2t),
                  const(b2), const(fr)],
        out_specs=pl.BlockSpec((sp, FILTER_HIDDEN, lanes), lambda j: (j, 0, 0)),
        compiler_params=_cparams(("parallel",)),
        name="hyena_filter_hidden",
    )(zt, w1t, b1, w2t, b2, fr)


def _filter_dft_kernel(zt_ref, hid_ref, w3_ref, e_ref, m1_ref, o_ref, *, st):
    j = pl.program_id(0)
    n1 = zt_ref.shape[2]
    width = w3_ref.shape[1]
    row = lax.broadcasted_iota(jnp.int32, (n1, width), 0)
    first_half = lax.broadcasted_iota(jnp.int32, (3 * FILTER_HIDDEN, n1), 1) < n1 // 2
    zero = jnp.zeros((), BF16)
    contract0 = (((0,), (0,)), ((), ()))
    cols = []
    for q in range(st):
        zt = zt_ref[q]
        a = hid_ref[q]
        a_hi = a.astype(BF16)
        a_lo = (a - a_hi.astype(F32)).astype(BF16)
        stack = jnp.concatenate([a_hi, a_hi, a_lo], axis=0)
        lhs = jnp.concatenate([jnp.where(first_half, stack, zero), jnp.where(first_half, zero, stack)], axis=0)
        h = lax.dot_general(lhs, w3_ref[...], contract0, preferred_element_type=F32)
        decay = lax.dot_general(zt.astype(BF16), e_ref[...], contract0, preferred_element_type=F32)
        h = h * jnp.exp(-decay)
        h = jnp.where((row == n1 // 2) & (j * st + q == 0), 0.0, h)
        cols.append(h.astype(BF16))
    o_ref[...] = jnp.dot(m1_ref[...], jnp.concatenate(cols, axis=1), preferred_element_type=F32).astype(BF16)


def _filter_outer_dft(zt, hid, w3r, e, m1):
    n_s, _, n1 = zt.shape
    width = w3r.shape[1]
    st = min(8, n_s)
    const = lambda a: pl.BlockSpec(a.shape, lambda j: (0,) * a.ndim)
    return pl.pallas_call(
        functools.partial(_filter_dft_kernel, st=st),
        out_shape=jax.ShapeDtypeStruct((m1.shape[0], n_s * width), BF16),
        grid=(n_s // st,),
        in_specs=[pl.BlockSpec((st, FEAT_PAD, n1), lambda j: (j, 0, 0)),
                  pl.BlockSpec((st, FILTER_HIDDEN, n1), lambda j: (j, 0, 0)), const(w3r), const(e), const(m1)],
        out_specs=pl.BlockSpec((m1.shape[0], st * width), lambda j: (0, j)),
        compiler_params=_cparams(("parallel",)),
        name="hyena_filter_outer_dft",
    )(zt, hid, w3r, e, m1)


def _mirror_hidden(hid_pairs, half):
    npair, nh, _ = hid_pairs.shape
    first = hid_pairs.reshape(npair, nh, 2, half).transpose(0, 2, 1, 3).reshape(2 * npair, nh, half)
    rev = first[:, :, ::-1]
    second = jnp.roll(rev[::-1], 1, axis=0)
    second0 = jnp.concatenate([first[0][:, 0:1], rev[0][:, :half - 1]], axis=-1)
    second = second.at[0].set(second0)
    return jnp.concatenate([first, second], axis=-1)


def _dft_consts(n):
    big = 2 * n
    n1 = big // DFT_INNER
    half = n1 // 2
    k1 = np.arange(n1)[:, None]
    a = np.arange(n1)[None, :]
    ang = -2.0 * np.pi * ((k1 * a) % n1) / n1
    fr, fi = np.cos(ang), np.sin(ang)
    first = np.block([[fr[:, :half], -fi[:, :half]], [fi[:, :half], fr[:, :half]]])
    first_real = np.concatenate([fr, fi], axis=0)
    last = np.block([[fr[:half, :], fi[:half, :]], [-fi[:half, :], fr[:half, :]]])
    k2 = np.arange(DFT_INNER)[:, None]
    s = np.arange(DFT_INNER)[None, :]
    ang2 = -2.0 * np.pi * ((k2 * s) % DFT_INNER) / DFT_INNER
    gr, gi = np.cos(ang2), np.sin(ang2)
    inner = np.block([[gr, -gi], [gi, gr]])
    inner_inv = np.block([[gr, gi], [-gi, gr]])
    prod = (jnp.arange(n1, dtype=jnp.int32)[:, None] * jnp.arange(DFT_INNER, dtype=jnp.int32)[None, :]) % big
    angt = prod.astype(F32) * (-2.0 * math.pi / big)
    tw = jnp.stack([jnp.cos(angt), jnp.sin(angt)])
    tw = jnp.broadcast_to(tw[..., None], tw.shape + (LANES,))
    f = lambda m: jnp.asarray(np.ascontiguousarray(m), F32).astype(BF16)
    return dict(first=f(first), first_real=f(first_real), last=f(last), inner=f(inner), inner_inv=f(inner_inv),
                tw=tw, n1=n1, inv_scale=1.0 / big)


def _left_matmul_kernel(*refs, mode, inv_scale):
    if mode == "first":
        m_ref, x_ref, o_ref = refs
        o_ref[...] = _bdot(m_ref[...], x_ref[0]).astype(BF16)
        return
    if mode == "last":
        m3_ref, b_ref, g_ref, v_ref, bias_ref, z_ref = refs
    else:
        m3_ref, b_ref, g_ref, v_ref, bias_ref, m1_ref, z_ref, a_ref = refs
    y = _bdot(m3_ref[...], b_ref[...]) * inv_scale
    z = g_ref[0].astype(F32) * (y + bias_ref[...] * v_ref[0].astype(F32))
    z_ref[0] = z.astype(z_ref.dtype)
    if mode == "last_first":
        a_ref[...] = _bdot(m1_ref[...], z).astype(BF16)


DFT_COLS = 2048


def _dft_first(m1, x3, part):
    _, rows, wtot = x3.shape
    wt = DFT_COLS
    return pl.pallas_call(
        functools.partial(_left_matmul_kernel, mode="first", inv_scale=None),
        out_shape=jax.ShapeDtypeStruct((m1.shape[0], wtot), BF16),
        grid=(wtot // wt,),
        in_specs=[pl.BlockSpec(m1.shape, lambda j: (0, 0)), pl.BlockSpec((1, rows, wt), lambda j: (part, 0, j))],
        out_specs=pl.BlockSpec((m1.shape[0], wt), lambda j: (0, j)),
        compiler_params=_cparams(("parallel",)),
        name="dft_outer_forward",
    )(m1, x3)


def _dft_last(m3, b2, gate3, gate_part, v3, v_part, bias_flat, inv_scale, m1=None):
    _, rows, wtot = gate3.shape
    wt = DFT_COLS
    col = lambda r: pl.BlockSpec((r, wt), lambda j: (0, j))
    part = lambda p: pl.BlockSpec((1, rows, wt), lambda j: (p, 0, j))
    const = lambda a: pl.BlockSpec(a.shape, lambda j: (0, 0))
    z_shape = jax.ShapeDtypeStruct((1, rows, wtot), BF16 if m1 is None else F32)
    in_specs = [const(m3), col(b2.shape[0]), part(gate_part), part(v_part), pl.BlockSpec((1, wt), lambda j: (0, 0))]
    if m1 is None:
        return pl.pallas_call(
            functools.partial(_left_matmul_kernel, mode="last", inv_scale=inv_scale),
            out_shape=z_shape,
            grid=(wtot // wt,),
            in_specs=in_specs,
            out_specs=part(0),
            compiler_params=_cparams(("parallel",)),
            name="dft_outer_inverse",
        )(m3, b2, gate3, v3, bias_flat)
    return pl.pallas_call(
        functools.partial(_left_matmul_kernel, mode="last_first", inv_scale=inv_scale),
        out_shape=(z_shape, jax.ShapeDtypeStruct((m1.shape[0], wtot), BF16)),
        grid=(wtot // wt,),
        in_specs=in_specs + [const(m1)],
        out_specs=(part(0), col(m1.shape[0])),
        compiler_params=_cparams(("parallel",)),
        name="dft_outer_inverse_forward",
    )(m3, b2, gate3, v3, bias_flat, m1)


def _dft_mid_kernel(a_ref, tw_ref, g_ref, *rest, slabs, with_filter):
    if with_filter:
        h_ref, gi_ref, o_ref = rest
    else:
        (o_ref,) = rest
    nl = a_ref.shape[3] // LANES
    for j in range(slabs):
        twr = jnp.concatenate([tw_ref[0, j]] * nl, axis=1)
        twi = jnp.concatenate([tw_ref[1, j]] * nl, axis=1)
        ar, ai = a_ref[0, j].astype(F32), a_ref[1, j].astype(F32)
        br = ar * twr - ai * twi
        bi = ar * twi + ai * twr
        x = _bdot(g_ref[...], jnp.concatenate([br.astype(BF16), bi.astype(BF16)], axis=0))
        if not with_filter:
            o_ref[0, j] = x[:DFT_INNER].astype(BF16)
            o_ref[1, j] = x[DFT_INNER:].astype(BF16)
            continue
        xr, xi = x[:DFT_INNER], x[DFT_INNER:]
        hr, hi = h_ref[0, j].astype(F32), h_ref[1, j].astype(F32)
        yr = xr * hr - xi * hi
        yi = xr * hi + xi * hr
        zz = _bdot(gi_ref[...], jnp.concatenate([yr.astype(BF16), yi.astype(BF16)], axis=0))
        zr, zi = zz[:DFT_INNER], zz[DFT_INNER:]
        o_ref[0, j] = (zr * twr + zi * twi).astype(BF16)
        o_ref[1, j] = (zi * twr - zr * twi).astype(BF16)


def _dft_mid(a4, tw, inner, spec=None, inner_inv=None, order=0):
    _, n1, _, c = a4.shape
    slabs = 8 if n1 % 8 == 0 else n1
    blk = lambda w: pl.BlockSpec((2, slabs, DFT_INNER, w), lambda j: (0, j, 0, 0))
    const = lambda a: pl.BlockSpec(a.shape, lambda j: (0, 0))
    in_specs = [blk(c), blk(LANES), const(inner)]
    args = [a4, tw, inner]
    if spec is not None:
        in_specs += [pl.BlockSpec((2, slabs, DFT_INNER, c), lambda j: (0, j, 0, order)), const(inner_inv)]
        args += [spec, inner_inv]
    return pl.pallas_call(
        functools.partial(_dft_mid_kernel, slabs=slabs, with_filter=spec is not None),
        out_shape=jax.ShapeDtypeStruct(a4.shape, BF16),
        grid=(n1 // slabs,),
        in_specs=in_specs,
        out_specs=blk(c),
        compiler_params=_cparams(("parallel",)),
        name="dft_inner_filter" if spec is not None else "dft_inner_forward",
    )(*args)


def _hyena_latent(vx, filt_a, bias, consts):
    _, nb, n, c = vx.shape
    assert nb == 2, "the two samples are carried as the real and imaginary part of one transform"
    n1 = consts["n1"]
    wflat = DFT_INNER * c
    flat = vx.reshape(3, nb * n1 // 2, wflat)
    bias_flat = jnp.tile(bias, (1, DFT_COLS // c))
    tw, inner, inner_inv, scale = consts["tw"], consts["inner"], consts["inner_inv"], consts["inv_scale"]

    spec = _dft_mid(filt_a.reshape(2, n1, DFT_INNER, HYENA_ORDER * c), tw, inner)

    a = _dft_first(consts["first"], flat, 0).reshape(2, n1, DFT_INNER, c)
    bm = _dft_mid(a, tw, inner, spec, inner_inv, order=0)
    z, a = _dft_last(consts["last"], bm.reshape(2 * n1, wflat), flat, 1, flat, 0, bias_flat[0:1], scale,
                     consts["first"])
    bm = _dft_mid(a.reshape(2, n1, DFT_INNER, c), tw, inner, spec, inner_inv, order=1)
    y = _dft_last(consts["last"], bm.reshape(2 * n1, wflat), flat, 2, z, 0, bias_flat[1:2], scale)
    return y.reshape(nb, n, c)


def _hyena_ctx_kernel(u_ref, h_ref, fs_ref, ff_ref, fi_ref, bias_ref, o_ref, *, n):
    big = 2 * n
    c = HYENA_WIDTH
    spec = _hdot(fs_ref[...], h_ref[...])
    u = u_ref[...]
    v, x1, x2 = u[:, 0:c], u[:, c:2 * c], u[:, 2 * c:3 * c]

    def conv(zin, order):
        x = _hdot(ff_ref[...], zin)
        xr, xi = x[:big], x[big:]
        hr, hi = spec[:big, order * c:(order + 1) * c], spec[big:, order * c:(order + 1) * c]
        y = jnp.concatenate([xr * hr - xi * hi, xr * hi + xi * hr], axis=0)
        return _hdot(fi_ref[...], y)

    bias = bias_ref[...]
    z = x1 * (conv(v, 0) + bias[0:1] * v)
    o_ref[...] = (x2 * (conv(z, 1) + bias[1:2] * z)).astype(BF16)


def _ctx_dft_consts(n):
    big = 2 * n
    k = np.arange(big)[:, None]
    t = np.arange(big)[None, :]
    ang = -2.0 * np.pi * ((k * t) % big) / big
    fr, fi = np.cos(ang), np.sin(ang)
    spec_m = np.concatenate([fr, fi], axis=0)
    fwd = np.block([[fr[:, :n], -fi[:, :n]], [fi[:, :n], fr[:, :n]]])
    inv = np.block([[fr.T[:n, :], fi.T[:n, :]], [-fi.T[:n, :], fr.T[:n, :]]]) / big
    f = lambda m: jnp.asarray(np.ascontiguousarray(m), F32)
    return f(spec_m), f(fwd), f(inv)


def _hyena_ctx(u_ctx, h_circ, bias, cconsts):
    nb, n, w = u_ctx.shape
    assert nb == 2
    fs, ff, fi = cconsts
    full = lambda a: pl.BlockSpec(a.shape, lambda i: (0,) * a.ndim)
    u2 = u_ctx.reshape(nb * n, w)
    out = pl.pallas_call(
        functools.partial(_hyena_ctx_kernel, n=n),
        out_shape=jax.ShapeDtypeStruct((nb * n, HYENA_WIDTH), BF16),
        grid=(1,),
        in_specs=[full(u2), full(h_circ), full(fs), full(ff), full(fi), full(bias)],
        out_specs=pl.BlockSpec((nb * n, HYENA_WIDTH), lambda i: (0, 0)),
        compiler_params=_cparams(("arbitrary",)),
        name="hyena_context",
    )(u2, h_circ, fs, ff, fi, bias)
    return out.reshape(nb, n, HYENA_WIDTH)


def _gla_kernel(xf_ref, xb_ref, of_ref, ob_ref, st_ref):
    s = pl.program_id(0)

    @pl.when(s == 0)
    def _():
        st_ref[...] = jnp.zeros(st_ref.shape, F32)

    c = GLA_CHUNK
    kw, vw = GLA_K_WIDTH, GLA_V_WIDTH
    ri3 = lax.broadcasted_iota(jnp.int32, (c, 3 * c), 0)
    ci3 = lax.broadcasted_iota(jnp.int32, (c, 3 * c), 1) % c
    kr = lax.broadcasted_iota(jnp.int32, (GLA_HEADS * c, kw), 0) // c
    kc = lax.broadcasted_iota(jnp.int32, (GLA_HEADS * c, kw), 1) // GLA_DK
    mask_k = kr == kc
    vr = lax.broadcasted_iota(jnp.int32, (GLA_HEADS * c, vw), 0) // c
    vc = lax.broadcasted_iota(jnp.int32, (GLA_HEADS * c, vw), 1) // GLA_DV
    mask_v = vr == vc
    sr = lax.broadcasted_iota(jnp.int32, (vw, kw), 0) // GLA_DV
    sc = lax.broadcasted_iota(jnp.int32, (vw, kw), 1) // GLA_DK
    mask_s = sr == sc
    qi = lax.broadcasted_iota(jnp.int32, (c, GLA_HEADS * c), 0)
    kj = lax.broadcasted_iota(jnp.int32, (c, GLA_HEADS * c), 1) % c

    def chunk(x_ref, o_ref, d, b, ch):
        backward = d == 1
        tri3 = ((ri3 <= ci3) if backward else (ri3 >= ci3)).astype(BF16)
        causal = (qi <= kj) if backward else (qi >= kj)
        goff = 2 * kw + vw + (kw if backward else 0)
        r0 = ch * c
        q = x_ref[b, r0:r0 + c, 0:kw]
        k = x_ref[b, r0:r0 + c, kw:2 * kw]
        v = x_ref[b, r0:r0 + c, 2 * kw:2 * kw + vw]
        g = x_ref[b, r0:r0 + c, goff:goff + kw]
        g_hi = g.astype(BF16)
        g_mid = (g - g_hi.astype(F32)).astype(BF16)
        g_lo = (g - g_hi.astype(F32) - g_mid.astype(F32)).astype(BF16)
        gc = jnp.dot(tri3, jnp.concatenate([g_hi, g_mid, g_lo], axis=0), preferred_element_type=F32)
        g_last = gc[0:1] if backward else gc[c - 1:c]
        q_dec = q * jnp.exp(gc)
        k_dec = k * jnp.exp(-gc)
        k_tail = k * jnp.exp(g_last - gc)
        k_blk = jnp.where(mask_k, jnp.concatenate([k_dec] * GLA_HEADS, axis=0), 0.0).astype(BF16)
        v_bf = v.astype(BF16)
        v_blk = jnp.where(mask_v, jnp.concatenate([v_bf] * GLA_HEADS, axis=0), jnp.zeros((), BF16))
        q_bf = q_dec.astype(BF16)
        scores = lax.dot_general(q_bf, k_blk, (((1,), (1,)), ((), ())), preferred_element_type=F32)
        scores = jnp.where(causal, scores, 0.0)
        st = st_ref[d, b]
        o = jnp.dot(scores.astype(BF16), v_blk, preferred_element_type=F32)
        o = o + lax.dot_general(q_bf, st.astype(BF16), (((1,), (1,)), ((), ())), preferred_element_type=F32)
        o_ref[b, r0:r0 + c, :] = o.astype(BF16)
        kv_t = lax.dot_general(v_bf, k_tail.astype(BF16), (((0,), (0,)), ((), ())), preferred_element_type=F32)
        st_ref[d, b] = st * jnp.exp(g_last) + jnp.where(mask_s, kv_t, 0.0)

    n_chunks = xf_ref.shape[1] // c
    for u in range(n_chunks):
        for b in range(xf_ref.shape[0]):
            chunk(xf_ref, of_ref, 0, b, u)
            chunk(xb_ref, ob_ref, 1, b, n_chunks - 1 - u)


def _gla(gla_in, *, ctx_len):
    nb, t, w = gla_in.shape
    tb = TOK_TILE
    nblk = t // tb
    cblk = ctx_len // tb
    fwd = lambda s: (0, s, 0)
    bwd = lambda s: (0, jnp.where(s < cblk, cblk - 1 - s, nblk - 1 - (s - cblk)), 0)
    out = jax.ShapeDtypeStruct((nb, t, GLA_V_WIDTH), BF16)
    return pl.pallas_call(
        _gla_kernel,
        out_shape=(out, out),
        grid=(nblk,),
        in_specs=[pl.BlockSpec((nb, tb, w), fwd), pl.BlockSpec((nb, tb, w), bwd)],
        out_specs=(pl.BlockSpec((nb, tb, GLA_V_WIDTH), fwd), pl.BlockSpec((nb, tb, GLA_V_WIDTH), bwd)),
        scratch_shapes=[pltpu.VMEM((2, nb, GLA_V_WIDTH, GLA_K_WIDTH), F32)],
        compiler_params=_cparams(("arbitrary",)),
        name="gla_scans",
    )(gla_in, gla_in)


def _out_kernel(h_ref, mod_ref, at_ref, hy_ref, gf_ref, gb_ref, gr_ref, gn_ref, gmat_ref, w_ref, o_ref, *,
                n_batch, ctx_tiles, tile_off, d_model):
    b, i = pl.program_id(0), pl.program_id(1)
    mod = _mod_row(mod_ref, i + tile_off < ctx_tiles, b, n_batch)
    gate = mod[:, 2 * d_model:3 * d_model]
    gmat = gmat_ref[...]
    gn = gn_ref[...]
    a0, a1 = ATTN_WIDTH, ATTN_WIDTH + HYENA_WIDTH
    ya = _head_rms(at_ref[0].astype(F32), gn[:, :a0], gmat)
    yh = _head_rms(hy_ref[0].astype(F32), gn[:, a0:a1], gmat)
    yg = _head_rms(gf_ref[0].astype(F32) + gb_ref[0].astype(F32), gn[:, a1:], gmat) * _silu(gr_ref[0].astype(F32))
    y = jnp.concatenate([ya, yh, yg], axis=1)
    o_ref[0] = h_ref[0] + gate * _bdot(y, w_ref[...])


def _out_projection(h, mod, attn, hy, gla_f, gla_b, gr, gn, gmat, w_out, *, ctx_len, skip_ctx):
    nb, t, d = h.shape
    tb = TOK_TILE
    off = ctx_len // tb if skip_ctx else 0
    kern = functools.partial(_out_kernel, n_batch=nb, ctx_tiles=ctx_len // tb, tile_off=off, d_model=d)
    const = lambda a: pl.BlockSpec(a.shape, lambda b, i: (0,) * a.ndim)
    tok = lambda w: pl.BlockSpec((1, tb, w), lambda b, i: (b, i + off, 0))
    return pl.pallas_call(
        kern,
        out_shape=jax.ShapeDtypeStruct((nb, t - off * tb, d), F32),
        grid=(nb, t // tb - off),
        in_specs=[tok(d), const(mod), tok(ATTN_WIDTH), tok(HYENA_WIDTH), tok(GLA_V_WIDTH), tok(GLA_V_WIDTH),
                  tok(GLA_V_WIDTH), const(gn), const(gmat), const(w_out)],
        out_specs=pl.BlockSpec((1, tb, d), lambda b, i: (b, i, 0)),
        compiler_params=_cparams(("parallel", "parallel")),
        name="out_projection",
    )(h, mod, attn, hy, gla_f, gla_b, gr, gn, gmat, w_out)


def _ffn_kernel(h_ref, mod_ref, g2_ref, w1_ref, w3_ref, w2_ref, fg_ref, o_ref, *, n_batch, ctx_tiles, tile_off,
                d_model, final):
    b, i = pl.program_id(0), pl.program_id(1)
    mod = _mod_row(mod_ref, i + tile_off < ctx_tiles, b, n_batch)
    shift, scale, gate = (mod[:, 3 * d_model:4 * d_model], mod[:, 4 * d_model:5 * d_model],
                          mod[:, 5 * d_model:6 * d_model])
    x = h_ref[0]
    xn = x * lax.rsqrt(jnp.mean(jnp.square(x), axis=-1, keepdims=True) + NORM_EPS) * g2_ref[...]
    u = (xn * (1.0 + scale) + shift).astype(BF16)
    a = jnp.dot(u, w1_ref[...], preferred_element_type=F32)
    c = jnp.dot(u, w3_ref[...], preferred_element_type=F32)
    hid = (_silu(a) * c).astype(BF16)
    y = x + gate * jnp.dot(hid, w2_ref[...], preferred_element_type=F32)
    if final:
        y = y * lax.rsqrt(jnp.mean(jnp.square(y), axis=-1, keepdims=True) + NORM_EPS) * fg_ref[...]
    o_ref[0] = y


def _ffn(h, mod, g2, w1, w3, w2, fg, *, ctx_len, lat_only, final):
    nb, t, d = h.shape
    tb = TOK_TILE
    off = ctx_len // tb if lat_only else 0
    kern = functools.partial(_ffn_kernel, n_batch=nb, ctx_tiles=ctx_len // tb, tile_off=off, d_model=d, final=final)
    const = lambda a: pl.BlockSpec(a.shape, lambda b, i: (0,) * a.ndim)
    tok = pl.BlockSpec((1, tb, d), lambda b, i: (b, i, 0))
    return pl.pallas_call(
        kern,
        out_shape=jax.ShapeDtypeStruct((nb, t, d), F32),
        grid=(nb, t // tb),
        in_specs=[tok, const(mod), const(g2), const(w1), const(w3), const(w2), const(fg)],
        out_specs=tok,
        compiler_params=_cparams(("parallel", "parallel")),
        name="swiglu_ffn",
    )(h, mod, g2, w1, w3, w2, fg)


def _rope_tables(n_lat, ctx_len):
    rows = n_lat // GRID_W
    row = jnp.broadcast_to(jnp.arange(rows, dtype=F32)[:, None], (rows, GRID_W)).reshape(-1)
    col = jnp.broadcast_to(jnp.arange(GRID_W, dtype=F32)[None, :], (rows, GRID_W)).reshape(-1)
    n_freq = HEAD_DIM // 4
    inv_freq = jnp.power(ROPE_THETA, -jnp.arange(n_freq, dtype=F32) / n_freq)
    ang = jnp.concatenate([row[:, None] * inv_freq, col[:, None] * inv_freq], axis=-1)
    cos, sin = jnp.cos(ang), jnp.sin(ang)
    cos = jnp.concatenate([jnp.ones((ctx_len, HEAD_DIM // 2), F32), cos], axis=0)
    sin = jnp.concatenate([jnp.zeros((ctx_len, HEAD_DIM // 2), F32), sin], axis=0)
    cosf = jnp.tile(cos, (1, 2 * LANES // HEAD_DIM))
    sinf = jnp.tile(jnp.concatenate([-sin, sin], axis=1), (1, LANES // HEAD_DIM))
    return cosf, sinf


def kernel(x, c, ctx, c_ctx, ada_w, ada_b, norm1_g, w_in, q_norm_g, k_norm_g, hy_conv_w, hy_conv_b, filt_w1, filt_b1, filt_w2, filt_b2, filt_w3, filt_freq, hy_bias, gla_gate_w, gla_gate_b, out_norm_g, w_out, norm2_g, ffn_w1, ffn_w3, ffn_w2, final_norm_g):
    nb, n_lat, d = x.shape
    ctx_len = ctx.shape[1]
    depth = w_in.shape[0]
    assert nb == 2 and ctx_len % TOK_TILE == 0 and n_lat % TOK_TILE == 0 and n_lat % DFT_INNER == 0

    h = jnp.concatenate([ctx, x], axis=1)
    cond = jnp.concatenate([c, c_ctx[None, :], jnp.zeros((8 - nb - 1, d), F32)], axis=0)
    mods = _modulation(cond, ada_w, ada_b)

    cosf, sinf = _rope_tables(n_lat, ctx_len)
    lane = np.arange(LANES)
    gmat = jnp.asarray(np.tile((lane[:, None] // HEAD_DIM) == (lane[None, :] // HEAD_DIM), (2, 1)), BF16)
    consts = _dft_consts(n_lat)
    cconsts = _ctx_dft_consts(ctx_len)
    zc_ctx = _filter_features(ctx_len)
    deltas = jnp.abs(jnp.linspace(HYENA_MIN_DECAY, HYENA_MAX_DECAY, HYENA_WIDTH, dtype=F32))
    deltas = jnp.tile(deltas[None, :], (1, HYENA_ORDER))
    n1 = consts["n1"]
    rows_sa = (DFT_INNER * jnp.arange(n1, dtype=jnp.int32)[None, :] + jnp.arange(DFT_INNER, dtype=jnp.int32)[:, None])
    zt_lat = _circular_features(rows_sa, n_lat).transpose(1, 0, 2)
    t_hi, t_mid, t_lo = _split3(zt_lat[:, 0, :])
    d_hi, d_mid, d_lo = _split3(deltas[0])
    zt_lat = jnp.concatenate([zt_lat] + [p[:, None, :] for p in (t_hi, t_hi, t_hi, t_mid, t_mid, t_lo)], axis=1)
    n_feat = zt_lat.shape[1]
    zt_lat = jnp.pad(zt_lat, ((0, 0), (0, FEAT_PAD - n_feat), (0, 0)))
    decay_rows = jnp.zeros((FEAT_PAD, deltas.shape[1]), F32).at[n_feat - 6:n_feat].set(
        jnp.stack([d_hi, d_mid, d_lo, d_hi, d_mid, d_hi])).astype(BF16)
    lanes_n1 = lambda vec: jnp.broadcast_to(vec[:, None], (vec.shape[0], n1))
    lane_i = jnp.arange(n1, dtype=jnp.int32)[None, :]
    pair_i = jnp.arange(DFT_INNER // 2, dtype=jnp.int32)[:, None]
    rows_pair = DFT_INNER * (lane_i % (n1 // 2)) + 2 * pair_i + lane_i // (n1 // 2)
    zt_pairs = _circular_features(rows_pair, n_lat).transpose(1, 0, 2)
    zt_pairs = jnp.pad(zt_pairs, ((0, 0), (0, FEAT_PAD - zt_pairs.shape[1]), (0, 0)))

    w_in_b = jnp.pad(w_in, ((0, 0), (0, 0), (0, IN_PAD - w_in.shape[2]))).astype(BF16)
    w_out_b = w_out.astype(BF16)
    w1_b, w3_b, w2_b = ffn_w1.astype(BF16), ffn_w3.astype(BF16), ffn_w2.astype(BF16)
    fg = final_norm_g.reshape(1, d)

    out = None
    for l in range(depth):
        last = l == depth - 1
        mod = mods[l]
        wg = jnp.zeros((LANES, 2 * GLA_K_WIDTH), F32)
        wg = wg.at[0:GLA_GATE_RANK, 0:GLA_K_WIDTH].set(gla_gate_w[l, 0])
        wg = wg.at[GLA_GATE_RANK:2 * GLA_GATE_RANK, GLA_K_WIDTH:].set(gla_gate_w[l, 1])
        bg = gla_gate_b[l].reshape(1, 2 * GLA_K_WIDTH)
        qg = jnp.tile(q_norm_g[l][None, :], (1, ATTN_HEADS))
        kg = jnp.tile(k_norm_g[l][None, :], (1, ATTN_KV_HEADS))

        q, k, v, hy, gla_in, gr = _in_projection(h, mod, norm1_g[l].reshape(1, d), w_in_b[l], qg, kg, cosf, sinf,
                                                 gmat, wg.astype(BF16), bg, ctx_len=ctx_len)
        attn = _attention(q, k, v, ctx_len=ctx_len)

        vx, u_ctx = _short_conv(hy, hy_conv_w[l], hy_conv_b[l], ctx_len=ctx_len)
        w1p = jnp.pad(filt_w1[l], ((0, LANES - filt_w1.shape[1]), (0, 0)))
        w3r = filt_w3[l].reshape(FILTER_HIDDEN, HYENA_ORDER, 2, HYENA_WIDTH).transpose(2, 0, 1, 3)
        w3r = w3r.reshape(2, FILTER_HIDDEN, HYENA_ORDER * HYENA_WIDTH)
        fargs = (w1p, filt_b1[l][None, :], filt_w2[l], filt_b2[l][None, :], w3r, filt_freq[l][None, :], deltas)
        w1t = jnp.pad(filt_w1[l].T, ((0, 0), (0, FEAT_PAD - filt_w1.shape[1])))
        w3_hi, w3_lo, _ = _split3(w3r)
        w3_stack = jnp.concatenate([w3_hi, w3_lo, w3_hi], axis=1).reshape(-1, w3r.shape[2]).astype(BF16)
        hid = _filter_hidden(zt_pairs, w1t, lanes_n1(filt_b1[l]), filt_w2[l].T, lanes_n1(filt_b2[l]),
                             lanes_n1(filt_freq[l]))
        filt_a = _filter_outer_dft(zt_lat, _mirror_hidden(hid, n1 // 2), w3_stack, decay_rows, consts["first_real"])
        hy_lat = _hyena_latent(vx, filt_a, hy_bias[l], consts)
        if not last:
            h_ctx = _hyena_filter(zc_ctx, *fargs, n=ctx_len)
            hy_ctx = _hyena_ctx(u_ctx, h_ctx, hy_bias[l], cconsts)
        else:
            hy_ctx = jnp.zeros((nb, ctx_len, HYENA_WIDTH), BF16)
        hy_out = jnp.concatenate([hy_ctx, hy_lat], axis=1)

        gla_f, gla_b = _gla(gla_in, ctx_len=ctx_len)

        h = _out_projection(h, mod, attn, hy_out, gla_f, gla_b, gr, out_norm_g[l].reshape(1, -1), gmat, w_out_b[l],
                            ctx_len=ctx_len, skip_ctx=last)
        h = _ffn(h, mod, norm2_g[l].reshape(1, d), w1_b[l], w3_b[l], w2_b[l], fg, ctx_len=ctx_len, lat_only=last,
                 final=last)
    return h
```

```python
import functools
import math

import numpy as np
import jax
import jax.numpy as jnp
from jax import lax
from jax.experimental import pallas as pl
from jax.experimental.pallas import tpu as pltpu

F32 = jnp.float32
BF16 = jnp.bfloat16
HI = lax.Precision.HIGHEST

LANES = 128
VMEM_LIMIT = 56 * 1024 * 1024

HEAD_DIM = 64
ATTN_HEADS = 8
ATTN_KV_HEADS = 2
ATTN_WIDTH = ATTN_HEADS * HEAD_DIM
KV_WIDTH = ATTN_KV_HEADS * HEAD_DIM
HYENA_WIDTH = 256
HYENA_ORDER = 2
FILTER_BANDS = 16
FILTER_HIDDEN = 64
GLA_HEADS = 4
GLA_DK = 32
GLA_DV = 64
GLA_K_WIDTH = GLA_HEADS * GLA_DK
GLA_V_WIDTH = GLA_HEADS * GLA_DV
GLA_GATE_RANK = 16
GLA_GATE_TAU = 16.0
GLA_CHUNK = 64
GRID_W = 64
ROPE_THETA = 10000.0
N_MOD = 6
NORM_EPS = 1e-6
HYENA_TARGET = 1e-2
HYENA_MIN_DECAY = math.log(HYENA_TARGET) / 1.5
HYENA_MAX_DECAY = math.log(HYENA_TARGET) / 0.3

TOK_TILE = 256
DFT_INNER = 128
IN_PAD = 2432

_OQ, _OK, _OV, _OHY = 0, 512, 640, 768
_OGQ, _OGK, _OGV, _OGR, _OGA = 1536, 1664, 1792, 2048, 2304


def _cparams(sem, vmem=VMEM_LIMIT):
    return pltpu.CompilerParams(dimension_semantics=sem, vmem_limit_bytes=vmem)


def _bdot(a, b):
    return jnp.dot(a.astype(BF16), b.astype(BF16), preferred_element_type=F32)


def _hdot(a, b):
    return jnp.dot(a, b, preferred_element_type=F32, precision=HI)


def _group_mean_square(x, gmat):
    outs = []
    for j in range(x.shape[1] // LANES):
        sq = jnp.square(x[:, j * LANES:(j + 1) * LANES])
        hi = sq.astype(BF16)
        lo = (sq - hi.astype(F32)).astype(BF16)
        s = jnp.dot(jnp.concatenate([hi, lo], axis=1), gmat, preferred_element_type=F32)
        outs.append(s * (1.0 / HEAD_DIM))
    return outs[0] if len(outs) == 1 else jnp.concatenate(outs, axis=1)


def _head_rms(x, gain, gmat):
    ms = _group_mean_square(x, gmat)
    return x * lax.rsqrt(ms + NORM_EPS) * gain


def _mod_row(mod_ref, is_ctx, b, n_batch):
    row = jnp.where(is_ctx, n_batch, b)
    return mod_ref[pl.ds(row, 1), :]


def _silu(x):
    return x * (1.0 / (1.0 + jnp.exp(-x)))


def _split3(x):
    hi = x.astype(BF16).astype(F32)
    mid = (x - hi).astype(BF16).astype(F32)
    lo = (x - hi - mid).astype(BF16).astype(F32)
    return hi, mid, lo


def _mod_kernel(c_ref, w_ref, b_ref, o_ref):
    s = _silu(c_ref[...])
    o_ref[0] = _bdot(s, w_ref[0]) + b_ref[0]


def _modulation(cond_rows, ada_w, ada_b):
    depth, d, w = ada_w.shape
    ct = 1536
    return pl.pallas_call(
        _mod_kernel,
        out_shape=jax.ShapeDtypeStruct((depth, 8, w), F32),
        grid=(depth, w // ct),
        in_specs=[pl.BlockSpec((8, d), lambda l, j: (0, 0)),
                  pl.BlockSpec((1, d, ct), lambda l, j: (l, 0, j)),
                  pl.BlockSpec((1, 1, ct), lambda l, j: (l, 0, j))],
        out_specs=pl.BlockSpec((1, 8, ct), lambda l, j: (l, 0, j)),
        compiler_params=_cparams(("arbitrary", "arbitrary")),
        name="adaln_modulation",
    )(cond_rows, ada_w, ada_b.reshape(depth, 1, w))


def _in_kernel(h_ref, mod_ref, g1_ref, w_ref, qg_ref, kg_ref, cos_ref, sin_ref, gmat_ref, wg_ref, bg_ref,
               q_ref, k_ref, v_ref, hy_ref, gla_ref, gr_ref, *, n_batch, ctx_tiles, d_model):
    b, i = pl.program_id(0), pl.program_id(1)
    mod = _mod_row(mod_ref, i < ctx_tiles, b, n_batch)
    shift, scale = mod[:, 0:d_model], mod[:, d_model:2 * d_model]
    x = h_ref[0]
    xn = x * lax.rsqrt(jnp.mean(jnp.square(x), axis=-1, keepdims=True) + NORM_EPS) * g1_ref[...]
    u = xn * (1.0 + scale) + shift
    proj = _bdot(u, w_ref[...])

    gmat = gmat_ref[...]
    cosf, sinf = cos_ref[...], sin_ref[...]
    lane = lax.broadcasted_iota(jnp.int32, (x.shape[0], LANES), 1)
    first_half = (lane % HEAD_DIM) < (HEAD_DIM // 2)
    low_head = lane < HEAD_DIM

    def rope(t):
        partner = jnp.where(first_half, pltpu.roll(t, LANES - HEAD_DIM // 2, 1), pltpu.roll(t, HEAD_DIM // 2, 1))
        return t * cosf + partner * sinf

    qn = _head_rms(proj[:, _OQ:_OQ + ATTN_WIDTH], qg_ref[...], gmat)
    zero = jnp.zeros((x.shape[0], LANES), F32)
    for p in range(ATTN_HEADS // 2):
        t = rope(qn[:, p * LANES:(p + 1) * LANES]) * (HEAD_DIM ** -0.5 * math.log2(math.e))
        tr = pltpu.roll(t, HEAD_DIM, 1)
        if 2 * p < ATTN_HEADS // ATTN_KV_HEADS:
            q_ref[0, 2 * p] = jnp.where(low_head, t, zero).astype(BF16)
            q_ref[0, 2 * p + 1] = jnp.where(low_head, tr, zero).astype(BF16)
        else:
            q_ref[0, 2 * p] = jnp.where(low_head, zero, tr).astype(BF16)
            q_ref[0, 2 * p + 1] = jnp.where(low_head, zero, t).astype(BF16)
    kn = _head_rms(proj[:, _OK:_OK + KV_WIDTH], kg_ref[...], gmat)
    k_ref[0] = rope(kn).astype(BF16)
    v_ref[0] = proj[:, _OV:_OV + KV_WIDTH].astype(BF16)
    hy_ref[0] = proj[:, _OHY:_OHY + 3 * HYENA_WIDTH].astype(BF16)

    zg = _bdot(proj[:, _OGA:_OGA + LANES], wg_ref[...]) + bg_ref[...]
    log_gate = (jnp.minimum(zg, 0.0) - jnp.log(1.0 + jnp.exp(-jnp.abs(zg)))) * (1.0 / GLA_GATE_TAU)
    gla_ref[0, :, 0:GLA_K_WIDTH] = proj[:, _OGQ:_OGQ + GLA_K_WIDTH] * (GLA_DK ** -0.5)
    gla_ref[0, :, GLA_K_WIDTH:2 * GLA_K_WIDTH + GLA_V_WIDTH] = proj[:, _OGK:_OGK + GLA_K_WIDTH + GLA_V_WIDTH]
    gla_ref[0, :, 2 * GLA_K_WIDTH + GLA_V_WIDTH:] = log_gate
    gr_ref[0] = proj[:, _OGR:_OGR + GLA_V_WIDTH].astype(BF16)


def _in_projection(h, mod, g1, w_in, qg, kg, cosf, sinf, gmat, wg, bg, *, ctx_len):
    nb, t, d = h.shape
    tb = TOK_TILE
    kern = functools.partial(_in_kernel, n_batch=nb, ctx_tiles=ctx_len // tb, d_model=d)
    const = lambda shape: pl.BlockSpec(shape, lambda b, i: (0,) * len(shape))
    tok = lambda w: pl.BlockSpec((1, tb, w), lambda b, i: (b, i, 0))
    gla_w = 2 * GLA_K_WIDTH + GLA_V_WIDTH + 2 * GLA_K_WIDTH
    return pl.pallas_call(
        kern,
        out_shape=(jax.ShapeDtypeStruct((nb, ATTN_HEADS, t, LANES), BF16),
                   jax.ShapeDtypeStruct((nb, t, KV_WIDTH), BF16),
                   jax.ShapeDtypeStruct((nb, t, KV_WIDTH), BF16),
                   jax.ShapeDtypeStruct((nb, t, 3 * HYENA_WIDTH), BF16),
                   jax.ShapeDtypeStruct((nb, t, gla_w), F32),
                   jax.ShapeDtypeStruct((nb, t, GLA_V_WIDTH), BF16)),
        grid=(nb, t // tb),
        in_specs=[tok(d), const(mod.shape), const(g1.shape), const(w_in.shape), const(qg.shape), const(kg.shape),
                  pl.BlockSpec((tb, LANES), lambda b, i: (i, 0)), pl.BlockSpec((tb, LANES), lambda b, i: (i, 0)),
                  const(gmat.shape), const(wg.shape), const(bg.shape)],
        out_specs=(pl.BlockSpec((1, ATTN_HEADS, tb, LANES), lambda b, i: (b, 0, i, 0)),
                   tok(KV_WIDTH), tok(KV_WIDTH), tok(3 * HYENA_WIDTH), tok(gla_w), tok(GLA_V_WIDTH)),
        compiler_params=_cparams(("parallel", "parallel")),
        name="in_projection",
    )(h, mod, g1, w_in, qg, kg, cosf, sinf, gmat, wg, bg)


KV_UNROLL = 4


def _attn_kernel(q_ref, k_ref, v_ref, o_ref, m_sc, acc_sc, s_sc, mx_sc, p_sc, *, ctx_len, ctx_tiles, kv_tile,
                 n_kv_tiles):
    i = pl.program_id(1)
    tq = q_ref.shape[2]
    per_kv = ATTN_HEADS // ATTN_KV_HEADS
    grp = per_kv * tq
    m_sc[...] = jnp.full(m_sc.shape, -jnp.inf, F32)
    acc_sc[...] = jnp.zeros(acc_sc.shape, F32)

    def scores(start, size, buf):
        q = q_ref[0].reshape(ATTN_HEADS * tq, LANES)
        kc = k_ref[0, pl.ds(start, size), :]
        s = lax.dot_general(q, kc, (((1,), (1,)), ((), ())), preferred_element_type=F32)
        s_sc[buf, :, 0:size] = s
        mx_sc[buf] = jnp.broadcast_to(jnp.max(s, axis=1, keepdims=True), mx_sc.shape[1:])

    def consume(start, size, buf, pbuf):
        vc = v_ref[0, pl.ds(start, size), :]
        low = lax.broadcasted_iota(jnp.int32, vc.shape, 1) < HEAD_DIM
        one = jnp.ones((), BF16)
        v_ones = (jnp.where(low, vc, one), jnp.where(low, one, vc))
        for g in range(ATTN_KV_HEADS):
            alphas = []
            for hh in range(g * per_kv, (g + 1) * per_kv):
                rows = slice(hh * tq, (hh + 1) * tq)
                m_prev = m_sc[rows, :]
                m_new = jnp.maximum(m_prev, mx_sc[buf, rows, :])
                alphas.append(jnp.exp2(m_prev - m_new))
                for j in range(size // LANES):
                    cols = slice(j * LANES, (j + 1) * LANES)
                    p_sc[pbuf, rows, cols] = jnp.exp2(s_sc[buf, rows, cols] - m_new).astype(BF16)
                m_sc[rows, :] = m_new
            rows_g = slice(g * grp, (g + 1) * grp)
            pv = jnp.dot(p_sc[pbuf, rows_g, 0:size], v_ones[g], preferred_element_type=F32)
            acc_sc[rows_g, :] = jnp.concatenate(alphas, axis=0) * acc_sc[rows_g, :] + pv

    @pl.when(i < ctx_tiles)
    def _():
        scores(0, ctx_len, 0)
        consume(0, ctx_len, 0, 0)

    @pl.when(i >= ctx_tiles)
    def _():
        def tile_start(j):
            return pl.multiple_of(j * kv_tile, kv_tile)

        def run(j0, count, tail):
            for u in range(count):
                if not (tail and u + 1 == count):
                    scores(tile_start(j0 + u + 1), kv_tile, (u + 1) % 2)
                consume(tile_start(j0 + u), kv_tile, u % 2, u)

        trips = (n_kv_tiles - 1) // KV_UNROLL
        scores(0, kv_tile, 0)
        lax.fori_loop(0, trips, lambda k, c: (run(k * KV_UNROLL, KV_UNROLL, False), c)[1], 0)
        run(trips * KV_UNROLL, n_kv_tiles - trips * KV_UNROLL, True)

    low_head = lax.broadcasted_iota(jnp.int32, (tq, LANES), 1) < HEAD_DIM
    for p in range(ATTN_HEADS // 2):
        outs = []
        for hh in (2 * p, 2 * p + 1):
            acc = acc_sc[hh * tq:(hh + 1) * tq, :]
            o = acc / pltpu.roll(acc, HEAD_DIM, 1)
            src_low = hh < per_kv
            dst_low = hh % 2 == 0
            outs.append(o if src_low == dst_low else pltpu.roll(o, HEAD_DIM, 1))
        o_ref[0, :, p * LANES:(p + 1) * LANES] = jnp.where(low_head, outs[0], outs[1]).astype(BF16)


def _attention(q, k, v, *, ctx_len):
    nb, nh, t, _ = q.shape
    tq = TOK_TILE
    n_lat = t - ctx_len
    kv_tile = max(w for w in (256, 512, 768, 1024) if t % w == 0)
    assert ctx_len <= kv_tile
    rows = nh * tq
    kern = functools.partial(_attn_kernel, ctx_len=ctx_len, ctx_tiles=ctx_len // tq, kv_tile=kv_tile,
                             n_kv_tiles=t // kv_tile)
    return pl.pallas_call(
        kern,
        out_shape=jax.ShapeDtypeStruct((nb, t, ATTN_WIDTH), BF16),
        grid=(nb, t // tq),
        in_specs=[pl.BlockSpec((1, nh, tq, LANES), lambda b, i: (b, 0, i, 0)),
                  pl.BlockSpec((1, t, KV_WIDTH), lambda b, i: (b, 0, 0)),
                  pl.BlockSpec((1, t, KV_WIDTH), lambda b, i: (b, 0, 0))],
        out_specs=pl.BlockSpec((1, tq, ATTN_WIDTH), lambda b, i: (b, i, 0)),
        scratch_shapes=[pltpu.VMEM((rows, LANES), F32), pltpu.VMEM((rows, LANES), F32),
                        pltpu.VMEM((2, rows, kv_tile), F32), pltpu.VMEM((2, rows, LANES), F32),
                        pltpu.VMEM((KV_UNROLL, rows, kv_tile), BF16)],
        compiler_params=_cparams(("parallel", "arbitrary")),
        name="gqa_attention",
    )(q, k, v)


def _short_conv_kernel(x_ref, w_ref, b_ref, lat_ref, ctx_ref, *, ctx_len):
    x = x_ref[0].astype(F32)
    t = x.shape[0]
    row = lax.broadcasted_iota(jnp.int32, x.shape, 0)
    prev = jnp.where((row == 0) | (row == ctx_len), 0.0, pltpu.roll(x, 1, 0))
    nxt = jnp.where((row == ctx_len - 1) | (row == t - 1), 0.0, pltpu.roll(x, t - 1, 0))
    w = w_ref[...]
    u = prev * w[0:1, :] + x * w[1:2, :] + nxt * w[2:3, :] + b_ref[...]
    ctx_ref[0] = u[:ctx_len]
    lat_ref[0, 0] = u[ctx_len:].astype(BF16)


def _short_conv(hy, conv_w, conv_b, *, ctx_len):
    nb, t, w = hy.shape
    per = HYENA_WIDTH // LANES
    return pl.pallas_call(
        functools.partial(_short_conv_kernel, ctx_len=ctx_len),
        out_shape=(jax.ShapeDtypeStruct((3, nb, t - ctx_len, HYENA_WIDTH), BF16),
                   jax.ShapeDtypeStruct((nb, ctx_len, w), F32)),
        grid=(nb, w // LANES),
        in_specs=[pl.BlockSpec((1, t, LANES), lambda b, j: (b, 0, j)),
                  pl.BlockSpec((3, LANES), lambda b, j: (0, j)),
                  pl.BlockSpec((1, LANES), lambda b, j: (0, j))],
        out_specs=(pl.BlockSpec((1, 1, t - ctx_len, LANES), lambda b, j: (j // per, b, 0, j % per)),
                   pl.BlockSpec((1, ctx_len, LANES), lambda b, j: (b, 0, j))),
        compiler_params=_cparams(("parallel", "parallel")),
        name="hyena_short_conv",
    )(hy, conv_w, conv_b.reshape(1, w))


def _filter_kernel(z_ref, w1_ref, b1_ref, w2_ref, b2_ref, w3_ref, fr_ref, dl_ref, o_ref, *, n):
    r = pl.program_id(0)
    z = z_ref[...]
    fr = fr_ref[...]
    a = jnp.sin(fr * (_hdot(z, w1_ref[...]) + b1_ref[...]))
    a = jnp.sin(fr * (_hdot(a, w2_ref[...]) + b2_ref[...]))
    h = _hdot(a, w3_ref[0])
    window = jnp.exp(-z[:, 0:1] * dl_ref[...])
    row = r * z.shape[0] + lax.broadcasted_iota(jnp.int32, h.shape, 0)
    o_ref[...] = jnp.where(row == n, 0.0, h * window)


def _circular_features(r, n):
    p = jnp.where(r < n, r, 2 * n - r)
    p = jnp.where(r == n, 0, p).astype(F32)
    t = p / (n - 1)
    omega = (2.0 * math.pi / n) * p
    bands = jnp.linspace(1e-4, FILTER_BANDS - 1, FILTER_BANDS, dtype=F32).reshape((-1,) + (1,) * r.ndim)
    phase = omega[None] * bands
    return jnp.concatenate([t[None], jnp.cos(phase), -jnp.sin(phase)], axis=0)


def _filter_features(n):
    z = _circular_features(jnp.arange(2 * n, dtype=jnp.int32), n).T
    return jnp.pad(z, ((0, 0), (0, LANES - z.shape[1])))


def _hyena_filter(zc, w1p, b1, w2, b2, w3r, freq, deltas, *, n):
    rt = min(512, n)
    width = HYENA_ORDER * HYENA_WIDTH
    const = lambda shape: pl.BlockSpec(shape, lambda r: (0,) * len(shape))
    return pl.pallas_call(
        functools.partial(_filter_kernel, n=n),
        out_shape=jax.ShapeDtypeStruct((2 * n, width), F32),
        grid=(2 * n // rt,),
        in_specs=[pl.BlockSpec((rt, LANES), lambda r: (r, 0)), const(w1p.shape), const(b1.shape), const(w2.shape),
                  const(b2.shape), pl.BlockSpec((1, FILTER_HIDDEN, width), lambda r: (r // (n // rt), 0, 0)),
                  const(freq.shape), const(deltas.shape)],
        out_specs=pl.BlockSpec((rt, width), lambda r: (r, 0)),
        compiler_params=_cparams(("parallel",)),
        name="hyena_filter",
    )(zc, w1p, b1, w2, b2, w3r, freq, deltas)


FEAT_PAD = 40


def _filter_hidden_kernel(zt_ref, w1t_ref, b1_ref, w2t_ref, b2_ref, fr_ref, o_ref):
    fr = fr_ref[...]
    for q in range(zt_ref.shape[0]):
        a = jnp.sin(fr * (_hdot(w1t_ref[...], zt_ref[q]) + b1_ref[...]))
        o_ref[q] = jnp.sin(fr * (_hdot(w2t_ref[...], a) + b2_ref[...]))


def _filter_hidden(zt, w1t, b1, w2t, b2, fr):
    n_blk, _, lanes = zt.shape
    sp = min(8, n_blk)
    const = lambda a: pl.BlockSpec(a.shape, lambda j: (0,) * a.ndim)
    return pl.pallas_call(
        _filter_hidden_kernel,
        out_shape=jax.ShapeDtypeStruct((n_blk, FILTER_HIDDEN, lanes), F32),
        grid=(n_blk // sp,),
        in_specs=[pl.BlockSpec((sp, FEAT_PAD, lanes), lambda j: (j, 0, 0)), const(w1t), const(b1), const(w2t),
                  const(b2), const(fr)],
        out_specs=pl.BlockSpec((sp, FILTER_HIDDEN, lanes), lambda j: (j, 0, 0)),
        compiler_params=_cparams(("parallel",)),
        name="hyena_filter_hidden",
    )(zt, w1t, b1, w2t, b2, fr)


def _filter_dft_kernel(zt_ref, hid_ref, w3_ref, e_ref, m1_ref, o_ref, *, st):
    j = pl.program_id(0)
    n1 = zt_ref.shape[2]
    width = w3_ref.shape[1]
    row = lax.broadcasted_iota(jnp.int32, (n1, width), 0)
    first_half = lax.broadcasted_iota(jnp.int32, (3 * FILTER_HIDDEN, n1), 1) < n1 // 2
    zero = jnp.zeros((), BF16)
    contract0 = (((0,), (0,)), ((), ()))
    cols = []
    for q in range(st):
        zt = zt_ref[q]
        a = hid_ref[q]
        a_hi = a.astype(BF16)
        a_lo = (a - a_hi.astype(F32)).astype(BF16)
        stack = jnp.concatenate([a_hi, a_hi, a_lo], axis=0)
        lhs = jnp.concatenate([jnp.where(first_half, stack, zero), jnp.where(first_half, zero, stack)], axis=0)
        h = lax.dot_general(lhs, w3_ref[...], contract0, preferred_element_type=F32)
        decay = lax.dot_general(zt.astype(BF16), e_ref[...], contract0, preferred_element_type=F32)
        h = h * jnp.exp(-decay)
        h = jnp.where((row == n1 // 2) & (j * st + q == 0), 0.0, h)
        cols.append(h.astype(BF16))
    o_ref[...] = jnp.dot(m1_ref[...], jnp.concatenate(cols, axis=1), preferred_element_type=F32).astype(BF16)


def _filter_outer_dft(zt, hid, w3r, e, m1):
    n_s, _, n1 = zt.shape
    width = w3r.shape[1]
    st = min(8, n_s)
    const = lambda a: pl.BlockSpec(a.shape, lambda j: (0,) * a.ndim)
    return pl.pallas_call(
        functools.partial(_filter_dft_kernel, st=st),
        out_shape=jax.ShapeDtypeStruct((m1.shape[0], n_s * width), BF16),
        grid=(n_s // st,),
        in_specs=[pl.BlockSpec((st, FEAT_PAD, n1), lambda j: (j, 0, 0)),
                  pl.BlockSpec((st, FILTER_HIDDEN, n1), lambda j: (j, 0, 0)), const(w3r), const(e), const(m1)],
        out_specs=pl.BlockSpec((m1.shape[0], st * width), lambda j: (0, j)),
        compiler_params=_cparams(("parallel",)),
        name="hyena_filter_outer_dft",
    )(zt, hid, w3r, e, m1)


def _mirror_hidden(hid_pairs, half):
    npair, nh, _ = hid_pairs.shape
    first = hid_pairs.reshape(npair, nh, 2, half).transpose(0, 2, 1, 3).reshape(2 * npair, nh, half)
    rev = first[:, :, ::-1]
    second = jnp.roll(rev[::-1], 1, axis=0)
    second0 = jnp.concatenate([first[0][:, 0:1], rev[0][:, :half - 1]], axis=-1)
    second = second.at[0].set(second0)
    return jnp.concatenate([first, second], axis=-1)


def _dft_consts(n):
    big = 2 * n
    n1 = big // DFT_INNER
    half = n1 // 2
    k1 = np.arange(n1)[:, None]
    a = np.arange(n1)[None, :]
    ang = -2.0 * np.pi * ((k1 * a) % n1) / n1
    fr, fi = np.cos(ang), np.sin(ang)
    first = np.block([[fr[:, :half], -fi[:, :half]], [fi[:, :half], fr[:, :half]]])
    first_real = np.concatenate([fr, fi], axis=0)
    last = np.block([[fr[:half, :], fi[:half, :]], [-fi[:half, :], fr[:half, :]]])
    k2 = np.arange(DFT_INNER)[:, None]
    s = np.arange(DFT_INNER)[None, :]
    ang2 = -2.0 * np.pi * ((k2 * s) % DFT_INNER) / DFT_INNER
    gr, gi = np.cos(ang2), np.sin(ang2)
    inner = np.block([[gr, -gi], [gi, gr]])
    inner_inv = np.block([[gr, gi], [-gi, gr]])
    prod = (jnp.arange(n1, dtype=jnp.int32)[:, None] * jnp.arange(DFT_INNER, dtype=jnp.int32)[None, :]) % big
    angt = prod.astype(F32) * (-2.0 * math.pi / big)
    tw = jnp.stack([jnp.cos(angt), jnp.sin(angt)])
    tw = jnp.broadcast_to(tw[..., None], tw.shape + (LANES,))
    f = lambda m: jnp.asarray(np.ascontiguousarray(m), F32).astype(BF16)
    return dict(first=f(first), first_real=f(first_real), last=f(last), inner=f(inner), inner_inv=f(inner_inv),
                tw=tw, n1=n1, inv_scale=1.0 / big)


def _left_matmul_kernel(*refs, mode, inv_scale):
    if mode == "first":
        m_ref, x_ref, o_ref = refs
        o_ref[...] = _bdot(m_ref[...], x_ref[0]).astype(BF16)
        return
    if mode == "last":
        m3_ref, b_ref, g_ref, v_ref, bias_ref, z_ref = refs
    else:
        m3_ref, b_ref, g_ref, v_ref, bias_ref, m1_ref, z_ref, a_ref = refs
    y = _bdot(m3_ref[...], b_ref[...]) * inv_scale
    z = g_ref[0].astype(F32) * (y + bias_ref[...] * v_ref[0].astype(F32))
    z_ref[0] = z.astype(z_ref.dtype)
    if mode == "last_first":
        a_ref[...] = _bdot(m1_ref[...], z).astype(BF16)


DFT_COLS = 2048


def _dft_first(m1, x3, part):
    _, rows, wtot = x3.shape
    wt = DFT_COLS
    return pl.pallas_call(
        functools.partial(_left_matmul_kernel, mode="first", inv_scale=None),
        out_shape=jax.ShapeDtypeStruct((m1.shape[0], wtot), BF16),
        grid=(wtot // wt,),
        in_specs=[pl.BlockSpec(m1.shape, lambda j: (0, 0)), pl.BlockSpec((1, rows, wt), lambda j: (part, 0, j))],
        out_specs=pl.BlockSpec((m1.shape[0], wt), lambda j: (0, j)),
        compiler_params=_cparams(("parallel",)),
        name="dft_outer_forward",
    )(m1, x3)


def _dft_last(m3, b2, gate3, gate_part, v3, v_part, bias_flat, inv_scale, m1=None):
    _, rows, wtot = gate3.shape
    wt = DFT_COLS
    col = lambda r: pl.BlockSpec((r, wt), lambda j: (0, j))
    part = lambda p: pl.BlockSpec((1, rows, wt), lambda j: (p, 0, j))
    const = lambda a: pl.BlockSpec(a.shape, lambda j: (0, 0))
    z_shape = jax.ShapeDtypeStruct((1, rows, wtot), BF16 if m1 is None else F32)
    in_specs = [const(m3), col(b2.shape[0]), part(gate_part), part(v_part), pl.BlockSpec((1, wt), lambda j: (0, 0))]
    if m1 is None:
        return pl.pallas_call(
            functools.partial(_left_matmul_kernel, mode="last", inv_scale=inv_scale),
            out_shape=z_shape,
            grid=(wtot // wt,),
            in_specs=in_specs,
            out_specs=part(0),
            compiler_params=_cparams(("parallel",)),
            name="dft_outer_inverse",
        )(m3, b2, gate3, v3, bias_flat)
    return pl.pallas_call(
        functools.partial(_left_matmul_kernel, mode="last_first", inv_scale=inv_scale),
        out_shape=(z_shape, jax.ShapeDtypeStruct((m1.shape[0], wtot), BF16)),
        grid=(wtot // wt,),
        in_specs=in_specs + [const(m1)],
        out_specs=(part(0), col(m1.shape[0])),
        compiler_params=_cparams(("parallel",)),
        name="dft_outer_inverse_forward",
    )(m3, b2, gate3, v3, bias_flat, m1)


def _dft_mid_kernel(a_ref, tw_ref, g_ref, *rest, slabs, with_filter):
    if with_filter:
        h_ref, gi_ref, o_ref = rest
    else:
        (o_ref,) = rest
    nl = a_ref.shape[3] // LANES
    for j in range(slabs):
        twr = jnp.concatenate([tw_ref[0, j]] * nl, axis=1)
        twi = jnp.concatenate([tw_ref[1, j]] * nl, axis=1)
        ar, ai = a_ref[0, j].astype(F32), a_ref[1, j].astype(F32)
        br = ar * twr - ai * twi
        bi = ar * twi + ai * twr
        x = _bdot(g_ref[...], jnp.concatenate([br.astype(BF16), bi.astype(BF16)], axis=0))
        if not with_filter:
            o_ref[0, j] = x[:DFT_INNER].astype(BF16)
            o_ref[1, j] = x[DFT_INNER:].astype(BF16)
            continue
        xr, xi = x[:DFT_INNER], x[DFT_INNER:]
        hr, hi = h_ref[0, j].astype(F32), h_ref[1, j].astype(F32)
        yr = xr * hr - xi * hi
        yi = xr * hi + xi * hr
        zz = _bdot(gi_ref[...], jnp.concatenate([yr.astype(BF16), yi.astype(BF16)], axis=0))
        zr, zi = zz[:DFT_INNER], zz[DFT_INNER:]
        o_ref[0, j] = (zr * twr + zi * twi).astype(BF16)
        o_ref[1, j] = (zi * twr - zr * twi).astype(BF16)


def _dft_mid(a4, tw, inner, spec=None, inner_inv=None, order=0):
    _, n1, _, c = a4.shape
    slabs = 8 if n1 % 8 == 0 else n1
    blk = lambda w: pl.BlockSpec((2, slabs, DFT_INNER, w), lambda j: (0, j, 0, 0))
    const = lambda a: pl.BlockSpec(a.shape, lambda j: (0, 0))
    in_specs = [blk(c), blk(LANES), const(inner)]
    args = [a4, tw, inner]
    if spec is not None:
        in_specs += [pl.BlockSpec((2, slabs, DFT_INNER, c), lambda j: (0, j, 0, order)), const(inner_inv)]
        args += [spec, inner_inv]
    return pl.pallas_call(
        functools.partial(_dft_mid_kernel, slabs=slabs, with_filter=spec is not None),
        out_shape=jax.ShapeDtypeStruct(a4.shape, BF16),
        grid=(n1 // slabs,),
        in_specs=in_specs,
        out_specs=blk(c),
        compiler_params=_cparams(("parallel",)),
        name="dft_inner_filter" if spec is not None else "dft_inner_forward",
    )(*args)


def _hyena_latent(vx, filt_a, bias, consts):
    _, nb, n, c = vx.shape
    assert nb == 2, "the two samples are carried as the real and imaginary part of one transform"
    n1 = consts["n1"]
    wflat = DFT_INNER * c
    flat = vx.reshape(3, nb * n1 // 2, wflat)
    bias_flat = jnp.tile(bias, (1, DFT_COLS // c))
    tw, inner, inner_inv, scale = consts["tw"], consts["inner"], consts["inner_inv"], consts["inv_scale"]

    spec = _dft_mid(filt_a.reshape(2, n1, DFT_INNER, HYENA_ORDER * c), tw, inner)

    a = _dft_first(consts["first"], flat, 0).reshape(2, n1, DFT_INNER, c)
    bm = _dft_mid(a, tw, inner, spec, inner_inv, order=0)
    z, a = _dft_last(consts["last"], bm.reshape(2 * n1, wflat), flat, 1, flat, 0, bias_flat[0:1], scale,
                     consts["first"])
    bm = _dft_mid(a.reshape(2, n1, DFT_INNER, c), tw, inner, spec, inner_inv, order=1)
    y = _dft_last(consts["last"], bm.reshape(2 * n1, wflat), flat, 2, z, 0, bias_flat[1:2], scale)
    return y.reshape(nb, n, c)


def _hyena_ctx_kernel(u_ref, h_ref, fs_ref, ff_ref, fi_ref, bias_ref, o_ref, *, n):
    big = 2 * n
    c = HYENA_WIDTH
    spec = _hdot(fs_ref[...], h_ref[...])
    u = u_ref[...]
    v, x1, x2 = u[:, 0:c], u[:, c:2 * c], u[:, 2 * c:3 * c]

    def conv(zin, order):
        x = _hdot(ff_ref[...], zin)
        xr, xi = x[:big], x[big:]
        hr, hi = spec[:big, order * c:(order + 1) * c], spec[big:, order * c:(order + 1) * c]
        y = jnp.concatenate([xr * hr - xi * hi, xr * hi + xi * hr], axis=0)
        return _hdot(fi_ref[...], y)

    bias = bias_ref[...]
    z = x1 * (conv(v, 0) + bias[0:1] * v)
    o_ref[...] = (x2 * (conv(z, 1) + bias[1:2] * z)).astype(BF16)


def _ctx_dft_consts(n):
    big = 2 * n
    k = np.arange(big)[:, None]
    t = np.arange(big)[None, :]
    ang = -2.0 * np.pi * ((k * t) % big) / big
    fr, fi = np.cos(ang), np.sin(ang)
    spec_m = np.concatenate([fr, fi], axis=0)
    fwd = np.block([[fr[:, :n], -fi[:, :n]], [fi[:, :n], fr[:, :n]]])
    inv = np.block([[fr.T[:n, :], fi.T[:n, :]], [-fi.T[:n, :], fr.T[:n, :]]]) / big
    f = lambda m: jnp.asarray(np.ascontiguousarray(m), F32)
    return f(spec_m), f(fwd), f(inv)


def _hyena_ctx(u_ctx, h_circ, bias, cconsts):
    nb, n, w = u_ctx.shape
    assert nb == 2
    fs, ff, fi = cconsts
    full = lambda a: pl.BlockSpec(a.shape, lambda i: (0,) * a.ndim)
    u2 = u_ctx.reshape(nb * n, w)
    out = pl.pallas_call(
        functools.partial(_hyena_ctx_kernel, n=n),
        out_shape=jax.ShapeDtypeStruct((nb * n, HYENA_WIDTH), BF16),
        grid=(1,),
        in_specs=[full(u2), full(h_circ), full(fs), full(ff), full(fi), full(bias)],
        out_specs=pl.BlockSpec((nb * n, HYENA_WIDTH), lambda i: (0, 0)),
        compiler_params=_cparams(("arbitrary",)),
        name="hyena_context",
    )(u2, h_circ, fs, ff, fi, bias)
    return out.reshape(nb, n, HYENA_WIDTH)


def _gla_kernel(xf_ref, xb_ref, of_ref, ob_ref, st_ref):
    s = pl.program_id(0)

    @pl.when(s == 0)
    def _():
        st_ref[...] = jnp.zeros(st_ref.shape, F32)

    c = GLA_CHUNK
    kw, vw = GLA_K_WIDTH, GLA_V_WIDTH
    ri3 = lax.broadcasted_iota(jnp.int32, (c, 3 * c), 0)
    ci3 = lax.broadcasted_iota(jnp.int32, (c, 3 * c), 1) % c
    kr = lax.broadcasted_iota(jnp.int32, (GLA_HEADS * c, kw), 0) // c
    kc = lax.broadcasted_iota(jnp.int32, (GLA_HEADS * c, kw), 1) // GLA_DK
    mask_k = kr == kc
    vr = lax.broadcasted_iota(jnp.int32, (GLA_HEADS * c, vw), 0) // c
    vc = lax.broadcasted_iota(jnp.int32, (GLA_HEADS * c, vw), 1) // GLA_DV
    mask_v = vr == vc
    sr = lax.broadcasted_iota(jnp.int32, (vw, kw), 0) // GLA_DV
    sc = lax.broadcasted_iota(jnp.int32, (vw, kw), 1) // GLA_DK
    mask_s = sr == sc
    qi = lax.broadcasted_iota(jnp.int32, (c, GLA_HEADS * c), 0)
    kj = lax.broadcasted_iota(jnp.int32, (c, GLA_HEADS * c), 1) % c

    def chunk(x_ref, o_ref, d, b, ch):
        backward = d == 1
        tri3 = ((ri3 <= ci3) if backward else (ri3 >= ci3)).astype(BF16)
        causal = (qi <= kj) if backward else (qi >= kj)
        goff = 2 * kw + vw + (kw if backward else 0)
        r0 = ch * c
        q = x_ref[b, r0:r0 + c, 0:kw]
        k = x_ref[b, r0:r0 + c, kw:2 * kw]
        v = x_ref[b, r0:r0 + c, 2 * kw:2 * kw + vw]
        g = x_ref[b, r0:r0 + c, goff:goff + kw]
        g_hi = g.astype(BF16)
        g_mid = (g - g_hi.astype(F32)).astype(BF16)
        g_lo = (g - g_hi.astype(F32) - g_mid.astype(F32)).astype(BF16)
        gc = jnp.dot(tri3, jnp.concatenate([g_hi, g_mid, g_lo], axis=0), preferred_element_type=F32)
        g_last = gc[0:1] if backward else gc[c - 1:c]
        q_dec = q * jnp.exp(gc)
        k_dec = k * jnp.exp(-gc)
        k_tail = k * jnp.exp(g_last - gc)
        k_blk = jnp.where(mask_k, jnp.concatenate([k_dec] * GLA_HEADS, axis=0), 0.0).astype(BF16)
        v_bf = v.astype(BF16)
        v_blk = jnp.where(mask_v, jnp.concatenate([v_bf] * GLA_HEADS, axis=0), jnp.zeros((), BF16))
        q_bf = q_dec.astype(BF16)
        scores = lax.dot_general(q_bf, k_blk, (((1,), (1,)), ((), ())), preferred_element_type=F32)
        scores = jnp.where(causal, scores, 0.0)
        st = st_ref[d, b]
        o = jnp.dot(scores.astype(BF16), v_blk, preferred_element_type=F32)
        o = o + lax.dot_general(q_bf, st.astype(BF16), (((1,), (1,)), ((), ())), preferred_element_type=F32)
        o_ref[b, r0:r0 + c, :] = o.astype(BF16)
        kv_t = lax.dot_general(v_bf, k_tail.astype(BF16), (((0,), (0,)), ((), ())), preferred_element_type=F32)
        st_ref[d, b] = st * jnp.exp(g_last) + jnp.where(mask_s, kv_t, 0.0)

    n_chunks = xf_ref.shape[1] // c
    for u in range(n_chunks):
        for b in range(xf_ref.shape[0]):
            chunk(xf_ref, of_ref, 0, b, u)
            chunk(xb_ref, ob_ref, 1, b, n_chunks - 1 - u)


def _gla(gla_in, *, ctx_len):
    nb, t, w = gla_in.shape
    tb = TOK_TILE
    nblk = t // tb
    cblk = ctx_len // tb
    fwd = lambda s: (0, s, 0)
    bwd = lambda s: (0, jnp.where(s < cblk, cblk - 1 - s, nblk - 1 - (s - cblk)), 0)
    out = jax.ShapeDtypeStruct((nb, t, GLA_V_WIDTH), BF16)
    return pl.pallas_call(
        _gla_kernel,
        out_shape=(out, out),
        grid=(nblk,),
        in_specs=[pl.BlockSpec((nb, tb, w), fwd), pl.BlockSpec((nb, tb, w), bwd)],
        out_specs=(pl.BlockSpec((nb, tb, GLA_V_WIDTH), fwd), pl.BlockSpec((nb, tb, GLA_V_WIDTH), bwd)),
        scratch_shapes=[pltpu.VMEM((2, nb, GLA_V_WIDTH, GLA_K_WIDTH), F32)],
        compiler_params=_cparams(("arbitrary",)),
        name="gla_scans",
    )(gla_in, gla_in)


def _out_kernel(h_ref, mod_ref, at_ref, hy_ref, gf_ref, gb_ref, gr_ref, gn_ref, gmat_ref, w_ref, o_ref, *,
                n_batch, ctx_tiles, tile_off, d_model):
    b, i = pl.program_id(0), pl.program_id(1)
    mod = _mod_row(mod_ref, i + tile_off < ctx_tiles, b, n_batch)
    gate = mod[:, 2 * d_model:3 * d_model]
    gmat = gmat_ref[...]
    gn = gn_ref[...]
    a0, a1 = ATTN_WIDTH, ATTN_WIDTH + HYENA_WIDTH
    ya = _head_rms(at_ref[0].astype(F32), gn[:, :a0], gmat)
    yh = _head_rms(hy_ref[0].astype(F32), gn[:, a0:a1], gmat)
    yg = _head_rms(gf_ref[0].astype(F32) + gb_ref[0].astype(F32), gn[:, a1:], gmat) * _silu(gr_ref[0].astype(F32))
    y = jnp.concatenate([ya, yh, yg], axis=1)
    o_ref[0] = h_ref[0] + gate * _bdot(y, w_ref[...])


def _out_projection(h, mod, attn, hy, gla_f, gla_b, gr, gn, gmat, w_out, *, ctx_len, skip_ctx):
    nb, t, d = h.shape
    tb = TOK_TILE
    off = ctx_len // tb if skip_ctx else 0
    kern = functools.partial(_out_kernel, n_batch=nb, ctx_tiles=ctx_len // tb, tile_off=off, d_model=d)
    const = lambda a: pl.BlockSpec(a.shape, lambda b, i: (0,) * a.ndim)
    tok = lambda w: pl.BlockSpec((1, tb, w), lambda b, i: (b, i + off, 0))
    return pl.pallas_call(
        kern,
        out_shape=jax.ShapeDtypeStruct((nb, t - off * tb, d), F32),
        grid=(nb, t // tb - off),
        in_specs=[tok(d), const(mod), tok(ATTN_WIDTH), tok(HYENA_WIDTH), tok(GLA_V_WIDTH), tok(GLA_V_WIDTH),
                  tok(GLA_V_WIDTH), const(gn), const(gmat), const(w_out)],
        out_specs=pl.BlockSpec((1, tb, d), lambda b, i: (b, i, 0)),
        compiler_params=_cparams(("parallel", "parallel")),
        name="out_projection",
    )(h, mod, attn, hy, gla_f, gla_b, gr, gn, gmat, w_out)


def _ffn_kernel(h_ref, mod_ref, g2_ref, w1_ref, w3_ref, w2_ref, fg_ref, o_ref, *, n_batch, ctx_tiles, tile_off,
                d_model, final):
    b, i = pl.program_id(0), pl.program_id(1)
    mod = _mod_row(mod_ref, i + tile_off < ctx_tiles, b, n_batch)
    shift, scale, gate = (mod[:, 3 * d_model:4 * d_model], mod[:, 4 * d_model:5 * d_model],
                          mod[:, 5 * d_model:6 * d_model])
    x = h_ref[0]
    xn = x * lax.rsqrt(jnp.mean(jnp.square(x), axis=-1, keepdims=True) + NORM_EPS) * g2_ref[...]
    u = (xn * (1.0 + scale) + shift).astype(BF16)
    a = jnp.dot(u, w1_ref[...], preferred_element_type=F32)
    c = jnp.dot(u, w3_ref[...], preferred_element_type=F32)
    hid = (_silu(a) * c).astype(BF16)
    y = x + gate * jnp.dot(hid, w2_ref[...], preferred_element_type=F32)
    if final:
        y = y * lax.rsqrt(jnp.mean(jnp.square(y), axis=-1, keepdims=True) + NORM_EPS) * fg_ref[...]
    o_ref[0] = y


def _ffn(h, mod, g2, w1, w3, w2, fg, *, ctx_len, lat_only, final):
    nb, t, d = h.shape
    tb = TOK_TILE
    off = ctx_len // tb if lat_only else 0
    kern = functools.partial(_ffn_kernel, n_batch=nb, ctx_tiles=ctx_len // tb, tile_off=off, d_model=d, final=final)
    const = lambda a: pl.BlockSpec(a.shape, lambda b, i: (0,) * a.ndim)
    tok = pl.BlockSpec((1, tb, d), lambda b, i: (b, i, 0))
    return pl.pallas_call(
        kern,
        out_shape=jax.ShapeDtypeStruct((nb, t, d), F32),
        grid=(nb, t // tb),
        in_specs=[tok, const(mod), const(g2), const(w1), const(w3), const(w2), const(fg)],
        out_specs=tok,
        compiler_params=_cparams(("parallel", "parallel")),
        name="swiglu_ffn",
    )(h, mod, g2, w1, w3, w2, fg)


def _rope_tables(n_lat, ctx_len):
    rows = n_lat // GRID_W
    row = jnp.broadcast_to(jnp.arange(rows, dtype=F32)[:, None], (rows, GRID_W)).reshape(-1)
    col = jnp.broadcast_to(jnp.arange(GRID_W, dtype=F32)[None, :], (rows, GRID_W)).reshape(-1)
    n_freq = HEAD_DIM // 4
    inv_freq = jnp.power(ROPE_THETA, -jnp.arange(n_freq, dtype=F32) / n_freq)
    ang = jnp.concatenate([row[:, None] * inv_freq, col[:, None] * inv_freq], axis=-1)
    cos, sin = jnp.cos(ang), jnp.sin(ang)
    cos = jnp.concatenate([jnp.ones((ctx_len, HEAD_DIM // 2), F32), cos], axis=0)
    sin = jnp.concatenate([jnp.zeros((ctx_len, HEAD_DIM // 2), F32), sin], axis=0)
    cosf = jnp.tile(cos, (1, 2 * LANES // HEAD_DIM))
    sinf = jnp.tile(jnp.concatenate([-sin, sin], axis=1), (1, LANES // HEAD_DIM))
    return cosf, sinf


def kernel(x, c, ctx, c_ctx, ada_w, ada_b, norm1_g, w_in, q_norm_g, k_norm_g, hy_conv_w, hy_conv_b, filt_w1, filt_b1, filt_w2, filt_b2, filt_w3, filt_freq, hy_bias, gla_gate_w, gla_gate_b, out_norm_g, w_out, norm2_g, ffn_w1, ffn_w3, ffn_w2, final_norm_g):
    nb, n_lat, d = x.shape
    ctx_len = ctx.shape[1]
    depth = w_in.shape[0]
    assert nb == 2 and ctx_len % TOK_TILE == 0 and n_lat % TOK_TILE == 0 and n_lat % DFT_INNER == 0

    h = jnp.concatenate([ctx, x], axis=1)
    cond = jnp.concatenate([c, c_ctx[None, :], jnp.zeros((8 - nb - 1, d), F32)], axis=0)
    mods = _modulation(cond, ada_w, ada_b)

    cosf, sinf = _rope_tables(n_lat, ctx_len)
    lane = np.arange(LANES)
    gmat = jnp.asarray(np.tile((lane[:, None] // HEAD_DIM) == (lane[None, :] // HEAD_DIM), (2, 1)), BF16)
    consts = _dft_consts(n_lat)
    cconsts = _ctx_dft_consts(ctx_len)
    zc_ctx = _filter_features(ctx_len)
    deltas = jnp.abs(jnp.linspace(HYENA_MIN_DECAY, HYENA_MAX_DECAY, HYENA_WIDTH, dtype=F32))
    deltas = jnp.tile(deltas[None, :], (1, HYENA_ORDER))
    n1 = consts["n1"]
    rows_sa = (DFT_INNER * jnp.arange(n1, dtype=jnp.int32)[None, :] + jnp.arange(DFT_INNER, dtype=jnp.int32)[:, None])
    zt_lat = _circular_features(rows_sa, n_lat).transpose(1, 0, 2)
    t_hi, t_mid, t_lo = _split3(zt_lat[:, 0, :])
    d_hi, d_mid, d_lo = _split3(deltas[0])
    zt_lat = jnp.concatenate([zt_lat] + [p[:, None, :] for p in (t_hi, t_hi, t_hi, t_mid, t_mid, t_lo)], axis=1)
    n_feat = zt_lat.shape[1]
    zt_lat = jnp.pad(zt_lat, ((0, 0), (0, FEAT_PAD - n_feat), (0, 0)))
    decay_rows = jnp.zeros((FEAT_PAD, deltas.shape[1]), F32).at[n_feat - 6:n_feat].set(
        jnp.stack([d_hi, d_mid, d_lo, d_hi, d_mid, d_hi])).astype(BF16)
    lanes_n1 = lambda vec: jnp.broadcast_to(vec[:, None], (vec.shape[0], n1))
    lane_i = jnp.arange(n1, dtype=jnp.int32)[None, :]
    pair_i = jnp.arange(DFT_INNER // 2, dtype=jnp.int32)[:, None]
    rows_pair = DFT_INNER * (lane_i % (n1 // 2)) + 2 * pair_i + lane_i // (n1 // 2)
    zt_pairs = _circular_features(rows_pair, n_lat).transpose(1, 0, 2)
    zt_pairs = jnp.pad(zt_pairs, ((0, 0), (0, FEAT_PAD - zt_pairs.shape[1]), (0, 0)))

    w_in_b = jnp.pad(w_in, ((0, 0), (0, 0), (0, IN_PAD - w_in.shape[2]))).astype(BF16)
    w_out_b = w_out.astype(BF16)
    w1_b, w3_b, w2_b = ffn_w1.astype(BF16), ffn_w3.astype(BF16), ffn_w2.astype(BF16)
    fg = final_norm_g.reshape(1, d)

    out = None
    for l in range(depth):
        last = l == depth - 1
        mod = mods[l]
        wg = jnp.zeros((LANES, 2 * GLA_K_WIDTH), F32)
        wg = wg.at[0:GLA_GATE_RANK, 0:GLA_K_WIDTH].set(gla_gate_w[l, 0])
        wg = wg.at[GLA_GATE_RANK:2 * GLA_GATE_RANK, GLA_K_WIDTH:].set(gla_gate_w[l, 1])
        bg = gla_gate_b[l].reshape(1, 2 * GLA_K_WIDTH)
        qg = jnp.tile(q_norm_g[l][None, :], (1, ATTN_HEADS))
        kg = jnp.tile(k_norm_g[l][None, :], (1, ATTN_KV_HEADS))

        q, k, v, hy, gla_in, gr = _in_projection(h, mod, norm1_g[l].reshape(1, d), w_in_b[l], qg, kg, cosf, sinf,
                                                 gmat, wg.astype(BF16), bg, ctx_len=ctx_len)
        attn = _attention(q, k, v, ctx_len=ctx_len)

        vx, u_ctx = _short_conv(hy, hy_conv_w[l], hy_conv_b[l], ctx_len=ctx_len)
        w1p = jnp.pad(filt_w1[l], ((0, LANES - filt_w1.shape[1]), (0, 0)))
        w3r = filt_w3[l].reshape(FILTER_HIDDEN, HYENA_ORDER, 2, HYENA_WIDTH).transpose(2, 0, 1, 3)
        w3r = w3r.reshape(2, FILTER_HIDDEN, HYENA_ORDER * HYENA_WIDTH)
        fargs = (w1p, filt_b1[l][None, :], filt_w2[l], filt_b2[l][None, :], w3r, filt_freq[l][None, :], deltas)
        w1t = jnp.pad(filt_w1[l].T, ((0, 0), (0, FEAT_PAD - filt_w1.shape[1])))
        w3_hi, w3_lo, _ = _split3(w3r)
        w3_stack = jnp.concatenate([w3_hi, w3_lo, w3_hi], axis=1).reshape(-1, w3r.shape[2]).astype(BF16)
        hid = _filter_hidden(zt_pairs, w1t, lanes_n1(filt_b1[l]), filt_w2[l].T, lanes_n1(filt_b2[l]),
                             lanes_n1(filt_freq[l]))
        filt_a = _filter_outer_dft(zt_lat, _mirror_hidden(hid, n1 // 2), w3_stack, decay_rows, consts["first_real"])
        hy_lat = _hyena_latent(vx, filt_a, hy_bias[l], consts)
        if not last:
            h_ctx = _hyena_filter(zc_ctx, *fargs, n=ctx_len)
            hy_ctx = _hyena_ctx(u_ctx, h_ctx, hy_bias[l], cconsts)
        else:
            hy_ctx = jnp.zeros((nb, ctx_len, HYENA_WIDTH), BF16)
        hy_out = jnp.concatenate([hy_ctx, hy_lat], axis=1)

        gla_f, gla_b = _gla(gla_in, ctx_len=ctx_len)

        h = _out_projection(h, mod, attn, hy_out, gla_f, gla_b, gr, out_norm_g[l].reshape(1, -1), gmat, w_out_b[l],
                            ctx_len=ctx_len, skip_ctx=last)
        h = _ffn(h, mod, norm2_g[l].reshape(1, d), w1_b[l], w3_b[l], w2_b[l], fg, ctx_len=ctx_len, lat_only=last,
                 final=last)
    return h
```
